```python
import math
import jax
import jax.numpy as jnp
from jax import lax
import numpy as np

D_MODEL = 1024
BATCH = 4
SEQ = 8192
DEPTH = 4

GRID_W = 64
CTX_LEN = 256
N_MIXERS = 3
LN_EPS = 1e-5

HY_ORDER = 2
HY_DIRS = 2
HY_EMB = 33
HY_BANDS = (HY_EMB - 1) // 2
HY_FF = 64
HY_SHORT = 3
HY_DECAY_TARGET = 1e-2
HY_DECAY_SHORT_PCT = 0.3
HY_DECAY_LONG_PCT = 1.5

GLA_HEADS = 4
GLA_DK = D_MODEL // 2
GLA_DV = D_MODEL
GLA_RANK = 16
GLA_TAU = 16.0
GLA_CHUNK = 64

GM_CHUNK = 128
GM_HEADS = 4

N_EXPERTS = 32
N_GROUPS = 4
TOP_K = 2
D_EXPERT = 512
MOE_BLOCK = 128

kernel_name = "hybrid_hyena_gla_gmlp_moe_flow_backbone"


def layer_norm(x, g, b):
    xf = x.astype(jnp.float32)
    mu = jnp.mean(xf, -1, keepdims=True)
    var = jnp.mean(jnp.square(xf - mu), -1, keepdims=True)
    return ((xf - mu) * lax.rsqrt(var + LN_EPS) * g + b).astype(x.dtype)


def sincos_2d(rows, cols, d):
    q = d // 4
    omega = 1.0 / (10000.0 ** (jnp.arange(q, dtype=jnp.float32) / q))
    pr = jnp.arange(rows, dtype=jnp.float32)[:, None] * omega
    pc = jnp.arange(cols, dtype=jnp.float32)[:, None] * omega
    er = jnp.concatenate([jnp.sin(pr), jnp.cos(pr)], -1)
    ec = jnp.concatenate([jnp.sin(pc), jnp.cos(pc)], -1)
    emb = jnp.concatenate([jnp.broadcast_to(er[:, None, :], (rows, cols, 2 * q)),
                           jnp.broadcast_to(ec[None, :, :], (rows, cols, 2 * q))], -1)
    return emb.reshape(rows * cols, 4 * q)


def depthwise_conv_centred(u, w, b):
    k = w.shape[0]
    y = lax.conv_general_dilated(u, w[:, None, :].astype(u.dtype), (1,), [(k // 2, k // 2)],
                                 dimension_numbers=("NWC", "WIO", "NWC"),
                                 feature_group_count=u.shape[-1])
    return y + b


def hyena_filters(L, w1, b1, w2, b2, w3, b3, w4):
    f32 = jnp.float32
    t = jnp.linspace(0.0, 1.0, L, dtype=f32)[:, None]
    wpos = 2.0 * math.pi * jnp.arange(L, dtype=f32)[:, None] / L
    bands = jnp.linspace(1e-4, HY_BANDS - 1, HY_BANDS, dtype=f32)[None, :]
    z = jnp.concatenate([t, jnp.cos(bands * wpos), -jnp.sin(bands * wpos)], -1)
    hf = jnp.sin(z @ w1 + b1)
    hf = jnp.sin(hf @ w2 + b2)
    hf = jnp.sin(hf @ w3 + b3)
    hf = (hf @ w4).astype(f32).reshape(L, HY_ORDER, HY_DIRS, -1)
    d = hf.shape[-1]
    max_decay = math.log(HY_DECAY_TARGET) / HY_DECAY_SHORT_PCT
    min_decay = math.log(HY_DECAY_TARGET) / HY_DECAY_LONG_PCT
    deltas = jnp.abs(jnp.linspace(min_decay, max_decay, d, dtype=f32))
    window = jnp.exp(-t * deltas)
    return hf * window[:, None, None, :]


def bidir_fftconv(u, h_fwd, h_bwd):
    L = u.shape[1]
    taps = jnp.concatenate([h_fwd[:1] + h_bwd[:1], h_fwd[1:],
                            jnp.zeros_like(h_fwd[:1]), h_bwd[:0:-1]], 0)
    kf = jnp.fft.rfft(taps, axis=0)
    uf = jnp.fft.rfft(u.astype(jnp.float32), n=2 * L, axis=1)
    return jnp.fft.irfft(uf * kf[None], n=2 * L, axis=1)[:, :L]


def hyena_mixer(a, w_in, b_in, conv_w, conv_b, f_w1, f_b1, f_w2, f_b2, f_w3, f_b3, f_w4,
                skip, w_out, b_out):
    L = a.shape[1]
    z = depthwise_conv_centred(a @ w_in + b_in, conv_w, conv_b)
    v, x1, x2 = jnp.split(z.astype(jnp.float32), 3, axis=-1)
    filt = hyena_filters(L, f_w1, f_b1, f_w2, f_b2, f_w3, f_b3, f_w4)
    y = v
    for n, gate in enumerate((x1, x2)):
        y = gate * (bidir_fftconv(y, filt[:, n, 0], filt[:, n, 1]) + y * skip[n])
    return y.astype(a.dtype) @ w_out + b_out


def gla_project(a, w_in, w_gate, b_gate):
    B, L, _ = a.shape
    z = a @ w_in
    dk, dv = GLA_DK, GLA_DV
    q = z[..., :dk]
    k = z[..., dk:2 * dk]
    v = z[..., 2 * dk:2 * dk + dv]
    r = z[..., 2 * dk + dv:2 * dk + 2 * dv]
    lr = z[..., 2 * dk + 2 * dv:].reshape(B, L, HY_DIRS, GLA_RANK)
    glog = jax.nn.log_sigmoid((jnp.einsum('bldr,drk->bldk', lr, w_gate) + b_gate).astype(jnp.float32)) / GLA_TAU

    def heads(t):
        return t.reshape(B, L, GLA_HEADS, -1).astype(jnp.float32)

    dkh = dk // GLA_HEADS
    return (heads(q) * (dkh ** -0.5), heads(k), heads(v), r,
            heads(glog[:, :, 0]), heads(glog[:, :, 1]))


def gla_chunked(q, k, v, g, s0):
    B, L, H, _ = q.shape
    C = GLA_CHUNK
    nc = L // C

    def to_chunks(t):
        return t.reshape(B, nc, C, H, t.shape[-1]).transpose(1, 0, 3, 2, 4)

    mask = jnp.tril(jnp.ones((C, C), bool))

    def step(s, inp):
        qi, ki, vi, gi = inp
        b = jnp.cumsum(gi, axis=2)
        diff = jnp.where(mask[:, :, None], b[:, :, :, None, :] - b[:, :, None, :, :], -jnp.inf)
        att = jnp.einsum('bhtd,bhsd,bhtsd->bhts', qi, ki, jnp.exp(diff))
        o = jnp.einsum('bhts,bhsv->bhtv', att, vi) + jnp.einsum('bhtd,bhdv->bhtv', qi * jnp.exp(b), s)
        b_last = b[:, :, -1:, :]
        s = s * jnp.exp(b_last[:, :, 0, :, None]) + jnp.einsum('bhsd,bhsv->bhdv', ki * jnp.exp(b_last - b), vi)
        return s, o

    s, o = lax.scan(step, s0, (to_chunks(q), to_chunks(k), to_chunks(v), to_chunks(g)))
    o = o.transpose(1, 0, 3, 2, 4).reshape(B, L, H, v.shape[-1])
    return o, s


def gla_final_state(k, v, g):
    b = jnp.cumsum(g, axis=1)
    return jnp.einsum('blhd,blhv->bhdv', k * jnp.exp(b[:, -1:] - b), v)


def gla_output(o, r, norm_g, w_out):
    B, L, H, dvh = o.shape
    mu = jnp.mean(o, -1, keepdims=True)
    var = jnp.mean(jnp.square(o - mu), -1, keepdims=True)
    on = (o - mu) * lax.rsqrt(var + LN_EPS) * norm_g.reshape(H, dvh)
    y = on.reshape(B, L, H * dvh) * jax.nn.silu(r.astype(jnp.float32))
    return y.astype(r.dtype) @ w_out


def gla_mixer(a, ac, w_in, w_gate, b_gate, norm_g, w_out, ctx_out):
    def flip(t):
        return t[:, ::-1]

    q, k, v, r, gf, gb = gla_project(a, w_in, w_gate, b_gate)
    qc, kc, vc, rc, gfc, gbc = gla_project(ac, w_in, w_gate, b_gate)
    if ctx_out:
        zero = jnp.zeros((ac.shape[0], GLA_HEADS, GLA_DK // GLA_HEADS, GLA_DV // GLA_HEADS), jnp.float32)
        ocf, s_f = gla_chunked(qc, kc, vc, gfc, zero)
        ocb, s_b = gla_chunked(flip(qc), flip(kc), flip(vc), flip(gbc), zero)
        yc = gla_output(ocf + flip(ocb), rc, norm_g, w_out)
    else:
        s_f = gla_final_state(kc, vc, gfc)
        s_b = gla_final_state(flip(kc), flip(vc), flip(gbc))
        yc = None
    of, _ = gla_chunked(q, k, v, gf, s_f)
    ob, _ = gla_chunked(flip(q), flip(k), flip(v), flip(gb), s_b)
    y = gla_output(of + flip(ob), r, norm_g, w_out)
    return y, yc


def gmlp_mixer(a, w_in, b_in, ln_g, ln_b, ws, bs, w_out, b_out):
    B, L, D = a.shape
    z = jax.nn.gelu(a @ w_in + b_in)
    u, v = jnp.split(z, 2, axis=-1)
    v = layer_norm(v, ln_g, ln_b)
    nc = L // GM_CHUNK
    vc = v.reshape(B, nc, GM_CHUNK, GM_HEADS, D // GM_HEADS)
    vm = jnp.einsum('gts,bnsgc->bntgc', ws, vc) + bs.T[None, None, :, :, None]
    return (u * vm.reshape(B, L, D)) @ w_out + b_out


def route(tok, router_w, router_b):
    n = tok.shape[0]
    gsz = N_EXPERTS // N_GROUPS
    s = jax.nn.sigmoid((tok @ router_w).astype(jnp.float32))
    grp = (s + router_b).reshape(n, N_GROUPS, gsz)
    gscore = jnp.sum(lax.top_k(grp, TOP_K)[0], -1)
    g = jnp.argmax(gscore, -1)
    in_grp = grp[jnp.arange(n), g]
    _, loc = lax.top_k(in_grp, TOP_K)
    eid = g[:, None] * gsz + loc
    w = jnp.take_along_axis(s, eid, axis=1)
    return eid, w / jnp.sum(w, -1, keepdims=True)


def moe_layer(tok, router_w, router_b, w1, w3, w2):
    n, d = tok.shape
    e_n = w1.shape[0]
    eid, w = route(tok, router_w, router_b)
    nk = n * TOP_K
    e_flat = eid.reshape(-1)
    tok_flat = jnp.repeat(jnp.arange(n, dtype=jnp.int32), TOP_K)
    order = jnp.argsort(e_flat)
    e_s, tok_s, w_s = e_flat[order], tok_flat[order], w.reshape(-1)[order]
    counts = jnp.bincount(e_flat, length=e_n)
    start = jnp.cumsum(counts) - counts
    padded = (counts + MOE_BLOCK - 1) // MOE_BLOCK * MOE_BLOCK
    pend = jnp.cumsum(padded)
    pstart = pend - padded
    dest = pstart[e_s] + jnp.arange(nk, dtype=jnp.int32) - start[e_s]
    p = (nk + MOE_BLOCK - 1) // MOE_BLOCK * MOE_BLOCK + e_n * MOE_BLOCK
    nblk = p // MOE_BLOCK
    buf_tok = jnp.full((p,), n, jnp.int32).at[dest].set(tok_s)
    buf_w = jnp.zeros((p,), w.dtype).at[dest].set(w_s)
    blk_e = jnp.clip(jnp.searchsorted(pend, jnp.arange(nblk) * MOE_BLOCK, side='right'), 0, e_n - 1)
    x_pad = jnp.concatenate([tok, jnp.zeros((1, d), tok.dtype)], 0)
    xb = x_pad[buf_tok].reshape(nblk, MOE_BLOCK, d)

    def expert_block(args):
        xi, e = args
        return (jax.nn.silu(xi @ w1[e]) * (xi @ w3[e])) @ w2[e]

    yb = lax.map(expert_block, (xb, blk_e)).reshape(p, d)
    out = jnp.zeros((n + 1, d), jnp.float32).at[buf_tok].add(yb.astype(jnp.float32) * buf_w[:, None])
    return out[:n].astype(tok.dtype)


def setup_inputs(seed: int = 0) -> dict:
    key = jax.random.key(seed)
    keys = iter(jax.random.split(key, 64))
    f32 = jnp.float32
    D = D_MODEL
    beta = (8.0 * DEPTH) ** -0.25
    n_a = len(range(0, DEPTH, N_MIXERS))
    n_b = len(range(1, DEPTH, N_MIXERS))
    n_c = len(range(2, DEPTH, N_MIXERS))
    hy_ch = HY_ORDER * HY_DIRS * D
    gla_in = 2 * GLA_DK + 2 * GLA_DV + HY_DIRS * GLA_RANK

    def nrm(shape, scale):
        return jax.random.normal(next(keys), shape, f32) * scale

    return {
        "x": nrm((BATCH, SEQ, D), 1.0),
        "c": nrm((BATCH, D), 1.0),
        "ctx": nrm((BATCH, CTX_LEN, D), 1.0),
        "c_ctx": nrm((D,), 1.0),
        "w_mod": nrm((DEPTH, D, 6 * D), 0.5 * D ** -0.5),
        "b_mod": nrm((DEPTH, 6 * D), 0.01),
        "ln_g": 1.0 + nrm((DEPTH, 2, D), 0.05),
        "ln_b": nrm((DEPTH, 2, D), 0.02),
        "hy_w_in": nrm((n_a, D, 3 * D), D ** -0.5),
        "hy_b_in": nrm((n_a, 3 * D), 0.02),
        "hy_conv_w": nrm((n_a, HY_SHORT, 3 * D), HY_SHORT ** -0.5),
        "hy_conv_b": nrm((n_a, 3 * D), 0.02),
        "hy_f_w1": nrm((n_a, HY_EMB, HY_FF), HY_EMB ** -0.5),
        "hy_f_b1": nrm((n_a, HY_FF), 0.1),
        "hy_f_w2": nrm((n_a, HY_FF, HY_FF), HY_FF ** -0.5),
        "hy_f_b2": nrm((n_a, HY_FF), 0.1),
        "hy_f_w3": nrm((n_a, HY_FF, HY_FF), HY_FF ** -0.5),
        "hy_f_b3": nrm((n_a, HY_FF), 0.1),
        "hy_f_w4": nrm((n_a, HY_FF, hy_ch), 0.02),
        "hy_skip": nrm((n_a, HY_ORDER, D), 0.5),
        "hy_w_out": nrm((n_a, D, D), beta * D ** -0.5),
        "hy_b_out": nrm((n_a, D), 0.02),
        "gla_w_in": nrm((n_b, D, gla_in), D ** -0.5),
        "gla_w_gate": nrm((n_b, HY_DIRS, GLA_RANK, GLA_DK), GLA_RANK ** -0.5),
        "gla_b_gate": nrm((n_b, HY_DIRS, GLA_DK), 0.1),
        "gla_norm_g": 1.0 + nrm((n_b, GLA_DV), 0.05),
        "gla_w_out": nrm((n_b, GLA_DV, D), beta * GLA_DV ** -0.5),
        "gm_w_in": nrm((n_c, D, 2 * D), D ** -0.5),
        "gm_b_in": nrm((n_c, 2 * D), 0.02),
        "gm_ln_g": 1.0 + nrm((n_c, D), 0.05),
        "gm_ln_b": nrm((n_c, D), 0.02),
        "gm_ws": nrm((n_c, GM_HEADS, GM_CHUNK, GM_CHUNK), GM_CHUNK ** -0.5),
        "gm_bs": 1.0 + nrm((n_c, GM_HEADS, GM_CHUNK), 0.02),
        "gm_w_out": nrm((n_c, D, D), beta * D ** -0.5),
        "gm_b_out": nrm((n_c, D), 0.02),
        "router_w": nrm((D, N_EXPERTS), D ** -0.5),
        "router_b": nrm((N_EXPERTS,), 0.01),
        "moe_w1": nrm((DEPTH, N_EXPERTS, D, D_EXPERT), D ** -0.5),
        "moe_w3": nrm((DEPTH, N_EXPERTS, D, D_EXPERT), D ** -0.5),
        "moe_w2": nrm((DEPTH, N_EXPERTS, D_EXPERT, D), beta * D_EXPERT ** -0.5),
    }


def reference(x, c, ctx, c_ctx, w_mod, b_mod, ln_g, ln_b,
              hy_w_in, hy_b_in, hy_conv_w, hy_conv_b, hy_f_w1, hy_f_b1, hy_f_w2, hy_f_b2,
              hy_f_w3, hy_f_b3, hy_f_w4, hy_skip, hy_w_out, hy_b_out,
              gla_w_in, gla_w_gate, gla_b_gate, gla_norm_g, gla_w_out,
              gm_w_in, gm_b_in, gm_ln_g, gm_ln_b, gm_ws, gm_bs, gm_w_out, gm_b_out,
              router_w, router_b, moe_w1, moe_w3, moe_w2):
    alpha = (2.0 * DEPTH) ** 0.25
    B, L, D = x.shape
    rows = L // GRID_W
    h = x + sincos_2d(rows, GRID_W, D).astype(x.dtype)[None]
    hc = ctx
    gla_layers = list(range(1, DEPTH, N_MIXERS))
    last_ctx = gla_layers[-1] if gla_layers else -1
    s_lat = jax.nn.silu(c)
    s_ctx = jax.nn.silu(c_ctx)
    yc = None
    cm = None
    ac = None
    for i in range(DEPTH):
        kind, j = i % N_MIXERS, i // N_MIXERS
        ctx_full = i < last_ctx
        m = jnp.split(s_lat @ w_mod[i] + b_mod[i], 6, axis=-1)
        sh1, sc1, g1, sh2, sc2, g2 = [t[:, None, :] for t in m]
        a = h * (1.0 + sc1) + sh1
        if i <= last_ctx:
            cm = jnp.split(s_ctx @ w_mod[i] + b_mod[i], 6, axis=-1)
            ac = hc * (1.0 + cm[1]) + cm[0]
        if kind == 0:
            hy = (hy_w_in[j], hy_b_in[j], hy_conv_w[j], hy_conv_b[j], hy_f_w1[j], hy_f_b1[j],
                  hy_f_w2[j], hy_f_b2[j], hy_f_w3[j], hy_f_b3[j], hy_f_w4[j], hy_skip[j],
                  hy_w_out[j], hy_b_out[j])
            y = hyena_mixer(a, *hy)
            if ctx_full:
                yc = hyena_mixer(ac, *hy)
        elif kind == 1:
            y, yc = gla_mixer(a, ac, gla_w_in[j], gla_w_gate[j], gla_b_gate[j], gla_norm_g[j],
                              gla_w_out[j], ctx_full)
        else:
            gm = (gm_w_in[j], gm_b_in[j], gm_ln_g[j], gm_ln_b[j], gm_ws[j], gm_bs[j],
                  gm_w_out[j], gm_b_out[j])
            y = gmlp_mixer(a, *gm)
            if ctx_full:
                yc = gmlp_mixer(ac, *gm)
        h = layer_norm(alpha * h + g1 * y, ln_g[i, 0], ln_b[i, 0])
        a = h * (1.0 + sc2) + sh2
        if ctx_full:
            hc = layer_norm(alpha * hc + cm[2] * yc, ln_g[i, 0], ln_b[i, 0])
            ac = hc * (1.0 + cm[4]) + cm[3]
            tok = jnp.concatenate([a.reshape(-1, D), ac.reshape(-1, D)], 0)
        else:
            tok = a.reshape(-1, D)
        f = moe_layer(tok, router_w, router_b, moe_w1[i], moe_w3[i], moe_w2[i])
        n_lat = B * L
        h = layer_norm(alpha * h + g2 * f[:n_lat].reshape(B, L, D), ln_g[i, 1], ln_b[i, 1])
        if ctx_full:
            hc = layer_norm(alpha * hc + cm[5] * f[n_lat:].reshape(hc.shape), ln_g[i, 1], ln_b[i, 1])
    return h
```

```python
import functools
import math

import jax
import jax.numpy as jnp
import numpy as np
from jax import lax
from jax.experimental import pallas as pl
from jax.experimental.pallas import tpu as pltpu

F32 = jnp.float32
BF16 = jnp.bfloat16
I32 = jnp.int32
HIGHEST = lax.Precision.HIGHEST

GRID_W = 64
N_MIXERS = 3
LN_EPS = 1e-5
HY_ORDER = 2
HY_DIRS = 2
HY_DECAY_TARGET = 1e-2
HY_DECAY_SHORT_PCT = 0.3
HY_DECAY_LONG_PCT = 1.5
GLA_HEADS = 4
GLA_RANK = 16
GLA_TAU = 16.0
GLA_CHUNK = 64
GM_CHUNK = 128
GM_HEADS = 4
N_GROUPS = 4
TOP_K = 2

LANES = 128
V7X_VMEM_LIMIT_BYTES = 56 * 1024 * 1024
MOE_ROWS = 256
ROW_TILE = 512
DMA_TILE = 256
FFT_N2 = 128
FFT_COLS = 2048


def _params(n_grid):
    return pltpu.CompilerParams(dimension_semantics=("arbitrary",) * n_grid,
                                vmem_limit_bytes=V7X_VMEM_LIMIT_BYTES)


def _dot(a, b):
    return jnp.dot(a, b, preferred_element_type=F32)


def _ln(x, g, b):
    mu = jnp.mean(x, -1, keepdims=True)
    xc = x - mu
    var = jnp.mean(xc * xc, -1, keepdims=True)
    return xc * lax.rsqrt(var + LN_EPS) * g + b


def _silu(x):
    return x * jax.nn.sigmoid(x)


def _gelu_tanh(x):
    return 0.5 * x * (1.0 + jnp.tanh(math.sqrt(2.0 / math.pi) * (x + 0.044715 * (x * x * x))))


def _mod_kernel(c_ref, w_ref, b_ref, o_ref):
    s = _silu(c_ref[...])
    o_ref[0] = _dot(s.astype(BF16), w_ref[0].astype(BF16)) + b_ref[0]


def _modulation(cvec, w_mod, b_mod):
    depth, d, n = w_mod.shape
    tn = n // 4
    return pl.pallas_call(
        _mod_kernel,
        out_shape=jax.ShapeDtypeStruct((depth, 8, n), F32),
        grid=(depth, n // tn),
        in_specs=[pl.BlockSpec((8, d), lambda i, j: (0, 0)),
                  pl.BlockSpec((1, d, tn), lambda i, j: (i, 0, j)),
                  pl.BlockSpec((1, 1, tn), lambda i, j: (i, 0, j))],
        out_specs=pl.BlockSpec((1, 8, tn), lambda i, j: (i, 0, j)),
        compiler_params=_params(2), name="modulation",
    )(cvec, w_mod, b_mod.reshape(depth, 1, n))


def _pos_table_kernel(o_ref, *, q):
    rows, cols = o_ref.shape
    p = lax.broadcasted_iota(I32, (rows, cols), 0).astype(F32)
    lane = lax.broadcasted_iota(I32, (rows, cols), 1)
    j = jnp.where(lane >= q, lane - q, lane).astype(F32)
    omega = jnp.exp(j * (-math.log(10000.0) / q))
    ang = p * omega
    o_ref[...] = jnp.where(lane >= q, jnp.cos(ang), jnp.sin(ang))


def _pos_table(n, d):
    q = d // 4
    return pl.pallas_call(functools.partial(_pos_table_kernel, q=q),
                          out_shape=jax.ShapeDtypeStruct((n, 2 * q), F32), name="pos_table")()


def _add_pos_kernel(x_ref, er_ref, ec_ref, o_ref, *, half):
    tm = x_ref.shape[0]
    reps = tm // GRID_W
    er = er_ref[...]
    er_rows = jnp.broadcast_to(er[:, None, :], (reps, GRID_W, half)).reshape(tm, half)
    ec_rows = jnp.broadcast_to(ec_ref[...][None], (reps, GRID_W, half)).reshape(tm, half)
    o_ref[:, :half] = x_ref[:, :half] + er_rows
    o_ref[:, half:] = x_ref[:, half:] + ec_rows


def _add_pos(x2d, seq):
    n, d = x2d.shape
    half = d // 2
    rows = seq // GRID_W
    er = _pos_table(rows, d)
    ec = _pos_table(GRID_W, d)
    tm = ROW_TILE
    reps = tm // GRID_W
    tps = seq // tm
    return pl.pallas_call(
        functools.partial(_add_pos_kernel, half=half),
        out_shape=jax.ShapeDtypeStruct((n, d), F32),
        grid=(n // tm,),
        in_specs=[pl.BlockSpec((tm, d), lambda i: (i, 0)),
                  pl.BlockSpec((reps, half), lambda i: (i % tps, 0)),
                  pl.BlockSpec((GRID_W, half), lambda i: (0, 0))],
        out_specs=pl.BlockSpec((tm, d), lambda i: (i, 0)),
        compiler_params=_params(1), name="add_pos",
    )(x2d, er, ec)


def _gm_in_kernel(h_ref, sc_ref, sh_ref, w_ref, b_ref, g_ref, bb_ref, u_ref, v_ref):
    d = h_ref.shape[1]
    a = (h_ref[...] * (1.0 + sc_ref[0]) + sh_ref[0]).astype(BF16)
    u_ref[...] = _gelu_tanh(_dot(a, w_ref[:, :d]) + b_ref[:, :d])
    v = _gelu_tanh(_dot(a, w_ref[:, d:]) + b_ref[:, d:])
    v_ref[...] = _ln(v, g_ref[...], bb_ref[...]).astype(BF16)


def _gm_in(h, sc, sh, w, b, g, bb, seq):
    n, d = h.shape
    tm = min(ROW_TILE, seq)
    tps = seq // tm
    row = lambda i: (i, 0)
    per_b = lambda i: (i // tps, 0, 0)
    fixed = lambda i: (0, 0)
    return pl.pallas_call(
        _gm_in_kernel,
        out_shape=(jax.ShapeDtypeStruct((n, d), F32), jax.ShapeDtypeStruct((n, d), BF16)),
        grid=(n // tm,),
        in_specs=[pl.BlockSpec((tm, d), row), pl.BlockSpec((1, 1, d), per_b), pl.BlockSpec((1, 1, d), per_b),
                  pl.BlockSpec((d, 2 * d), fixed), pl.BlockSpec((1, 2 * d), fixed),
                  pl.BlockSpec((1, d), fixed), pl.BlockSpec((1, d), fixed)],
        out_specs=(pl.BlockSpec((tm, d), row), pl.BlockSpec((tm, d), row)),
        compiler_params=_params(1), name="gmlp_in",
    )(h, sc, sh, w, b, g, bb)


def _gm_prologue(u_ref, v_ref, ws_ref, bs_ref):
    tm, d = u_ref.shape
    dh = d // GM_HEADS
    for c in range(tm // GM_CHUNK):
        rows = slice(c * GM_CHUNK, (c + 1) * GM_CHUNK)
        parts = [_dot(ws_ref[g], v_ref[rows, g * dh:(g + 1) * dh]) for g in range(GM_HEADS)]
        vm = jnp.concatenate(parts, axis=1) + bs_ref[...]
        yield rows, (u_ref[rows, :] * vm).astype(BF16)


def _gla_prologue(of_ref, ob_ref, r_ref, ng_ref):
    tm, d = of_ref.shape
    dh = d // GLA_HEADS
    o = of_ref[...] + ob_ref[...]
    parts = []
    for hd in range(GLA_HEADS):
        oh = o[:, hd * dh:(hd + 1) * dh]
        mu = jnp.mean(oh, -1, keepdims=True)
        oc = oh - mu
        var = jnp.mean(oc * oc, -1, keepdims=True)
        parts.append(oc * lax.rsqrt(var + LN_EPS) * ng_ref[:, hd * dh:(hd + 1) * dh])
    y = jnp.concatenate(parts, axis=1) * _silu(r_ref[...])
    yield slice(0, tm), y.astype(BF16)


def _hy_prologue(y_ref):
    yield slice(0, y_ref.shape[0]), y_ref[...]


_PROLOGUES = {"gmlp": (_gm_prologue, 4), "gla": (_gla_prologue, 4), "hyena": (_hy_prologue, 1)}


def _route(logits, rb, n_experts):
    tm = logits.shape[0]
    gsz = n_experts // N_GROUPS
    lane = lax.broadcasted_iota(I32, (tm, LANES), 1)
    neg = jnp.float32(-jnp.inf)
    s = jax.nn.sigmoid(logits)
    biased = jnp.where(lane < n_experts, s + rb, neg)
    best_score = None
    for g in range(N_GROUPS):
        in_g = (lane >= g * gsz) & (lane < (g + 1) * gsz)
        vals = jnp.where(in_g, biased, neg)
        m1 = jnp.max(vals, -1, keepdims=True)
        i1 = jnp.min(jnp.where(vals == m1, lane, LANES), -1, keepdims=True)
        vals2 = jnp.where(lane == i1, neg, vals)
        m2 = jnp.max(vals2, -1, keepdims=True)
        i2 = jnp.min(jnp.where(vals2 == m2, lane, LANES), -1, keepdims=True)
        score = m1 + m2
        if best_score is None:
            best_score, e0, e1 = score, i1, i2
        else:
            better = score > best_score
            best_score = jnp.where(better, score, best_score)
            e0 = jnp.where(better, i1, e0)
            e1 = jnp.where(better, i2, e1)
    oh0 = (lane == e0).astype(F32)
    oh1 = (lane == e1).astype(F32)
    w0 = jnp.sum(oh0 * s, -1, keepdims=True)
    w1 = jnp.sum(oh1 * s, -1, keepdims=True)
    den = w0 + w1
    return e0, e1, oh0, oh1, w0 / den, w1 / den


def _post_kernel(*refs, kind, alpha, n_experts):
    prologue, n_pro = _PROLOGUES[kind]
    pro = refs[:n_pro]
    (h_ref, w_ref, b_ref, g1_ref, lg_ref, lb_ref, sc_ref, sh_ref, rw_ref, rb_ref, cin_ref,
     h1_ref, tok_ref, meta_ref, cout_ref, cnt_scr) = refs[n_pro:]
    i = pl.program_id(0)
    tm = h_ref.shape[0]

    @pl.when(i == 0)
    def _():
        cnt_scr[...] = cin_ref[...]

    for rows, y in prologue(*pro):
        out = _dot(y, w_ref[...]) + b_ref[...]
        h1_ref[rows, :] = _ln(alpha * h_ref[rows, :] + g1_ref[0] * out, lg_ref[...], lb_ref[...])
    tok = h1_ref[...] * (1.0 + sc_ref[0]) + sh_ref[0]
    tok_ref[...] = tok
    logits = jnp.dot(tok, rw_ref[...], precision=HIGHEST, preferred_element_type=F32)
    e0, e1, oh0, oh1, w0, w1 = _route(logits, rb_ref[...], n_experts)
    r_i = lax.broadcasted_iota(I32, (tm, tm), 0)
    c_i = lax.broadcasted_iota(I32, (tm, tm), 1)
    tri = (c_i < r_i).astype(BF16)
    oh = oh0 + oh1
    before = _dot(tri, oh.astype(BF16)) + cnt_scr[0:1, :]
    rank0 = jnp.sum(oh0 * before, -1, keepdims=True)
    rank1 = jnp.sum(oh1 * before, -1, keepdims=True)
    cnt_scr[0:1, :] = cnt_scr[0:1, :] + jnp.sum(oh, 0, keepdims=True)
    lane = lax.broadcasted_iota(I32, (tm, LANES), 1)
    cols = (e0.astype(F32), e1.astype(F32), rank0, rank1, w0, w1)
    meta = jnp.zeros((tm, LANES), F32)
    for k, col in enumerate(cols):
        meta = jnp.where(lane == k, col, meta)
    meta_ref[...] = meta
    cout_ref[...] = cnt_scr[...]


def _post(kind, pro_args, pro_specs, h, w, b, g1, lg, lb, sc, sh, rw, rb, cnt_in, seq, alpha, n_experts):
    n, d = h.shape
    tm = min(ROW_TILE, seq)
    tps = seq // tm
    row = lambda i: (i, 0)
    per_b = lambda i: (i // tps, 0, 0)
    fixed = lambda i: (0, 0)
    in_specs = list(pro_specs(tm, d)) + [
        pl.BlockSpec((tm, d), row), pl.BlockSpec((d, d), fixed), pl.BlockSpec((1, d), fixed),
        pl.BlockSpec((1, 1, d), per_b), pl.BlockSpec((1, d), fixed), pl.BlockSpec((1, d), fixed),
        pl.BlockSpec((1, 1, d), per_b), pl.BlockSpec((1, 1, d), per_b),
        pl.BlockSpec((d, LANES), fixed), pl.BlockSpec((1, LANES), fixed), pl.BlockSpec((8, LANES), fixed)]
    return pl.pallas_call(
        functools.partial(_post_kernel, kind=kind, alpha=alpha, n_experts=n_experts),
        out_shape=(jax.ShapeDtypeStruct((n, d), F32), jax.ShapeDtypeStruct((n, d), F32),
                   jax.ShapeDtypeStruct((n, LANES), F32), jax.ShapeDtypeStruct((8, LANES), F32)),
        grid=(n // tm,),
        in_specs=in_specs,
        out_specs=(pl.BlockSpec((tm, d), row), pl.BlockSpec((tm, d), row),
                   pl.BlockSpec((tm, LANES), row), pl.BlockSpec((8, LANES), fixed)),
        scratch_shapes=[pltpu.VMEM((8, LANES), F32)],
        compiler_params=_params(1), name="post_" + kind,
    )(*pro_args, h, w, b, g1, lg, lb, sc, sh, rw, rb, cnt_in)


def _dest_kernel(meta_ref, pstart_ref, o_ref):
    tm = meta_ref.shape[0]
    meta = meta_ref[...]
    lane = lax.broadcasted_iota(I32, (tm, LANES), 1)
    pstart = pstart_ref[0:1, :]
    dests = []
    for k in range(TOP_K):
        e = meta[:, k:k + 1].astype(I32)
        off = jnp.sum(jnp.where(lane == e, pstart, 0.0), -1, keepdims=True)
        dests.append(off + meta[:, TOP_K + k:TOP_K + k + 1])
    m = jnp.zeros((tm, LANES), F32)
    for k in range(TOP_K):
        m = jnp.where(lane == k, dests[k], m)
    o_ref[0] = m.T[0:8, :].astype(I32)


def _dest(meta, pstart):
    n = meta.shape[0]
    tm = DMA_TILE
    return pl.pallas_call(
        _dest_kernel,
        out_shape=jax.ShapeDtypeStruct((n // tm, 8, tm), I32),
        grid=(n // tm,),
        in_specs=[pl.BlockSpec((tm, LANES), lambda i: (i, 0)), pl.BlockSpec((8, LANES), lambda i: (0, 0))],
        out_specs=pl.BlockSpec((1, 8, tm), lambda i: (i, 0, 0)),
        compiler_params=_params(1), name="moe_dest",
    )(meta, pstart)


def _dispatch_kernel(dest_ref, tok_ref, xs_in_ref, xs_ref, idx_scr, isem, rsem):
    del xs_in_ref
    i = pl.program_id(0)
    tm = tok_ref.shape[0]
    for k in range(TOP_K):
        pltpu.make_async_copy(dest_ref.at[i, k], idx_scr.at[k], isem).start()
    for k in range(TOP_K):
        pltpu.make_async_copy(dest_ref.at[i, k], idx_scr.at[k], isem).wait()

    def row_copy(r, k):
        return pltpu.make_async_copy(tok_ref.at[pl.ds(r, 1)], xs_ref.at[pl.ds(idx_scr[k, r], 1)], rsem)

    def start(r, c):
        for k in range(TOP_K):
            row_copy(r, k).start()
        return c

    def wait(r, c):
        for k in range(TOP_K):
            row_copy(r, k).wait()
        return c

    lax.fori_loop(0, tm, start, 0)
    lax.fori_loop(0, tm, wait, 0)


def _dispatch(dest, tok, xs):
    n, d = tok.shape
    tm = DMA_TILE
    return pl.pallas_call(
        _dispatch_kernel,
        out_shape=jax.ShapeDtypeStruct(xs.shape, xs.dtype),
        grid=(n // tm,),
        in_specs=[pl.BlockSpec(memory_space=pl.ANY), pl.BlockSpec((tm, d), lambda i: (i, 0)),
                  pl.BlockSpec(memory_space=pl.ANY)],
        out_specs=pl.BlockSpec(memory_space=pl.ANY),
        scratch_shapes=[pltpu.SMEM((TOP_K, tm), I32), pltpu.SemaphoreType.DMA, pltpu.SemaphoreType.DMA],
        input_output_aliases={2: 0},
        compiler_params=_params(1), name="moe_dispatch",
    )(dest, tok, xs)


def _expert_kernel(be_ref, nu_ref, x_ref, w1_ref, w3_ref, w2_ref, y_ref, w1_scr, w3_scr, w2_scr):
    j = pl.program_id(0)
    e = be_ref[j]
    e_prev = be_ref[jnp.maximum(j - 1, 0)]

    @pl.when(j < nu_ref[0])
    def _():
        @pl.when((j == 0) | (e != e_prev))
        def _():
            w1_scr[...] = w1_ref[0, 0].astype(BF16)
            w3_scr[...] = w3_ref[0, 0].astype(BF16)
            w2_scr[...] = w2_ref[0, 0].astype(BF16)

        x = x_ref[...].astype(BF16)
        hid = _silu(_dot(x, w1_scr[...])) * _dot(x, w3_scr[...])
        y_ref[...] = _dot(hid.astype(BF16), w2_scr[...])

    @pl.when(j >= nu_ref[0])
    def _():
        y_ref[...] = jnp.zeros_like(y_ref)


def _experts(blk_e, n_used, xs, w1, w3, w2, layer):
    p, d = xs.shape
    de = w1.shape[-1]
    nblk = p // MOE_ROWS
    wmap = lambda j, be, nu: (layer, be[j], 0, 0)
    return pl.pallas_call(
        _expert_kernel,
        out_shape=jax.ShapeDtypeStruct((p, d), F32),
        grid_spec=pltpu.PrefetchScalarGridSpec(
            num_scalar_prefetch=2, grid=(nblk,),
            in_specs=[pl.BlockSpec((MOE_ROWS, d), lambda j, be, nu: (j, 0)),
                      pl.BlockSpec((1, 1, d, de), wmap), pl.BlockSpec((1, 1, d, de), wmap),
                      pl.BlockSpec((1, 1, de, d), wmap)],
            out_specs=pl.BlockSpec((MOE_ROWS, d), lambda j, be, nu: (j, 0)),
            scratch_shapes=[pltpu.VMEM((d, de), BF16), pltpu.VMEM((d, de), BF16), pltpu.VMEM((de, d), BF16)]),
        compiler_params=_params(1), name="moe_experts",
    )(blk_e, n_used, xs, w1, w3, w2)


def _combine_kernel(dest_ref, ys_ref, meta_ref, h_ref, g2_ref, lg_ref, lb_ref, o_ref,
                    idx_scr, buf_scr, isem, rsem, *, alpha):
    i = pl.program_id(0)
    tm = h_ref.shape[0]
    for k in range(TOP_K):
        pltpu.make_async_copy(dest_ref.at[i, k], idx_scr.at[k], isem).start()
    for k in range(TOP_K):
        pltpu.make_async_copy(dest_ref.at[i, k], idx_scr.at[k], isem).wait()

    def row_copy(r, k):
        return pltpu.make_async_copy(ys_ref.at[pl.ds(idx_scr[k, r], 1)], buf_scr.at[k, pl.ds(r, 1)], rsem)

    def start(r, c):
        for k in range(TOP_K):
            row_copy(r, k).start()
        return c

    def wait(r, c):
        for k in range(TOP_K):
            row_copy(r, k).wait()
        return c

    lax.fori_loop(0, tm, start, 0)
    lax.fori_loop(0, tm, wait, 0)
    meta = meta_ref[...]
    f = meta[:, 2 * TOP_K:2 * TOP_K + 1] * buf_scr[0]
    for k in range(1, TOP_K):
        f = f + meta[:, 2 * TOP_K + k:2 * TOP_K + k + 1] * buf_scr[k]
    o_ref[...] = _ln(alpha * h_ref[...] + g2_ref[0] * f, lg_ref[...], lb_ref[...])


def _combine(dest, ys, meta, h, g2, lg, lb, seq, alpha):
    n, d = h.shape
    tm = DMA_TILE
    tps = seq // tm
    row = lambda i: (i, 0)
    fixed = lambda i: (0, 0)
    return pl.pallas_call(
        functools.partial(_combine_kernel, alpha=alpha),
        out_shape=jax.ShapeDtypeStruct((n, d), F32),
        grid=(n // tm,),
        in_specs=[pl.BlockSpec(memory_space=pl.ANY), pl.BlockSpec(memory_space=pl.ANY),
                  pl.BlockSpec((tm, LANES), row), pl.BlockSpec((tm, d), row),
                  pl.BlockSpec((1, 1, d), lambda i: (i // tps, 0, 0)),
                  pl.BlockSpec((1, d), fixed), pl.BlockSpec((1, d), fixed)],
        out_specs=pl.BlockSpec((tm, d), row),
        scratch_shapes=[pltpu.SMEM((TOP_K, tm), I32), pltpu.VMEM((TOP_K, tm, d), F32),
                        pltpu.SemaphoreType.DMA, pltpu.SemaphoreType.DMA],
        compiler_params=_params(1), name="moe_combine",
    )(dest, ys, meta, h, g2, lg, lb)


def _moe(streams, counts, w1, w3, w2, layer, alpha):
    n_experts = w1.shape[1]
    d = streams[0]["tok"].shape[1]
    n_assign = TOP_K * sum(s["tok"].shape[0] for s in streams)
    p = n_assign + n_experts * MOE_ROWS
    nblk = p // MOE_ROWS
    cnt = counts[0, :n_experts].astype(I32)
    padded = (cnt + MOE_ROWS - 1) // MOE_ROWS * MOE_ROWS
    pend = jnp.cumsum(padded)
    pstart = pend - padded
    n_used = (pend[-1] // MOE_ROWS).astype(I32).reshape(1)
    blk = jnp.arange(nblk, dtype=I32) * MOE_ROWS
    blk_e = jnp.sum((pend[None, :] <= blk[:, None]).astype(I32), axis=1)
    last_e = jnp.sum((pend <= jnp.maximum(pend[-1] - 1, 0)).astype(I32))
    blk_e = jnp.minimum(blk_e, last_e).astype(I32)
    pstart_f = jnp.zeros((8, LANES), F32).at[0, :n_experts].set(pstart.astype(F32))
    dests = [_dest(s["meta"], pstart_f) for s in streams]
    xs = jnp.zeros((p, d), F32)
    for s, dst in zip(streams, dests):
        xs = _dispatch(dst, s["tok"], xs)
    ys = _experts(blk_e, n_used, xs, w1, w3, w2, layer)
    return [_combine(dst, ys, s["meta"], s["h1"], s["g2"], s["lg"], s["lb"], s["seq"], alpha)
            for s, dst in zip(streams, dests)]


def _log_sigmoid(x):
    return jnp.minimum(x, 0.0) - jnp.log(1.0 + jnp.exp(-jnp.abs(x)))


def _gla_in_kernel(h_ref, sc_ref, sh_ref, w_ref, wl_ref, wg_ref, bg_ref, qk_ref, v_ref, r_ref, g_ref, *, qscale):
    tm, d = h_ref.shape
    a = (h_ref[...] * (1.0 + sc_ref[0]) + sh_ref[0]).astype(BF16)
    qk = _dot(a, w_ref[:, :d])
    lane = lax.broadcasted_iota(I32, (tm, d), 1)
    qk_ref[...] = jnp.where(lane < d // 2, qk * qscale, qk)
    v_ref[...] = _dot(a, w_ref[:, d:2 * d]).astype(BF16)
    r_ref[...] = _dot(a, w_ref[:, 2 * d:])
    lr = _dot(a, wl_ref[...])
    gpre = _dot(lr.astype(BF16), wg_ref[...]) + bg_ref[...]
    g_ref[...] = _log_sigmoid(gpre) * (1.0 / GLA_TAU)


def _gla_weights(w_in, w_gate, b_gate, d):
    dk = d // 2
    n_lr = HY_DIRS * GLA_RANK
    w_main = w_in[:, :2 * dk + 2 * d].astype(BF16)
    w_lr = jnp.zeros((d, LANES), F32).at[:, :n_lr].set(w_in[:, 2 * dk + 2 * d:]).astype(BF16)
    w_g = jnp.zeros((LANES, HY_DIRS * dk), F32)
    for dr in range(HY_DIRS):
        w_g = w_g.at[dr * GLA_RANK:(dr + 1) * GLA_RANK, dr * dk:(dr + 1) * dk].set(w_gate[dr])
    return w_main, w_lr, w_g.astype(BF16), b_gate.reshape(1, HY_DIRS * dk)


def _gla_in(h, sc, sh, w_main, w_lr, w_g, b_g, seq):
    n, d = h.shape
    tm = min(ROW_TILE, seq)
    tps = seq // tm
    row = lambda i: (i, 0)
    per_b = lambda i: (i // tps, 0, 0)
    fixed = lambda i: (0, 0)
    qscale = float(((d // 2) // GLA_HEADS) ** -0.5)
    return pl.pallas_call(
        functools.partial(_gla_in_kernel, qscale=qscale),
        out_shape=(jax.ShapeDtypeStruct((n, d), F32), jax.ShapeDtypeStruct((n, d), BF16),
                   jax.ShapeDtypeStruct((n, d), F32), jax.ShapeDtypeStruct((n, d), F32)),
        grid=(n // tm,),
        in_specs=[pl.BlockSpec((tm, d), row), pl.BlockSpec((1, 1, d), per_b), pl.BlockSpec((1, 1, d), per_b),
                  pl.BlockSpec((d, 3 * d), fixed), pl.BlockSpec((d, LANES), fixed),
                  pl.BlockSpec((LANES, d), fixed), pl.BlockSpec((1, d), fixed)],
        out_specs=tuple(pl.BlockSpec((tm, d), row) for _ in range(4)),
        compiler_params=_params(1), name="gla_in",
    )(h, sc, sh, w_main, w_lr, w_g, b_g)


def _gla_chunk(q, k, v, g, tri, mask, st_scr, o_ref, rows, edge):
    b = jnp.dot(tri, g, precision=HIGHEST, preferred_element_type=F32)
    b_edge = b[edge:edge + 1, :]
    qe = (q * jnp.exp(b)).astype(BF16)
    ke = (k * jnp.exp(-b)).astype(BF16)
    att = lax.dot_general(qe, ke, (((1,), (1,)), ((), ())), preferred_element_type=F32)
    att = jnp.where(mask, att, 0.0).astype(BF16)
    st = st_scr[...]
    carry = lax.dot_general(qe, st.astype(BF16), (((1,), (1,)), ((), ())), preferred_element_type=F32)
    o_ref[rows, :] = _dot(att, v) + carry
    kd = (k * jnp.exp(b_edge - b)).astype(BF16)
    upd = lax.dot_general(v, kd, (((0,), (0,)), ((), ())), preferred_element_type=F32)
    st_scr[...] = st * jnp.exp(b_edge) + upd


def _gla_scan_kernel(qf, kf, vf, gf, qb, kb, vb, gb, s0f, s0b, of, ob, sfo, sbo, sf_scr, sb_scr):
    c = pl.program_id(2)

    @pl.when(c == 0)
    def _():
        sf_scr[...] = s0f[0, 0]
        sb_scr[...] = s0b[0, 0]

    cs = GLA_CHUNK
    nchunk = qf.shape[0] // cs
    r_i = lax.broadcasted_iota(I32, (cs, cs), 0)
    c_i = lax.broadcasted_iota(I32, (cs, cs), 1)
    low, up = c_i <= r_i, c_i >= r_i
    low_f, up_f = low.astype(F32), up.astype(F32)
    for j in range(nchunk):
        rows = slice(j * cs, (j + 1) * cs)
        _gla_chunk(qf[rows, :], kf[rows, :], vf[rows, :], gf[rows, :], low_f, low, sf_scr, of, rows, cs - 1)
        rows = slice((nchunk - 1 - j) * cs, (nchunk - j) * cs)
        _gla_chunk(qb[rows, :], kb[rows, :], vb[rows, :], gb[rows, :], up_f, up, sb_scr, ob, rows, 0)

    sfo[0, 0] = sf_scr[...]
    sbo[0, 0] = sb_scr[...]


def _gla_scan(qk, v, r, g, s0f, s0b, nb, seq):
    del r
    n, d = qk.shape
    nh = GLA_HEADS
    dkh, dvh = (d // 2) // nh, d // nh
    rt = min(ROW_TILE, seq)
    npb = seq // rt
    fw = lambda col: (lambda b, hd, c: (b * npb + c, col(hd)))
    bw = lambda col: (lambda b, hd, c: (b * npb + npb - 1 - c, col(hd)))
    qcol, kcol = (lambda hd: hd), (lambda hd: nh + hd)
    st = lambda b, hd, c: (b, hd, 0, 0)
    state = jax.ShapeDtypeStruct((nb, nh, dvh, dkh), F32)
    return pl.pallas_call(
        _gla_scan_kernel,
        out_shape=(jax.ShapeDtypeStruct((n, d), F32), jax.ShapeDtypeStruct((n, d), F32), state, state),
        grid=(nb, nh, npb),
        in_specs=[pl.BlockSpec((rt, dkh), fw(qcol)), pl.BlockSpec((rt, dkh), fw(kcol)),
                  pl.BlockSpec((rt, dvh), fw(qcol)), pl.BlockSpec((rt, dkh), fw(qcol)),
                  pl.BlockSpec((rt, dkh), bw(qcol)), pl.BlockSpec((rt, dkh), bw(kcol)),
                  pl.BlockSpec((rt, dvh), bw(qcol)), pl.BlockSpec((rt, dkh), bw(kcol)),
                  pl.BlockSpec((1, 1, dvh, dkh), st), pl.BlockSpec((1, 1, dvh, dkh), st)],
        out_specs=(pl.BlockSpec((rt, dvh), fw(qcol)), pl.BlockSpec((rt, dvh), bw(qcol)),
                   pl.BlockSpec((1, 1, dvh, dkh), st), pl.BlockSpec((1, 1, dvh, dkh), st)),
        scratch_shapes=[pltpu.VMEM((dvh, dkh), F32), pltpu.VMEM((dvh, dkh), F32)],
        compiler_params=_params(3), name="gla_scan",
    )(qk, qk, v, g, qk, qk, v, g, s0f, s0b)


def _hy_in_kernel(h_ref, hp_ref, hn_ref, sc_ref, sh_ref, w_ref, b_ref, cw_ref, cb_ref, o_ref, a_scr, z_scr,
                  *, tps, halo):
    i = pl.program_id(0)
    tm, d = h_ref.shape
    sc = 1.0 + sc_ref[0]
    sh = sh_ref[0]
    a_scr[0:halo, :] = (hp_ref[...] * sc + sh).astype(BF16)
    a_scr[halo:halo + tm, :] = (h_ref[...] * sc + sh).astype(BF16)
    a_scr[halo + tm:, :] = (hn_ref[...] * sc + sh).astype(BF16)
    row = lax.broadcasted_iota(I32, (tm, 1), 0)
    first = jnp.where(i % tps == 0, 0, -1)
    last = jnp.where(i % tps == tps - 1, tm - 1, -1)
    for c in range(o_ref.shape[0]):
        cols = slice(c * d, (c + 1) * d)
        z_scr[...] = _dot(a_scr[...], w_ref[:, cols]) + b_ref[:, cols]
        zp = jnp.where(row == first, 0.0, z_scr[pl.ds(halo - 1, tm), :])
        zc = z_scr[pl.ds(halo, tm), :]
        zn = jnp.where(row == last, 0.0, z_scr[pl.ds(halo + 1, tm), :])
        o_ref[c] = cw_ref[0:1, cols] * zp + cw_ref[1:2, cols] * zc + cw_ref[2:3, cols] * zn + cb_ref[:, cols]


def _hy_in(h, sc, sh, w, b, cw, cb, seq):
    n, d = h.shape
    nsplit = w.shape[1] // d
    halo = 16
    tm = min(ROW_TILE, seq)
    tps = seq // tm
    hb = tm // halo
    row = lambda i: (i, 0)
    per_b = lambda i: (i // tps, 0, 0)
    fixed = lambda i: (0, 0)
    return pl.pallas_call(
        functools.partial(_hy_in_kernel, tps=tps, halo=halo),
        out_shape=jax.ShapeDtypeStruct((nsplit, n, d), F32),
        grid=(n // tm,),
        in_specs=[pl.BlockSpec((tm, d), row),
                  pl.BlockSpec((halo, d), lambda i: (jnp.maximum(i * hb - 1, 0), 0)),
                  pl.BlockSpec((halo, d), lambda i: (jnp.minimum((i + 1) * hb, n // halo - 1), 0)),
                  pl.BlockSpec((1, 1, d), per_b), pl.BlockSpec((1, 1, d), per_b),
                  pl.BlockSpec((d, nsplit * d), fixed), pl.BlockSpec((1, nsplit * d), fixed),
                  pl.BlockSpec((cw.shape[0], nsplit * d), fixed), pl.BlockSpec((1, nsplit * d), fixed)],
        out_specs=pl.BlockSpec((nsplit, tm, d), lambda i: (0, i, 0)),
        scratch_shapes=[pltpu.VMEM((tm + 2 * halo, d), BF16), pltpu.VMEM((tm + 2 * halo, d), F32)],
        compiler_params=_params(1), name="hyena_in",
    )(h, h, h, sc, sh, w, b, cw, cb)


def _hy_filter_rows(t, seq, w1_ref, b1_ref, w2_ref, b2_ref, w3_ref, b3_ref, n_bands):
    rows = t.shape[0]
    tf = t.astype(F32)
    t_lin = tf * (1.0 / (seq - 1))
    wpos = tf * (2.0 * math.pi / seq)
    lane = lax.broadcasted_iota(I32, (rows, LANES), 1)
    jb = jnp.where(lane > n_bands, lane - n_bands - 1, lane - 1).astype(F32)
    band = 1e-4 + jb * ((n_bands - 1 - 1e-4) / (n_bands - 1))
    ang = band * wpos
    z = jnp.where(lane == 0, t_lin,
                  jnp.where(lane <= n_bands, jnp.cos(ang), jnp.where(lane <= 2 * n_bands, -jnp.sin(ang), 0.0)))
    hf = jnp.sin(_dot(z.astype(BF16), w1_ref[...]) + b1_ref[...])
    hf = jnp.sin(_dot(hf.astype(BF16), w2_ref[...]) + b2_ref[...])
    hf = jnp.sin(_dot(hf.astype(BF16), w3_ref[...]) + b3_ref[...])
    return hf.astype(BF16), t_lin


def _hy_taps_kernel(w1_ref, b1_ref, w2_ref, b2_ref, w3_ref, b3_ref, w4_ref, o_ref, *, seq, n_bands):
    i = pl.program_id(0)
    n_ord, tr, d = o_ref.shape
    mlp = (w1_ref, b1_ref, w2_ref, b2_ref, w3_ref, b3_ref)
    r0 = i * tr
    second = (r0 >= seq).astype(I32)
    n_idx = r0 + lax.broadcasted_iota(I32, (tr, 1), 0)
    t = jnp.where(second == 1, 2 * seq - n_idx, n_idx)
    hf, t_lin = _hy_filter_rows(t, seq, *mlp, n_bands)
    c_idx = lax.broadcasted_iota(I32, (1, d), 1).astype(F32)
    min_decay = math.log(HY_DECAY_TARGET) / HY_DECAY_LONG_PCT
    max_decay = math.log(HY_DECAY_TARGET) / HY_DECAY_SHORT_PCT
    delta = jnp.abs(min_decay + c_idx * ((max_decay - min_decay) / (d - 1)))
    window = jnp.where(n_idx == seq, 0.0, jnp.exp(-t_lin * delta))
    for o in range(n_ord):
        o_ref[o] = _dot(hf, w4_ref[second, o]) * window

    @pl.when(i == 0)
    def _():
        hf0, _ = _hy_filter_rows(jnp.zeros((8, 1), I32), seq, *mlp, n_bands)
        first = lax.broadcasted_iota(I32, (8, 1), 0) == 0
        for o in range(n_ord):
            o_ref[o, 0:8, :] = o_ref[o, 0:8, :] + jnp.where(first, _dot(hf0, w4_ref[1, o]), 0.0)


def _hy_taps(seq, d, w1, b1, w2, b2, w3, b3, w4):
    emb, ff = w1.shape
    n_bands = (emb - 1) // 2
    w1p = jnp.zeros((LANES, ff), F32).at[:emb].set(w1).astype(BF16)
    w4r = w4.reshape(ff, HY_ORDER, HY_DIRS, d).transpose(2, 1, 0, 3).astype(BF16)
    tr = min(ROW_TILE, seq)
    fixed = lambda i: (0, 0)
    return pl.pallas_call(
        functools.partial(_hy_taps_kernel, seq=seq, n_bands=n_bands),
        out_shape=jax.ShapeDtypeStruct((HY_ORDER, 2 * seq, d), F32),
        grid=(2 * seq // tr,),
        in_specs=[pl.BlockSpec((LANES, ff), fixed), pl.BlockSpec((1, ff), fixed),
                  pl.BlockSpec((ff, ff), fixed), pl.BlockSpec((1, ff), fixed),
                  pl.BlockSpec((ff, ff), fixed), pl.BlockSpec((1, ff), fixed),
                  pl.BlockSpec((HY_DIRS, HY_ORDER, ff, d), lambda i: (0, 0, 0, 0))],
        out_specs=pl.BlockSpec((HY_ORDER, tr, d), lambda i: (0, i, 0)),
        compiler_params=_params(1), name="hyena_taps",
    )(w1p, b1.reshape(1, ff), w2.astype(BF16), b2.reshape(1, ff), w3.astype(BF16), b3.reshape(1, ff), w4r)


def _cos_sin(num, den):
    ang = (num % den).astype(F32) * (2.0 * math.pi / den)
    return jnp.cos(ang), jnp.sin(ang)


def _fft_tables(seq, n2):
    n = 2 * seq
    n1 = n // n2
    half = seq // n2
    ar = jnp.arange(n1, dtype=I32)
    ca, sa = _cos_sin(ar[:, None] * ar[None, :], n1)
    fa_half = jnp.concatenate([ca[:, :half], sa[:, :half]], 0).astype(BF16)
    fa_full = jnp.concatenate([ca, sa], 0).astype(BF16)
    ia = (jnp.concatenate([ca[:half], sa[:half]], 0) * (1.0 / n)).astype(BF16)
    k = ar[:, None, None] + n1 * jnp.arange(n2, dtype=I32)[None, :, None]
    cb, sb = _cos_sin(k * jnp.arange(n2, dtype=I32)[None, None, :], n)
    fb = jnp.concatenate([cb, sb], 1).astype(BF16)
    gb = jnp.concatenate([cb.transpose(0, 2, 1), sb.transpose(0, 2, 1)], 1).astype(BF16)
    return fa_half, fa_full, ia, fb, gb


def _fft_a_kernel(x_ref, f_ref, o_ref, *, cplx):
    n1 = f_ref.shape[0] // 2
    w = x_ref.shape[-1]
    if cplx:
        z = jnp.concatenate([x_ref[0, 0], x_ref[0, 1]], axis=1).astype(BF16)
        p = _dot(f_ref[...], z)
        o_ref[0, 0] = (p[:n1, :w] + p[n1:, w:]).astype(BF16)
        o_ref[0, 1] = (p[:n1, w:] - p[n1:, :w]).astype(BF16)
    else:
        p = _dot(f_ref[...], x_ref[0, 0].astype(BF16))
        o_ref[0, 0] = p[:n1].astype(BF16)
        o_ref[0, 1] = (-p[n1:]).astype(BF16)


def _fft_a(x4, which, f, cplx):
    _, nb, rows, cols = x4.shape
    n1 = f.shape[0] // 2
    per = 2 if cplx else 1
    groups = nb // per
    w = min(FFT_COLS, cols)
    return pl.pallas_call(
        functools.partial(_fft_a_kernel, cplx=cplx),
        out_shape=jax.ShapeDtypeStruct((groups, 2, n1, cols), BF16),
        grid=(groups, cols // w),
        in_specs=[pl.BlockSpec((1, per, rows, w), lambda p, j: (which, p, 0, j)),
                  pl.BlockSpec((2 * n1, rows), lambda p, j: (0, 0))],
        out_specs=pl.BlockSpec((1, 2, n1, w), lambda p, j: (p, 0, 0, j)),
        compiler_params=_params(2), name="fft_stage_a",
    )(x4, f)


def _fft_b_kernel(*refs, conv):
    if conv:
        a_ref, fb_ref, k_ref, gb_ref, o_ref = refs
    else:
        a_ref, fb_ref, o_ref = refs
    n2, d = a_ref.shape[-2:]
    z = jnp.concatenate([a_ref[0, 0, 0], a_ref[0, 1, 0]], axis=1)
    p = _dot(fb_ref[0], z)
    xr = p[:n2, :d] + p[n2:, d:]
    xi = p[:n2, d:] - p[n2:, :d]
    if not conv:
        o_ref[0, 0, 0] = xr
        o_ref[0, 1, 0] = xi
        return
    kr, ki = k_ref[0, 0, 0], k_ref[0, 1, 0]
    y = jnp.concatenate([xr * kr - xi * ki, xr * ki + xi * kr], axis=1).astype(BF16)
    p2 = _dot(gb_ref[0], y)
    o_ref[0, 0, 0] = (p2[:n2, :d] - p2[n2:, d:]).astype(BF16)
    o_ref[0, 1, 0] = (p2[:n2, d:] + p2[n2:, :d]).astype(BF16)


def _fft_b(a5, fb, kspec=None, which=0, gb=None):
    groups, _, n1, n2, d = a5.shape
    conv = kspec is not None
    slab = pl.BlockSpec((1, 2, 1, n2, d), lambda k, p: (p, 0, k, 0, 0))
    tab = pl.BlockSpec((1, 2 * n2, n2), lambda k, p: (k, 0, 0))
    in_specs = [slab, tab]
    args = [a5, fb]
    if conv:
        in_specs += [pl.BlockSpec((1, 2, 1, n2, d), lambda k, p: (which, 0, k, 0, 0)), tab]
        args += [kspec, gb]
    return pl.pallas_call(
        functools.partial(_fft_b_kernel, conv=conv),
        out_shape=jax.ShapeDtypeStruct((groups, 2, n1, n2, d), BF16 if conv else F32),
        grid=(n1, groups),
        in_specs=in_specs, out_specs=slab,
        compiler_params=_params(2), name="fft_stage_b_conv" if conv else "fft_stage_b",
    )(*args)


def _fft_c_kernel(y_ref, ia_ref, u_ref, g_ref, skip_ref, o_ref):
    half = ia_ref.shape[0] // 2
    w = y_ref.shape[-1]
    z = jnp.concatenate([y_ref[0, 0], y_ref[0, 1]], axis=1)
    p = _dot(ia_ref[...], z)
    cr = p[:half, :w] - p[half:, w:]
    ci = p[:half, w:] + p[half:, :w]
    skip = skip_ref[...]
    o_ref[0] = (g_ref[0, 0] * (cr + u_ref[0, 0] * skip)).astype(o_ref.dtype)
    o_ref[1] = (g_ref[0, 1] * (ci + u_ref[0, 1] * skip)).astype(o_ref.dtype)


def _fft_c(y4, ia, u4, u_which, g4, g_which, skip, out_dtype):
    groups, _, n1, cols = y4.shape
    half = ia.shape[0] // 2
    nb = 2 * groups
    w = min(FFT_COLS, cols)
    d = skip.shape[-1]
    skip_w = jnp.tile(skip.reshape(1, d), (1, w // d))
    return pl.pallas_call(
        _fft_c_kernel,
        out_shape=jax.ShapeDtypeStruct((nb, half, cols), out_dtype),
        grid=(groups, cols // w),
        in_specs=[pl.BlockSpec((1, 2, n1, w), lambda p, j: (p, 0, 0, j)),
                  pl.BlockSpec((2 * half, n1), lambda p, j: (0, 0)),
                  pl.BlockSpec((1, 2, half, w), lambda p, j: (u_which, p, 0, j)),
                  pl.BlockSpec((1, 2, half, w), lambda p, j: (g_which, p, 0, j)),
                  pl.BlockSpec((1, w), lambda p, j: (0, 0))],
        out_specs=pl.BlockSpec((2, half, w), lambda p, j: (p, 0, j)),
        compiler_params=_params(2), name="fft_stage_c",
    )(y4, ia, u4, g4, skip_w)


def _hy_long_conv(zs, taps, skip, nb, seq):
    nsplit, n, d = zs.shape
    n2 = FFT_N2
    half = seq // n2
    n1 = 2 * half
    fa_half, fa_full, ia, fb, gb = _fft_tables(seq, n2)
    taps4 = taps.reshape(HY_ORDER, 1, n1, n2 * d)
    kspec = jnp.stack([
        _fft_b(_fft_a(taps4, o, fa_full, False).reshape(1, 2, n1, n2, d), fb)[0] for o in range(HY_ORDER)])
    zs4 = zs.reshape(nsplit, nb, half, n2 * d)
    u4, u_which = zs4, 0
    for o in range(HY_ORDER):
        a = _fft_a(u4, u_which, fa_half, True)
        y = _fft_b(a.reshape(nb // 2, 2, n1, n2, d), fb, kspec, o, gb)
        out = _fft_c(y.reshape(nb // 2, 2, n1, n2 * d), ia, u4, u_which, zs4, o + 1, skip[o],
                     BF16 if o == HY_ORDER - 1 else F32)
        u4, u_which = out[None], 0
    return out.reshape(n, d)


def _short_conv_kernel(u_ref, g_ref, t_ref, skip_ref, ff_ref, tf_ref, fi_ref, o_ref):
    seq, ct = u_ref.shape[-2:]
    n = 2 * seq
    z = jnp.concatenate([u_ref[0, 0], u_ref[0, 1]], axis=1).astype(BF16)
    p = _dot(ff_ref[...], z)
    xr = p[:n, :ct] + p[n:, ct:]
    xi = p[:n, ct:] - p[n:, :ct]
    pt = _dot(tf_ref[...], t_ref[0].astype(BF16))
    kr, ki = pt[:n], -pt[n:]
    y = jnp.concatenate([xr * kr - xi * ki, xr * ki + xi * kr], axis=1).astype(BF16)
    p2 = _dot(fi_ref[...], y)
    cr = p2[:seq, :ct] - p2[seq:, ct:]
    ci = p2[:seq, ct:] + p2[seq:, :ct]
    skip = skip_ref[...]
    o_ref[0] = (g_ref[0, 0] * (cr + u_ref[0, 0] * skip)).astype(o_ref.dtype)
    o_ref[1] = (g_ref[0, 1] * (ci + u_ref[0, 1] * skip)).astype(o_ref.dtype)


def _hy_short_conv(zs, taps, skip, nb, seq):
    nsplit, n, d = zs.shape
    nn = 2 * seq
    ar = jnp.arange(nn, dtype=I32)
    cm, sm = _cos_sin(ar[:, None] * ar[None, :], nn)
    ff = jnp.concatenate([cm[:, :seq], sm[:, :seq]], 0).astype(BF16)
    tf = jnp.concatenate([cm, sm], 0).astype(BF16)
    fi = (jnp.concatenate([cm[:seq], sm[:seq]], 0) * (1.0 / nn)).astype(BF16)
    ct = d // 2
    zs4 = zs.reshape(nsplit, nb, seq, d)
    u4, u_which = zs4, 0
    for o in range(HY_ORDER):
        out = pl.pallas_call(
            _short_conv_kernel,
            out_shape=jax.ShapeDtypeStruct((nb, seq, d), BF16 if o == HY_ORDER - 1 else F32),
            grid=(nb // 2, d // ct),
            in_specs=[pl.BlockSpec((1, 2, seq, ct), lambda p, j, w=u_which: (w, p, 0, j)),
                      pl.BlockSpec((1, 2, seq, ct), lambda p, j, w=o + 1: (w, p, 0, j)),
                      pl.BlockSpec((1, nn, ct), lambda p, j, w=o: (w, 0, j)),
                      pl.BlockSpec((1, ct), lambda p, j: (0, j)),
                      pl.BlockSpec((2 * nn, seq), lambda p, j: (0, 0)),
                      pl.BlockSpec((2 * nn, nn), lambda p, j: (0, 0)),
                      pl.BlockSpec((2 * seq, nn), lambda p, j: (0, 0))],
            out_specs=pl.BlockSpec((2, seq, ct), lambda p, j: (p, 0, j)),
            compiler_params=_params(2), name="hyena_short_conv",
        )(u4, zs4, taps, skip[o].reshape(1, d), ff, tf, fi)
        u4, u_which = out[None], 0
    return out.reshape(n, d)


def kernel(x, c, ctx, c_ctx, w_mod, b_mod, ln_g, ln_b, hy_w_in, hy_b_in, hy_conv_w, hy_conv_b, hy_f_w1, hy_f_b1, hy_f_w2, hy_f_b2, hy_f_w3, hy_f_b3, hy_f_w4, hy_skip, hy_w_out, hy_b_out, gla_w_in, gla_w_gate, gla_b_gate, gla_norm_g, gla_w_out, gm_w_in, gm_b_in, gm_ln_g, gm_ln_b, gm_ws, gm_bs, gm_w_out, gm_b_out, router_w, router_b, moe_w1, moe_w3, moe_w2):
    B, L, D = x.shape
    Lc = ctx.shape[1]
    depth = w_mod.shape[0]
    E = router_w.shape[1]
    alpha = (2.0 * depth) ** 0.25
    gla_layers = list(range(1, depth, N_MIXERS))
    last_ctx = gla_layers[-1] if gla_layers else -1

    cvec = jnp.zeros((8, D), F32).at[:B].set(c).at[B].set(c_ctx)
    mod = _modulation(cvec, w_mod, b_mod)
    h = _add_pos(x.reshape(B * L, D), L)
    hc = ctx.reshape(B * Lc, D)
    rwp = jnp.zeros((D, LANES), F32).at[:, :E].set(router_w)
    rbp = jnp.zeros((1, LANES), F32).at[0, :E].set(router_b)
    zero_cnt = jnp.zeros((8, LANES), F32)
    row2 = lambda v: v.reshape(1, -1)

    for i in range(depth):
        kind, j = i % N_MIXERS, i // N_MIXERS
        ctx_full = i < last_ctx
        ctx_any = i <= last_ctx
        lat = [mod[i, :B, k * D:(k + 1) * D].reshape(B, 1, D) for k in range(6)]
        cm = [jnp.broadcast_to(mod[i, B, k * D:(k + 1) * D].reshape(1, 1, D), (B, 1, D)) for k in range(6)]
        streams_in = [(h, lat, L)] + ([(hc, cm, Lc)] if ctx_any else [])
        pre = []
        if kind == 0:
            w_in = hy_w_in[j].astype(BF16)
            for s_h, s_m, s_len in streams_in[:1 + int(ctx_full)]:
                zs = _hy_in(s_h, s_m[1], s_m[0], w_in, row2(hy_b_in[j]), hy_conv_w[j], row2(hy_conv_b[j]), s_len)
                taps = _hy_taps(s_len, D, hy_f_w1[j], hy_f_b1[j], hy_f_w2[j], hy_f_b2[j], hy_f_w3[j], hy_f_b3[j],
                                hy_f_w4[j])
                if s_len == L:
                    y2 = _hy_long_conv(zs, taps, hy_skip[j], B, s_len)
                else:
                    y2 = _hy_short_conv(zs, taps, hy_skip[j], B, s_len)
                pre.append(("hyena", (y2,), lambda tm, d: [pl.BlockSpec((tm, d), lambda t: (t, 0))]))
            w_out, b_out = hy_w_out[j].astype(BF16), row2(hy_b_out[j])
        elif kind == 1:
            w_main, w_lr, w_g, b_g = _gla_weights(gla_w_in[j], gla_w_gate[j], gla_b_gate[j], D)
            proj = [_gla_in(s_h, s_m[1], s_m[0], w_main, w_lr, w_g, b_g, s_len) for s_h, s_m, s_len in streams_in]
            dvh = D // GLA_HEADS
            dkh = (D // 2) // GLA_HEADS
            zero_state = jnp.zeros((B, GLA_HEADS, dvh, dkh), F32)
            if ctx_any:
                ocf, ocb, s_f, s_b = _gla_scan(*proj[1], zero_state, zero_state, B, Lc)
            else:
                s_f = s_b = zero_state
            o_f, o_b, _, _ = _gla_scan(*proj[0], s_f, s_b, B, L)
            gla_specs = lambda tm, d: [pl.BlockSpec((tm, d), lambda t: (t, 0))] * 3 + [pl.BlockSpec((1, d), lambda t: (0, 0))]
            pre.append(("gla", (o_f, o_b, proj[0][2], row2(gla_norm_g[j])), gla_specs))
            if ctx_full:
                pre.append(("gla", (ocf, ocb, proj[1][2], row2(gla_norm_g[j])), gla_specs))
            w_out, b_out = gla_w_out[j].astype(BF16), jnp.zeros((1, D), F32)
        else:
            w_in = gm_w_in[j].astype(BF16)
            ws = gm_ws[j].astype(BF16)
            bs_exp = jnp.repeat(gm_bs[j].T, D // GM_HEADS, axis=1)
            gm_specs = lambda tm, d: [pl.BlockSpec((tm, d), lambda t: (t, 0))] * 2 + [
                pl.BlockSpec((GM_HEADS, GM_CHUNK, GM_CHUNK), lambda t: (0, 0, 0)), pl.BlockSpec((GM_CHUNK, d), lambda t: (0, 0))]
            for s_h, s_m, s_len in streams_in[:1 + int(ctx_full)]:
                u, vn = _gm_in(s_h, s_m[1], s_m[0], w_in, row2(gm_b_in[j]), row2(gm_ln_g[j]), row2(gm_ln_b[j]), s_len)
                pre.append(("gmlp", (u, vn, ws, bs_exp), gm_specs))
            w_out, b_out = gm_w_out[j].astype(BF16), row2(gm_b_out[j])

        moe_streams = []
        cnt = zero_cnt
        for (s_h, s_m, s_len), (pk, pargs, pspecs) in zip(streams_in, pre):
            h1, tok, meta, cnt = _post(pk, pargs, pspecs, s_h, w_out, b_out, s_m[2], row2(ln_g[i, 0]), row2(ln_b[i, 0]),
                                       s_m[4], s_m[3], rwp, rbp, cnt, s_len, alpha, E)
            moe_streams.append(dict(h1=h1, tok=tok, meta=meta, g2=s_m[5], lg=row2(ln_g[i, 1]), lb=row2(ln_b[i, 1]),
                                    seq=s_len))
        outs = _moe(moe_streams, cnt, moe_w1, moe_w3, moe_w2, i, alpha)
        h = outs[0]
        if ctx_full:
            hc = outs[1]
    return h.reshape(B, L, D)
```

```python
import functools
import math

import jax
import jax.numpy as jnp
from jax import lax
from jax.experimental import pallas as pl
from jax.experimental.pallas import tpu as pltpu

F32 = jnp.float32
BF16 = jnp.bfloat16
I32 = jnp.int32
U32 = jnp.uint32
HIGHEST = lax.Precision.HIGHEST

GRID_W = 64
N_MIXERS = 3
LN_EPS = 1e-5
HY_ORDER = 2
HY_DIRS = 2
HY_DECAY_TARGET = 1e-2
HY_DECAY_SHORT_PCT = 0.3
HY_DECAY_LONG_PCT = 1.5
GLA_HEADS = 4
GLA_RANK = 16
GLA_TAU = 16.0
GLA_CHUNK = 64
GM_CHUNK = 128
GM_HEADS = 4
N_GROUPS = 4
TOP_K = 2

LANES = 128
SUBLANES = 8
V7X_VMEM_LIMIT_BYTES = 56 * 1024 * 1024
MOE_ROWS = 256
ROW_TILE = 512
DMA_TILE = 256
DMA_UNROLL = 8
FFT_N2 = 128
FFT_GROUP = SUBLANES


def _params(n_grid):
    return pltpu.CompilerParams(dimension_semantics=("arbitrary",) * n_grid,
                                vmem_limit_bytes=V7X_VMEM_LIMIT_BYTES)


def _dot(a, b):
    return jnp.dot(a, b, preferred_element_type=F32)


def _dot_nt(a, b):
    return lax.dot_general(a, b, (((1,), (1,)), ((), ())), preferred_element_type=F32)


def _dot_tn(a, b):
    return lax.dot_general(a, b, (((0,), (0,)), ((), ())), preferred_element_type=F32)


def _ln(x, g, b):
    mu = jnp.mean(x, -1, keepdims=True)
    xc = x - mu
    var = jnp.mean(xc * xc, -1, keepdims=True)
    return xc * lax.rsqrt(var + LN_EPS) * g + b


def _silu(x):
    return x * jax.nn.sigmoid(x)


def _gelu_tanh(x):
    return 0.5 * x * (1.0 + jnp.tanh(math.sqrt(2.0 / math.pi) * (x + 0.044715 * (x * x * x))))


def _mod_kernel(c_ref, w_ref, b_ref, o_ref):
    s = _silu(c_ref[...])
    o_ref[0] = _dot(s.astype(BF16), w_ref[0].astype(BF16)) + b_ref[0]


def _modulation(cvec, w_mod, b_mod):
    depth, d, n = w_mod.shape
    tn = n // 4
    return pl.pallas_call(
        _mod_kernel,
        out_shape=jax.ShapeDtypeStruct((depth, 8, n), F32),
        grid=(depth, n // tn),
        in_specs=[pl.BlockSpec((8, d), lambda i, j: (0, 0)),
                  pl.BlockSpec((1, d, tn), lambda i, j: (i, 0, j)),
                  pl.BlockSpec((1, 1, tn), lambda i, j: (i, 0, j))],
        out_specs=pl.BlockSpec((1, 8, tn), lambda i, j: (i, 0, j)),
        compiler_params=_params(2), name="modulation",
    )(cvec, w_mod, b_mod.reshape(depth, 1, n))


def _pos_table_kernel(o_ref, *, q):
    rows, cols = o_ref.shape
    p = lax.broadcasted_iota(I32, (rows, cols), 0).astype(F32)
    lane = lax.broadcasted_iota(I32, (rows, cols), 1)
    j = jnp.where(lane >= q, lane - q, lane).astype(F32)
    omega = jnp.exp(j * (-math.log(10000.0) / q))
    ang = p * omega
    o_ref[...] = jnp.where(lane >= q, jnp.cos(ang), jnp.sin(ang))


def _pos_table(n, d):
    q = d // 4
    return pl.pallas_call(functools.partial(_pos_table_kernel, q=q),
                          out_shape=jax.ShapeDtypeStruct((n, 2 * q), F32), name="pos_table")()


def _add_pos_kernel(x_ref, er_ref, ec_ref, o_ref, *, half):
    tm = x_ref.shape[0]
    reps = tm // GRID_W
    er = er_ref[...]
    er_rows = jnp.broadcast_to(er[:, None, :], (reps, GRID_W, half)).reshape(tm, half)
    ec_rows = jnp.broadcast_to(ec_ref[...][None], (reps, GRID_W, half)).reshape(tm, half)
    o_ref[:, :half] = x_ref[:, :half] + er_rows
    o_ref[:, half:] = x_ref[:, half:] + ec_rows


def _add_pos(x2d, seq):
    n, d = x2d.shape
    half = d // 2
    rows = seq // GRID_W
    er = _pos_table(rows, d)
    ec = _pos_table(GRID_W, d)
    tm = ROW_TILE
    reps = tm // GRID_W
    tps = seq // tm
    return pl.pallas_call(
        functools.partial(_add_pos_kernel, half=half),
        out_shape=jax.ShapeDtypeStruct((n, d), F32),
        grid=(n // tm,),
        in_specs=[pl.BlockSpec((tm, d), lambda i: (i, 0)),
                  pl.BlockSpec((reps, half), lambda i: (i % tps, 0)),
                  pl.BlockSpec((GRID_W, half), lambda i: (0, 0))],
        out_specs=pl.BlockSpec((tm, d), lambda i: (i, 0)),
        compiler_params=_params(1), name="add_pos",
    )(x2d, er, ec)


def _gm_in_kernel(h_ref, sc_ref, sh_ref, w_ref, b_ref, g_ref, bb_ref, u_ref, v_ref):
    d = h_ref.shape[1]
    a = (h_ref[...] * (1.0 + sc_ref[0]) + sh_ref[0]).astype(BF16)
    u_ref[...] = _gelu_tanh(_dot(a, w_ref[:, :d]) + b_ref[:, :d])
    v = _gelu_tanh(_dot(a, w_ref[:, d:]) + b_ref[:, d:])
    v_ref[...] = _ln(v, g_ref[...], bb_ref[...]).astype(BF16)


def _gm_in(h, sc, sh, w, b, g, bb, seq):
    n, d = h.shape
    tm = min(ROW_TILE, seq)
    tps = seq // tm
    row = lambda i: (i, 0)
    per_b = lambda i: (i // tps, 0, 0)
    fixed = lambda i: (0, 0)
    return pl.pallas_call(
        _gm_in_kernel,
        out_shape=(jax.ShapeDtypeStruct((n, d), F32), jax.ShapeDtypeStruct((n, d), BF16)),
        grid=(n // tm,),
        in_specs=[pl.BlockSpec((tm, d), row), pl.BlockSpec((1, 1, d), per_b), pl.BlockSpec((1, 1, d), per_b),
                  pl.BlockSpec((d, 2 * d), fixed), pl.BlockSpec((1, 2 * d), fixed),
                  pl.BlockSpec((1, d), fixed), pl.BlockSpec((1, d), fixed)],
        out_specs=(pl.BlockSpec((tm, d), row), pl.BlockSpec((tm, d), row)),
        compiler_params=_params(1), name="gmlp_in",
    )(h, sc, sh, w, b, g, bb)


def _gm_prologue(u_ref, v_ref, ws_ref, bs_ref):
    tm, d = u_ref.shape
    dh = d // GM_HEADS
    for c in range(tm // GM_CHUNK):
        rows = slice(c * GM_CHUNK, (c + 1) * GM_CHUNK)
        parts = [_dot(ws_ref[g], v_ref[rows, g * dh:(g + 1) * dh]) for g in range(GM_HEADS)]
        vm = jnp.concatenate(parts, axis=1) + bs_ref[...]
        yield rows, (u_ref[rows, :] * vm).astype(BF16)


def _gla_prologue(of_ref, ob_ref, r_ref, ng_ref):
    tm, d = of_ref.shape
    dh = d // GLA_HEADS
    o = of_ref[...] + ob_ref[...]
    parts = []
    for hd in range(GLA_HEADS):
        oh = o[:, hd * dh:(hd + 1) * dh]
        mu = jnp.mean(oh, -1, keepdims=True)
        oc = oh - mu
        var = jnp.mean(oc * oc, -1, keepdims=True)
        parts.append(oc * lax.rsqrt(var + LN_EPS) * ng_ref[:, hd * dh:(hd + 1) * dh])
    y = jnp.concatenate(parts, axis=1) * _silu(r_ref[...])
    yield slice(0, tm), y.astype(BF16)


def _hy_prologue(y_ref):
    yield slice(0, y_ref.shape[0]), y_ref[...].astype(BF16)


_PROLOGUES = {"gmlp": (_gm_prologue, 4), "gla": (_gla_prologue, 4), "hyena": (_hy_prologue, 1)}


def _route_t(scores, biased):
    n_experts, tm = scores.shape
    gsz = n_experts // N_GROUPS
    neg = jnp.float32(-jnp.inf)
    v3 = biased.reshape(N_GROUPS, gsz, tm)
    sub = lax.broadcasted_iota(I32, v3.shape, 1).astype(F32)
    m1 = jnp.max(v3, axis=1, keepdims=True)
    i1 = jnp.min(jnp.where(v3 == m1, sub, float(gsz)), axis=1, keepdims=True)
    v3b = jnp.where(sub == i1, neg, v3)
    m2 = jnp.max(v3b, axis=1, keepdims=True)
    i2 = jnp.min(jnp.where(v3b == m2, sub, float(gsz)), axis=1, keepdims=True)
    gscore = (m1 + m2).reshape(N_GROUPS, tm)
    i1 = i1.reshape(N_GROUPS, tm)
    i2 = i2.reshape(N_GROUPS, tm)
    best, e0, e1 = gscore[0:1], i1[0:1], i2[0:1]
    for g in range(1, N_GROUPS):
        better = gscore[g:g + 1] > best
        best = jnp.where(better, gscore[g:g + 1], best)
        e0 = jnp.where(better, i1[g:g + 1] + float(g * gsz), e0)
        e1 = jnp.where(better, i2[g:g + 1] + float(g * gsz), e1)
    row = lax.broadcasted_iota(I32, scores.shape, 0).astype(F32)
    oh0 = (row == e0).astype(F32)
    oh1 = (row == e1).astype(F32)
    w0 = jnp.sum(oh0 * scores, axis=0, keepdims=True)
    w1 = jnp.sum(oh1 * scores, axis=0, keepdims=True)
    den = w0 + w1
    return e0, e1, oh0, oh1, w0 / den, w1 / den


def _post_kernel(*refs, kind, alpha):
    prologue, n_pro = _PROLOGUES[kind]
    pro = refs[:n_pro]
    (h_ref, w_ref, b_ref, g1_ref, lg_ref, lb_ref, sc_ref, sh_ref, rhl_ref, rb_ref, cin_ref,
     h1_ref, tok_ref, meta_ref, cout_ref, cnt_scr) = refs[n_pro:]
    i = pl.program_id(0)
    tm = h_ref.shape[0]
    n_experts = rb_ref.shape[0]

    @pl.when(i == 0)
    def _():
        cnt_scr[...] = cin_ref[...]

    for rows, y in prologue(*pro):
        out = _dot(y, w_ref[...]) + b_ref[...]
        h1_ref[rows, :] = _ln(alpha * h_ref[rows, :] + g1_ref[0] * out, lg_ref[...], lb_ref[...])
    tok = h1_ref[...] * (1.0 + sc_ref[0]) + sh_ref[0]
    tok_ref[...] = tok
    t_hi = tok.astype(BF16)
    t_lo = (tok - t_hi.astype(F32)).astype(BF16)
    p_hi = _dot_nt(rhl_ref[...], t_hi)
    logits = p_hi[:n_experts] + p_hi[n_experts:] + _dot_nt(rhl_ref[0:n_experts, :], t_lo)
    scores = jax.nn.sigmoid(logits)
    e0, e1, oh0, oh1, w0, w1 = _route_t(scores, scores + rb_ref[...])
    r_i = lax.broadcasted_iota(I32, (tm, tm), 0)
    c_i = lax.broadcasted_iota(I32, (tm, tm), 1)
    upper = (r_i < c_i).astype(BF16)
    oh = oh0 + oh1
    carry = cnt_scr[...]
    before = _dot(oh.astype(BF16), upper) + jnp.concatenate([carry] * (tm // LANES), axis=1)
    rank0 = jnp.sum(oh0 * before, axis=0, keepdims=True)
    rank1 = jnp.sum(oh1 * before, axis=0, keepdims=True)
    cnt_scr[...] = carry + jnp.sum(oh, axis=1, keepdims=True)
    zero = jnp.zeros_like(w0)
    meta_ref[0] = jnp.concatenate([e0, e1, rank0, rank1, w0, w1, zero, zero], axis=0)
    cout_ref[...] = cnt_scr[...]


def _post(kind, pro_args, pro_specs, h, w, b, g1, lg, lb, sc, sh, rhl, rb, cnt_in, seq, alpha):
    n, d = h.shape
    tm = min(ROW_TILE, seq)
    tps = seq // tm
    n_experts = rb.shape[0]
    row = lambda i: (i, 0)
    per_b = lambda i: (i // tps, 0, 0)
    fixed = lambda i: (0, 0)
    in_specs = list(pro_specs(tm, d)) + [
        pl.BlockSpec((tm, d), row), pl.BlockSpec((d, d), fixed), pl.BlockSpec((1, d), fixed),
        pl.BlockSpec((1, 1, d), per_b), pl.BlockSpec((1, d), fixed), pl.BlockSpec((1, d), fixed),
        pl.BlockSpec((1, 1, d), per_b), pl.BlockSpec((1, 1, d), per_b),
        pl.BlockSpec((2 * n_experts, d), fixed), pl.BlockSpec((n_experts, tm), fixed),
        pl.BlockSpec((n_experts, LANES), fixed)]
    return pl.pallas_call(
        functools.partial(_post_kernel, kind=kind, alpha=alpha),
        out_shape=(jax.ShapeDtypeStruct((n, d), F32), jax.ShapeDtypeStruct((n, d), F32),
                   jax.ShapeDtypeStruct((n // tm, SUBLANES, tm), F32),
                   jax.ShapeDtypeStruct((n_experts, LANES), F32)),
        grid=(n // tm,),
        in_specs=in_specs,
        out_specs=(pl.BlockSpec((tm, d), row), pl.BlockSpec((tm, d), row),
                   pl.BlockSpec((1, SUBLANES, tm), lambda i: (i, 0, 0)), pl.BlockSpec((n_experts, LANES), fixed)),
        scratch_shapes=[pltpu.VMEM((n_experts, LANES), F32)],
        compiler_params=_params(1), name="post_" + kind,
    )(*pro_args, h, w, b, g1, lg, lb, sc, sh, rhl, rb[:, :tm], cnt_in)


def _dest_kernel(meta_ref, pstart_ref, o_ref):
    meta = meta_ref[0]
    n_experts = pstart_ref.shape[0]
    tm = meta.shape[1]
    row = lax.broadcasted_iota(I32, (n_experts, tm), 0).astype(F32)
    rows = []
    for k in range(TOP_K):
        off = jnp.sum(jnp.where(row == meta[k:k + 1], pstart_ref[...], 0.0), axis=0, keepdims=True)
        rows.append(off + meta[TOP_K + k:TOP_K + k + 1])
    rows += [jnp.zeros_like(rows[0])] * (SUBLANES - TOP_K)
    o_ref[0] = jnp.concatenate(rows, axis=0).astype(I32)


def _dest(meta, pstart_b):
    ntile, _, tmeta = meta.shape
    tm = DMA_TILE
    per = tmeta // tm
    n_experts = pstart_b.shape[0]
    return pl.pallas_call(
        _dest_kernel,
        out_shape=jax.ShapeDtypeStruct((ntile * per, SUBLANES, tm), I32),
        grid=(ntile * per,),
        in_specs=[pl.BlockSpec((1, SUBLANES, tm), lambda i: (i // per, 0, i % per)),
                  pl.BlockSpec((n_experts, tm), lambda i: (0, 0))],
        out_specs=pl.BlockSpec((1, SUBLANES, tm), lambda i: (i, 0, 0)),
        compiler_params=_params(1), name="moe_dest",
    )(meta, pstart_b[:, :tm])


def _row_loop(tm, body):
    def step(it, c):
        for u in range(DMA_UNROLL):
            body(it * DMA_UNROLL + u, u)
        return c
    lax.fori_loop(0, tm // DMA_UNROLL, step, 0)


def _dispatch_kernel(lb_ref, dest_ref, tok_ref, *rest, fill):
    if fill:
        xs_ref, idx_scr, zero_scr, isem, rsem, zsem = rest
    else:
        _, xs_ref, idx_scr, zero_scr, isem, rsem, zsem = rest
    i = pl.program_id(0)
    nstep = pl.num_programs(0)
    tm = idx_scr.shape[1]

    if fill:
        @pl.when(i == 0)
        def _():
            zero_scr[...] = jnp.zeros_like(zero_scr)

            def fill_copy(e):
                start_row = pl.multiple_of(jnp.maximum(lb_ref[e], 0), MOE_ROWS)
                return pltpu.make_async_copy(zero_scr, xs_ref.at[pl.ds(start_row, MOE_ROWS)], zsem)

            def start(e, c):
                @pl.when(lb_ref[e] >= 0)
                def _():
                    fill_copy(e).start()
                return c

            def wait(e, c):
                @pl.when(lb_ref[e] >= 0)
                def _():
                    fill_copy(e).wait()
                return c

            lax.fori_loop(0, lb_ref.shape[0], start, 0)
            lax.fori_loop(0, lb_ref.shape[0], wait, 0)

    for k in range(TOP_K):
        pltpu.make_async_copy(dest_ref.at[i, k], idx_scr.at[k], isem).start()
    for k in range(TOP_K):
        pltpu.make_async_copy(dest_ref.at[i, k], idx_scr.at[k], isem).wait()

    def row_copy(r, k):
        return pltpu.make_async_copy(tok_ref.at[pl.ds(i * tm + r, 1)], xs_ref.at[pl.ds(idx_scr[k, r], 1)], rsem)

    def start_row(r, u):
        for k in range(TOP_K):
            row_copy(r, k).start(priority=(u + k) % 2)

    def wait_row(r, u):
        for k in range(TOP_K):
            row_copy(r, k).wait()

    _row_loop(tm, start_row)

    @pl.when(i > 0)
    def _():
        _row_loop(tm, wait_row)

    @pl.when(i == nstep - 1)
    def _():
        _row_loop(tm, wait_row)


def _dispatch(last_blk, dest, tok, xs, p):
    n, d = tok.shape
    tm = DMA_TILE
    fill = xs is None
    any_spec = pl.BlockSpec(memory_space=pl.ANY)
    args = [last_blk, dest, tok] + ([] if fill else [xs])
    return pl.pallas_call(
        functools.partial(_dispatch_kernel, fill=fill),
        out_shape=jax.ShapeDtypeStruct((p, d), F32),
        grid_spec=pltpu.PrefetchScalarGridSpec(
            num_scalar_prefetch=1, grid=(n // tm,), in_specs=[any_spec] * (len(args) - 1), out_specs=any_spec,
            scratch_shapes=[pltpu.SMEM((TOP_K, tm), I32), pltpu.VMEM((MOE_ROWS, d), F32),
                            pltpu.SemaphoreType.DMA, pltpu.SemaphoreType.DMA, pltpu.SemaphoreType.DMA]),
        input_output_aliases={} if fill else {3: 0},
        compiler_params=_params(1), name="moe_dispatch",
    )(*args)


def _expert_kernel(be_ref, nu_ref, x_ref, w1_ref, w3_ref, w2_ref, y_ref, w1_scr, w3_scr, w2_scr):
    j = pl.program_id(0)
    e = be_ref[j]
    e_prev = be_ref[jnp.maximum(j - 1, 0)]

    @pl.when(j < nu_ref[0])
    def _():
        @pl.when((j == 0) | (e != e_prev))
        def _():
            w1_scr[...] = w1_ref[0, 0].astype(BF16)
            w3_scr[...] = w3_ref[0, 0].astype(BF16)
            w2_scr[...] = w2_ref[0, 0].astype(BF16)

        x = x_ref[...].astype(BF16)
        hid = _silu(_dot(x, w1_scr[...])) * _dot(x, w3_scr[...])
        y_ref[...] = _dot(hid.astype(BF16), w2_scr[...])

    @pl.when(j >= nu_ref[0])
    def _():
        y_ref[...] = jnp.zeros_like(y_ref)


def _experts(blk_e, n_used, xs, w1, w3, w2, layer):
    p, d = xs.shape
    de = w1.shape[-1]
    nblk = p // MOE_ROWS
    wmap = lambda j, be, nu: (layer, be[j], 0, 0)
    xmap = lambda j, be, nu: (jnp.minimum(j, nu[0] - 1), 0)
    return pl.pallas_call(
        _expert_kernel,
        out_shape=jax.ShapeDtypeStruct((p, d), F32),
        grid_spec=pltpu.PrefetchScalarGridSpec(
            num_scalar_prefetch=2, grid=(nblk,),
            in_specs=[pl.BlockSpec((MOE_ROWS, d), xmap),
                      pl.BlockSpec((1, 1, d, de), wmap), pl.BlockSpec((1, 1, d, de), wmap),
                      pl.BlockSpec((1, 1, de, d), wmap)],
            out_specs=pl.BlockSpec((MOE_ROWS, d), lambda j, be, nu: (j, 0)),
            scratch_shapes=[pltpu.VMEM((d, de), BF16), pltpu.VMEM((d, de), BF16), pltpu.VMEM((de, d), BF16)]),
        compiler_params=_params(1), name="moe_experts",
    )(blk_e, n_used, xs, w1, w3, w2)


def _combine_kernel(dest_ref, ys_ref, meta_ref, h_ref, g2_ref, lg_ref, lb_ref, o_ref,
                    idx_scr, buf_scr, isem, rsem, *, alpha):
    i = pl.program_id(0)
    nstep = pl.num_programs(0)
    tm = h_ref.shape[0]

    def gather(step, slot):
        for k in range(TOP_K):
            pltpu.make_async_copy(dest_ref.at[step, k], idx_scr.at[slot, k], isem).start()
        for k in range(TOP_K):
            pltpu.make_async_copy(dest_ref.at[step, k], idx_scr.at[slot, k], isem).wait()

        def start_row(r, u):
            for k in range(TOP_K):
                pltpu.make_async_copy(ys_ref.at[pl.ds(idx_scr[slot, k, r], 1)], buf_scr.at[slot, k, pl.ds(r, 1)],
                                      rsem.at[slot]).start(priority=(u + k) % 2)

        _row_loop(tm, start_row)

    slot = i % 2

    @pl.when(i == 0)
    def _():
        gather(0, 0)

    @pl.when(i + 1 < nstep)
    def _():
        gather(i + 1, 1 - slot)

    def wait_row(r, u):
        for k in range(TOP_K):
            pltpu.make_async_copy(ys_ref.at[pl.ds(0, 1)], buf_scr.at[slot, k, pl.ds(r, 1)], rsem.at[slot]).wait()

    _row_loop(tm, wait_row)
    meta = meta_ref[0]
    wt = jnp.concatenate([meta, jnp.zeros((LANES - SUBLANES, tm), F32)], axis=0).T
    f = wt[:, 2 * TOP_K:2 * TOP_K + 1] * buf_scr[slot, 0]
    for k in range(1, TOP_K):
        f = f + wt[:, 2 * TOP_K + k:2 * TOP_K + k + 1] * buf_scr[slot, k]
    o_ref[...] = _ln(alpha * h_ref[...] + g2_ref[0] * f, lg_ref[...], lb_ref[...])


def _combine(dest, ys, meta, h, g2, lg, lb, seq, alpha):
    n, d = h.shape
    tm = DMA_TILE
    tps = seq // tm
    per = meta.shape[2] // tm
    row = lambda i: (i, 0)
    fixed = lambda i: (0, 0)
    return pl.pallas_call(
        functools.partial(_combine_kernel, alpha=alpha),
        out_shape=jax.ShapeDtypeStruct((n, d), F32),
        grid=(n // tm,),
        in_specs=[pl.BlockSpec(memory_space=pl.ANY), pl.BlockSpec(memory_space=pl.ANY),
                  pl.BlockSpec((1, SUBLANES, tm), lambda i: (i // per, 0, i % per)), pl.BlockSpec((tm, d), row),
                  pl.BlockSpec((1, 1, d), lambda i: (i // tps, 0, 0)),
                  pl.BlockSpec((1, d), fixed), pl.BlockSpec((1, d), fixed)],
        out_specs=pl.BlockSpec((tm, d), row),
        scratch_shapes=[pltpu.SMEM((2, TOP_K, tm), I32), pltpu.VMEM((2, TOP_K, tm, d), F32),
                        pltpu.SemaphoreType.DMA, pltpu.SemaphoreType.DMA((2,))],
        compiler_params=_params(1), name="moe_combine",
    )(dest, ys, meta, h, g2, lg, lb)


def _moe(streams, counts, w1, w3, w2, layer, alpha):
    n_experts = w1.shape[1]
    d = streams[0]["tok"].shape[1]
    n_assign = TOP_K * sum(s["tok"].shape[0] for s in streams)
    p = n_assign + n_experts * MOE_ROWS
    nblk = p // MOE_ROWS
    cnt = counts[:, 0].astype(I32)
    padded = (cnt + MOE_ROWS - 1) // MOE_ROWS * MOE_ROWS
    pend = jnp.cumsum(padded)
    pstart = pend - padded
    n_used = (pend[-1] // MOE_ROWS).astype(I32).reshape(1)
    blk = jnp.arange(nblk, dtype=I32) * MOE_ROWS
    blk_e = jnp.sum((pend[None, :] <= blk[:, None]).astype(I32), axis=1)
    last_e = jnp.sum((pend <= jnp.maximum(pend[-1] - 1, 0)).astype(I32))
    blk_e = jnp.minimum(blk_e, last_e).astype(I32)
    last_blk = jnp.where(padded > 0, pend - MOE_ROWS, -1).astype(I32)
    pstart_b = jnp.broadcast_to(pstart.astype(F32)[:, None], (n_experts, ROW_TILE))
    dests = [_dest(s["meta"], pstart_b) for s in streams]
    xs = None
    for s, dst in zip(streams, dests):
        xs = _dispatch(last_blk, dst, s["tok"], xs, p)
    ys = _experts(blk_e, n_used, xs, w1, w3, w2, layer)
    return [_combine(dst, ys, s["meta"], s["h1"], s["g2"], s["lg"], s["lb"], s["seq"], alpha)
            for s, dst in zip(streams, dests)]


def _log_sigmoid(x):
    return jnp.minimum(x, 0.0) - jnp.log(1.0 + jnp.exp(-jnp.abs(x)))


def _gla_in_kernel(h_ref, sc_ref, sh_ref, w_ref, wl_ref, wg_ref, bg_ref, qk_ref, v_ref, r_ref, g_ref, *, qscale):
    tm, d = h_ref.shape
    a = (h_ref[...] * (1.0 + sc_ref[0]) + sh_ref[0]).astype(BF16)
    qk = _dot(a, w_ref[:, :d])
    lane = lax.broadcasted_iota(I32, (tm, d), 1)
    qk_ref[...] = jnp.where(lane < d // 2, qk * qscale, qk)
    v_ref[...] = _dot(a, w_ref[:, d:2 * d]).astype(BF16)
    r_ref[...] = _dot(a, w_ref[:, 2 * d:])
    lr = _dot(a, wl_ref[...])
    gpre = _dot(lr.astype(BF16), wg_ref[...]) + bg_ref[...]
    g_ref[...] = _log_sigmoid(gpre) * (1.0 / GLA_TAU)


def _gla_weights(w_in, w_gate, b_gate, d):
    dk = d // 2
    n_lr = HY_DIRS * GLA_RANK
    w_main = w_in[:, :2 * dk + 2 * d].astype(BF16)
    w_lr = jnp.zeros((d, LANES), F32).at[:, :n_lr].set(w_in[:, 2 * dk + 2 * d:]).astype(BF16)
    w_g = jnp.zeros((LANES, HY_DIRS * dk), F32)
    for dr in range(HY_DIRS):
        w_g = w_g.at[dr * GLA_RANK:(dr + 1) * GLA_RANK, dr * dk:(dr + 1) * dk].set(w_gate[dr])
    return w_main, w_lr, w_g.astype(BF16), b_gate.reshape(1, HY_DIRS * dk)


def _gla_in(h, sc, sh, w_main, w_lr, w_g, b_g, seq):
    n, d = h.shape
    tm = min(ROW_TILE, seq)
    tps = seq // tm
    row = lambda i: (i, 0)
    per_b = lambda i: (i // tps, 0, 0)
    fixed = lambda i: (0, 0)
    qscale = float(((d // 2) // GLA_HEADS) ** -0.5)
    return pl.pallas_call(
        functools.partial(_gla_in_kernel, qscale=qscale),
        out_shape=(jax.ShapeDtypeStruct((n, d), F32), jax.ShapeDtypeStruct((n, d), BF16),
                   jax.ShapeDtypeStruct((n, d), F32), jax.ShapeDtypeStruct((n, d), F32)),
        grid=(n // tm,),
        in_specs=[pl.BlockSpec((tm, d), row), pl.BlockSpec((1, 1, d), per_b), pl.BlockSpec((1, 1, d), per_b),
                  pl.BlockSpec((d, 3 * d), fixed), pl.BlockSpec((d, LANES), fixed),
                  pl.BlockSpec((LANES, d), fixed), pl.BlockSpec((1, d), fixed)],
        out_specs=tuple(pl.BlockSpec((tm, d), row) for _ in range(4)),
        compiler_params=_params(1), name="gla_in",
    )(h, sc, sh, w_main, w_lr, w_g, b_g)


def _gla_direction(q_ref, k_ref, v_ref, g_ref, st_scr, o_ref, reverse):
    rt, dk = q_ref.shape
    cs = GLA_CHUNK
    nch = rt // cs
    g = g_ref[...]
    pos = lax.broadcasted_iota(I32, (rt, dk), 0) % cs
    b = g
    sh = 1
    while sh < cs:
        if reverse:
            b = b + jnp.where(pos < cs - sh, pltpu.roll(b, rt - sh, 0), 0.0)
        else:
            b = b + jnp.where(pos >= sh, pltpu.roll(b, sh, 0), 0.0)
        sh *= 2
    edge = 0 if reverse else cs - 1
    b3 = b.reshape(nch, cs, dk)
    b_edge = b3[:, edge:edge + 1, :]
    q = q_ref[...]
    k = k_ref[...]
    qe = (q * jnp.exp(b)).astype(BF16)
    ke = (k * jnp.exp(-b)).astype(BF16)
    kd = (k.reshape(nch, cs, dk) * jnp.exp(b_edge - b3)).astype(BF16)
    decay = jnp.exp(b_edge)
    r_i = lax.broadcasted_iota(I32, (rt, rt), 0)
    c_i = lax.broadcasted_iota(I32, (rt, rt), 1)
    same = (r_i // cs) == (c_i // cs)
    tri = (c_i >= r_i) if reverse else (c_i <= r_i)
    att = jnp.where(same & tri, _dot_nt(qe, ke), 0.0).astype(BF16)
    v = v_ref[...]
    intra = _dot(att, v)
    st = st_scr[...]
    order = range(nch - 1, -1, -1) if reverse else range(nch)
    for j in order:
        rows = slice(j * cs, (j + 1) * cs)
        o_ref[rows, :] = intra[rows, :] + _dot_nt(qe[rows, :], st.astype(BF16))
        st = st * decay[j] + _dot_tn(v[rows, :], kd[j])
    st_scr[...] = st


def _gla_scan_kernel(qf, kf, vf, gf, qb, kb, vb, gb, s0f, s0b, of, ob, sfo, sbo, sf_scr, sb_scr):
    c = pl.program_id(2)

    @pl.when(c == 0)
    def _():
        sf_scr[...] = s0f[0, 0]
        sb_scr[...] = s0b[0, 0]

    _gla_direction(qf, kf, vf, gf, sf_scr, of, False)
    _gla_direction(qb, kb, vb, gb, sb_scr, ob, True)
    sfo[0, 0] = sf_scr[...]
    sbo[0, 0] = sb_scr[...]


def _gla_scan(qk, v, r, g, s0f, s0b, nb, seq):
    del r
    n, d = qk.shape
    nh = GLA_HEADS
    dkh, dvh = (d // 2) // nh, d // nh
    rt = min(ROW_TILE, seq)
    npb = seq // rt
    fw = lambda col: (lambda b, hd, c: (b * npb + c, col(hd)))
    bw = lambda col: (lambda b, hd, c: (b * npb + npb - 1 - c, col(hd)))
    qcol, kcol = (lambda hd: hd), (lambda hd: nh + hd)
    st = lambda b, hd, c: (b, hd, 0, 0)
    state = jax.ShapeDtypeStruct((nb, nh, dvh, dkh), F32)
    return pl.pallas_call(
        _gla_scan_kernel,
        out_shape=(jax.ShapeDtypeStruct((n, d), F32), jax.ShapeDtypeStruct((n, d), F32), state, state),
        grid=(nb, nh, npb),
        in_specs=[pl.BlockSpec((rt, dkh), fw(qcol)), pl.BlockSpec((rt, dkh), fw(kcol)),
                  pl.BlockSpec((rt, dvh), fw(qcol)), pl.BlockSpec((rt, dkh), fw(qcol)),
                  pl.BlockSpec((rt, dkh), bw(qcol)), pl.BlockSpec((rt, dkh), bw(kcol)),
                  pl.BlockSpec((rt, dvh), bw(qcol)), pl.BlockSpec((rt, dkh), bw(kcol)),
                  pl.BlockSpec((1, 1, dvh, dkh), st), pl.BlockSpec((1, 1, dvh, dkh), st)],
        out_specs=(pl.BlockSpec((rt, dvh), fw(qcol)), pl.BlockSpec((rt, dvh), bw(qcol)),
                   pl.BlockSpec((1, 1, dvh, dkh), st), pl.BlockSpec((1, 1, dvh, dkh), st)),
        scratch_shapes=[pltpu.VMEM((dvh, dkh), F32), pltpu.VMEM((dvh, dkh), F32)],
        compiler_params=_params(3), name="gla_scan",
    )(qk, qk, v, g, qk, qk, v, g, s0f, s0b)


def _hy_in_kernel(h_ref, hp_ref, hn_ref, sc_ref, sh_ref, w_ref, b_ref, cw_ref, cb_ref, o_ref, a_scr, z_scr,
                  *, tps, halo):
    i = pl.program_id(0)
    tm, d = h_ref.shape
    sc = 1.0 + sc_ref[0]
    sh = sh_ref[0]
    a_scr[0:halo, :] = (hp_ref[...] * sc + sh).astype(BF16)
    a_scr[halo:halo + tm, :] = (h_ref[...] * sc + sh).astype(BF16)
    a_scr[halo + tm:, :] = (hn_ref[...] * sc + sh).astype(BF16)
    row = lax.broadcasted_iota(I32, (tm, 1), 0)
    first = jnp.where(i % tps == 0, 0, -1)
    last = jnp.where(i % tps == tps - 1, tm - 1, -1)
    for c in range(o_ref.shape[0]):
        cols = slice(c * d, (c + 1) * d)
        z_scr[...] = _dot(a_scr[...], w_ref[:, cols]) + b_ref[:, cols]
        zp = jnp.where(row == first, 0.0, z_scr[pl.ds(halo - 1, tm), :])
        zc = z_scr[pl.ds(halo, tm), :]
        zn = jnp.where(row == last, 0.0, z_scr[pl.ds(halo + 1, tm), :])
        o_ref[c] = cw_ref[0:1, cols] * zp + cw_ref[1:2, cols] * zc + cw_ref[2:3, cols] * zn + cb_ref[:, cols]


def _hy_in(h, sc, sh, w, b, cw, cb, seq):
    n, d = h.shape
    nsplit = w.shape[1] // d
    halo = 16
    tm = min(ROW_TILE, seq)
    tps = seq // tm
    hb = tm // halo
    row = lambda i: (i, 0)
    per_b = lambda i: (i // tps, 0, 0)
    fixed = lambda i: (0, 0)
    return pl.pallas_call(
        functools.partial(_hy_in_kernel, tps=tps, halo=halo),
        out_shape=jax.ShapeDtypeStruct((nsplit, n, d), F32),
        grid=(n // tm,),
        in_specs=[pl.BlockSpec((tm, d), row),
                  pl.BlockSpec((halo, d), lambda i: (jnp.maximum(i * hb - 1, 0), 0)),
                  pl.BlockSpec((halo, d), lambda i: (jnp.minimum((i + 1) * hb, n // halo - 1), 0)),
                  pl.BlockSpec((1, 1, d), per_b), pl.BlockSpec((1, 1, d), per_b),
                  pl.BlockSpec((d, nsplit * d), fixed), pl.BlockSpec((1, nsplit * d), fixed),
                  pl.BlockSpec((cw.shape[0], nsplit * d), fixed), pl.BlockSpec((1, nsplit * d), fixed)],
        out_specs=pl.BlockSpec((nsplit, tm, d), lambda i: (0, i, 0)),
        scratch_shapes=[pltpu.VMEM((tm + 2 * halo, d), BF16), pltpu.VMEM((tm + 2 * halo, d), F32)],
        compiler_params=_params(1), name="hyena_in",
    )(h, h, h, sc, sh, w, b, cw, cb)


def _hy_filter_rows(t, seq, w1_ref, b1_ref, w2_ref, b2_ref, w3_ref, b3_ref, n_bands):
    rows = t.shape[0]
    tf = t.astype(F32)
    t_lin = tf * (1.0 / (seq - 1))
    wpos = tf * (2.0 * math.pi / seq)
    lane = lax.broadcasted_iota(I32, (rows, LANES), 1)
    jb = jnp.where(lane > n_bands, lane - n_bands - 1, lane - 1).astype(F32)
    band = 1e-4 + jb * ((n_bands - 1 - 1e-4) / (n_bands - 1))
    ang = band * wpos
    z = jnp.where(lane == 0, t_lin,
                  jnp.where(lane <= n_bands, jnp.cos(ang), jnp.where(lane <= 2 * n_bands, -jnp.sin(ang), 0.0)))
    hf = jnp.sin(_dot(z.astype(BF16), w1_ref[...]) + b1_ref[...])
    hf = jnp.sin(_dot(hf.astype(BF16), w2_ref[...]) + b2_ref[...])
    hf = jnp.sin(_dot(hf.astype(BF16), w3_ref[...]) + b3_ref[...])
    return hf.astype(BF16), t_lin


def _hy_taps_kernel(w1_ref, b1_ref, w2_ref, b2_ref, w3_ref, b3_ref, w4_ref, o_ref, *, seq, n_bands):
    i = pl.program_id(0)
    n_ord, tr, d = o_ref.shape
    mlp = (w1_ref, b1_ref, w2_ref, b2_ref, w3_ref, b3_ref)
    r0 = i * tr
    second = (r0 >= seq).astype(I32)
    n_idx = r0 + lax.broadcasted_iota(I32, (tr, 1), 0)
    t = jnp.where(second == 1, 2 * seq - n_idx, n_idx)
    hf, t_lin = _hy_filter_rows(t, seq, *mlp, n_bands)
    c_idx = lax.broadcasted_iota(I32, (1, d), 1).astype(F32)
    min_decay = math.log(HY_DECAY_TARGET) / HY_DECAY_LONG_PCT
    max_decay = math.log(HY_DECAY_TARGET) / HY_DECAY_SHORT_PCT
    delta = jnp.abs(min_decay + c_idx * ((max_decay - min_decay) / (d - 1)))
    window = jnp.where(n_idx == seq, 0.0, jnp.exp(-t_lin * delta))
    for o in range(n_ord):
        o_ref[o] = _dot(hf, w4_ref[second, o]) * window

    @pl.when(i == 0)
    def _():
        hf0, _ = _hy_filter_rows(jnp.zeros((8, 1), I32), seq, *mlp, n_bands)
        first = lax.broadcasted_iota(I32, (8, 1), 0) == 0
        for o in range(n_ord):
            o_ref[o, 0:8, :] = o_ref[o, 0:8, :] + jnp.where(first, _dot(hf0, w4_ref[1, o]), 0.0)


def _hy_taps(seq, d, w1, b1, w2, b2, w3, b3, w4):
    emb, ff = w1.shape
    n_bands = (emb - 1) // 2
    w1p = jnp.zeros((LANES, ff), F32).at[:emb].set(w1).astype(BF16)
    w4r = w4.reshape(ff, HY_ORDER, HY_DIRS, d).transpose(2, 1, 0, 3).astype(BF16)
    tr = min(ROW_TILE, seq)
    fixed = lambda i: (0, 0)
    return pl.pallas_call(
        functools.partial(_hy_taps_kernel, seq=seq, n_bands=n_bands),
        out_shape=jax.ShapeDtypeStruct((HY_ORDER, 2 * seq, d), F32),
        grid=(2 * seq // tr,),
        in_specs=[pl.BlockSpec((LANES, ff), fixed), pl.BlockSpec((1, ff), fixed),
                  pl.BlockSpec((ff, ff), fixed), pl.BlockSpec((1, ff), fixed),
                  pl.BlockSpec((ff, ff), fixed), pl.BlockSpec((1, ff), fixed),
                  pl.BlockSpec((HY_DIRS, HY_ORDER, ff, d), lambda i: (0, 0, 0, 0))],
        out_specs=pl.BlockSpec((HY_ORDER, tr, d), lambda i: (0, i, 0)),
        compiler_params=_params(1), name="hyena_taps",
    )(w1p, b1.reshape(1, ff), w2.astype(BF16), b2.reshape(1, ff), w3.astype(BF16), b3.reshape(1, ff), w4r)


def _cos_sin(num, den):
    ang = (num % den).astype(F32) * (2.0 * math.pi / den)
    return jnp.cos(ang), jnp.sin(ang)


def _fft_tables(seq, n2):
    n = 2 * seq
    n1 = n // n2
    half = seq // n2
    ar = jnp.arange(n1, dtype=I32)
    ca, sa = _cos_sin(ar[:, None] * ar[None, :], n1)
    fa_half = jnp.concatenate([ca[:, :half], sa[:, :half]], 0).astype(BF16)
    fa_full = jnp.concatenate([ca, sa], 0).astype(BF16)
    ia = (jnp.concatenate([ca[:half], sa[:half]], 0) * (1.0 / n)).astype(BF16)
    k = ar[:, None, None] + n1 * jnp.arange(n2, dtype=I32)[None, :, None]
    cb, sb = _cos_sin(k * jnp.arange(n2, dtype=I32)[None, None, :], n)
    fb = jnp.concatenate([cb, sb], 1).astype(BF16)
    gb = jnp.concatenate([cb.transpose(0, 2, 1), sb.transpose(0, 2, 1)], 1).astype(BF16)
    return fa_half, fa_full, ia, fb, gb


def _pack_c(re, im):
    hi = pltpu.bitcast(re.astype(BF16).astype(F32), U32)
    lo = pltpu.bitcast(im.astype(BF16).astype(F32), U32)
    return hi | (lo >> 16)


def _unpack_c(w):
    re = pltpu.bitcast(w & jnp.uint32(0xFFFF0000), F32)
    im = pltpu.bitcast(w << 16, F32)
    return jnp.concatenate([re, im], axis=1).astype(BF16)


def _cmul_split(p, rows, cols, conj):
    a, b, c, d = p[:rows, :cols], p[:rows, cols:], p[rows:, :cols], p[rows:, cols:]
    return (a - d, b + c) if conj else (a + d, b - c)


def _fft_a_kernel(x_ref, f_ref, o_ref, *, cplx):
    n1 = f_ref.shape[0] // 2
    g, d = x_ref.shape[-2:]
    for s in range(g):
        if cplx:
            z = jnp.concatenate([x_ref[0, 0, :, s, :], x_ref[0, 1, :, s, :]], axis=1).astype(BF16)
            ar, ai = _cmul_split(_dot(f_ref[...], z), n1, d, False)
        else:
            p = _dot(f_ref[...], x_ref[0, 0, :, s, :].astype(BF16))
            ar, ai = p[:n1], -p[n1:]
        o_ref[0, :, s, :] = _pack_c(ar, ai)


def _fft_a(x5, which, f, cplx):
    _, nb, rows, n2, d = x5.shape
    n1 = f.shape[0] // 2
    per = 2 if cplx else 1
    groups = nb // per
    g = FFT_GROUP
    return pl.pallas_call(
        functools.partial(_fft_a_kernel, cplx=cplx),
        out_shape=jax.ShapeDtypeStruct((groups, n1, n2, d), U32),
        grid=(groups, n2 // g),
        in_specs=[pl.BlockSpec((1, per, rows, g, d), lambda p, j: (which, p, 0, j, 0)),
                  pl.BlockSpec((2 * n1, rows), lambda p, j: (0, 0))],
        out_specs=pl.BlockSpec((1, n1, g, d), lambda p, j: (p, 0, j, 0)),
        compiler_params=_params(2), name="fft_stage_a",
    )(x5, f)


def _fft_b_kernel(*refs, conv):
    if conv:
        a_ref, fb_ref, k_ref, gb_ref, o_ref = refs
    else:
        a_ref, fb_ref, o_ref = refs
    g, n2, d = a_ref.shape[1:]
    for s in range(g):
        xr, xi = _cmul_split(_dot(fb_ref[s], _unpack_c(a_ref[0, s])), n2, d, False)
        if not conv:
            o_ref[0, 0, s] = xr
            o_ref[0, 1, s] = xi
            continue
        kr, ki = k_ref[0, 0, s], k_ref[0, 1, s]
        y = jnp.concatenate([xr * kr - xi * ki, xr * ki + xi * kr], axis=1).astype(BF16)
        yr, yi = _cmul_split(_dot(gb_ref[s], y), n2, d, True)
        o_ref[0, :, s, :] = _pack_c(yr, yi)


def _fft_b(a4, fb, kspec=None, which=0, gb=None):
    groups, n1, n2, d = a4.shape
    conv = kspec is not None
    g = FFT_GROUP
    slab = pl.BlockSpec((1, g, n2, d), lambda k, p: (p, k, 0, 0))
    tab = pl.BlockSpec((g, 2 * n2, n2), lambda k, p: (k, 0, 0))
    if conv:
        in_specs = [slab, tab, pl.BlockSpec((1, 2, g, n2, d), lambda k, p: (which, 0, k, 0, 0)), tab]
        args = [a4, fb, kspec, gb]
        out_shape = jax.ShapeDtypeStruct((groups, n2, n1, d), U32)
        out_spec = pl.BlockSpec((1, n2, g, d), lambda k, p: (p, 0, k, 0))
    else:
        in_specs = [slab, tab]
        args = [a4, fb]
        out_shape = jax.ShapeDtypeStruct((groups, 2, n1, n2, d), F32)
        out_spec = pl.BlockSpec((1, 2, g, n2, d), lambda k, p: (p, 0, k, 0, 0))
    return pl.pallas_call(
        functools.partial(_fft_b_kernel, conv=conv),
        out_shape=out_shape,
        grid=(n1 // g, groups),
        in_specs=in_specs, out_specs=out_spec,
        compiler_params=_params(2), name="fft_stage_b_conv" if conv else "fft_stage_b",
    )(*args)


def _fft_c_kernel(y_ref, ia_ref, u_ref, g_ref, skip_ref, o_ref):
    half = ia_ref.shape[0] // 2
    g, _, d = y_ref.shape[1:]
    for s in range(g):
        cr, ci = _cmul_split(_dot(ia_ref[...], _unpack_c(y_ref[0, s])), half, d, True)
        o_ref[0, :, s, :] = cr
        o_ref[1, :, s, :] = ci
    skip = skip_ref[...][None]
    for b in range(2):
        o_ref[b] = g_ref[0, b] * (o_ref[b] + u_ref[0, b] * skip)


def _fft_c(y4, ia, u5, u_which, g5, g_which, skip):
    groups, n2, n1, d = y4.shape
    half = ia.shape[0] // 2
    g = FFT_GROUP
    return pl.pallas_call(
        _fft_c_kernel,
        out_shape=jax.ShapeDtypeStruct((2 * groups, half, n2, d), F32),
        grid=(groups, n2 // g),
        in_specs=[pl.BlockSpec((1, g, n1, d), lambda p, j: (p, j, 0, 0)),
                  pl.BlockSpec((2 * half, n1), lambda p, j: (0, 0)),
                  pl.BlockSpec((1, 2, half, g, d), lambda p, j: (u_which, p, 0, j, 0)),
                  pl.BlockSpec((1, 2, half, g, d), lambda p, j: (g_which, p, 0, j, 0)),
                  pl.BlockSpec((1, d), lambda p, j: (0, 0))],
        out_specs=pl.BlockSpec((2, half, g, d), lambda p, j: (p, 0, j, 0)),
        compiler_params=_params(2), name="fft_stage_c",
    )(y4, ia, u5, g5, skip.reshape(1, d))


def _hy_long_conv(zs, taps, skip, nb, seq):
    nsplit, n, d = zs.shape
    n2 = FFT_N2
    half = seq // n2
    n1 = 2 * half
    fa_half, fa_full, ia, fb, gb = _fft_tables(seq, n2)
    kspec = _fft_b(_fft_a(taps.reshape(1, HY_ORDER, n1, n2, d), 0, fa_full, False), fb)
    zs5 = zs.reshape(nsplit, nb, half, n2, d)
    u5, u_which = zs5, 0
    for o in range(HY_ORDER):
        a = _fft_a(u5, u_which, fa_half, True)
        y = _fft_b(a, fb, kspec, o, gb)
        out = _fft_c(y, ia, u5, u_which, zs5, o + 1, skip[o])
        u5, u_which = out[None], 0
    return out.reshape(n, d)


def _short_conv_kernel(u_ref, g_ref, t_ref, skip_ref, ff_ref, tf_ref, fi_ref, o_ref):
    seq, ct = u_ref.shape[-2:]
    n = 2 * seq
    z = jnp.concatenate([u_ref[0, 0], u_ref[0, 1]], axis=1).astype(BF16)
    xr, xi = _cmul_split(_dot(ff_ref[...], z), n, ct, False)
    pt = _dot(tf_ref[...], t_ref[0].astype(BF16))
    kr, ki = pt[:n], -pt[n:]
    y = jnp.concatenate([xr * kr - xi * ki, xr * ki + xi * kr], axis=1).astype(BF16)
    cr, ci = _cmul_split(_dot(fi_ref[...], y), seq, ct, True)
    skip = skip_ref[...]
    o_ref[0] = g_ref[0, 0] * (cr + u_ref[0, 0] * skip)
    o_ref[1] = g_ref[0, 1] * (ci + u_ref[0, 1] * skip)


def _hy_short_conv(zs, taps, skip, nb, seq):
    nsplit, n, d = zs.shape
    nn = 2 * seq
    ar = jnp.arange(nn, dtype=I32)
    cm, sm = _cos_sin(ar[:, None] * ar[None, :], nn)
    ff = jnp.concatenate([cm[:, :seq], sm[:, :seq]], 0).astype(BF16)
    tf = jnp.concatenate([cm, sm], 0).astype(BF16)
    fi = (jnp.concatenate([cm[:seq], sm[:seq]], 0) * (1.0 / nn)).astype(BF16)
    ct = d // 2
    zs4 = zs.reshape(nsplit, nb, seq, d)
    u4, u_which = zs4, 0
    for o in range(HY_ORDER):
        out = pl.pallas_call(
            _short_conv_kernel,
            out_shape=jax.ShapeDtypeStruct((nb, seq, d), F32),
            grid=(nb // 2, d // ct),
            in_specs=[pl.BlockSpec((1, 2, seq, ct), lambda p, j, w=u_which: (w, p, 0, j)),
                      pl.BlockSpec((1, 2, seq, ct), lambda p, j, w=o + 1: (w, p, 0, j)),
                      pl.BlockSpec((1, nn, ct), lambda p, j, w=o: (w, 0, j)),
                      pl.BlockSpec((1, ct), lambda p, j: (0, j)),
                      pl.BlockSpec((2 * nn, seq), lambda p, j: (0, 0)),
                      pl.BlockSpec((2 * nn, nn), lambda p, j: (0, 0)),
                      pl.BlockSpec((2 * seq, nn), lambda p, j: (0, 0))],
            out_specs=pl.BlockSpec((2, seq, ct), lambda p, j: (p, 0, j)),
            compiler_params=_params(2), name="hyena_short_conv",
        )(u4, zs4, taps, skip[o].reshape(1, d), ff, tf, fi)
        u4, u_which = out[None], 0
    return out.reshape(n, d)


def kernel(x, c, ctx, c_ctx, w_mod, b_mod, ln_g, ln_b, hy_w_in, hy_b_in, hy_conv_w, hy_conv_b, hy_f_w1, hy_f_b1, hy_f_w2, hy_f_b2, hy_f_w3, hy_f_b3, hy_f_w4, hy_skip, hy_w_out, hy_b_out, gla_w_in, gla_w_gate, gla_b_gate, gla_norm_g, gla_w_out, gm_w_in, gm_b_in, gm_ln_g, gm_ln_b, gm_ws, gm_bs, gm_w_out, gm_b_out, router_w, router_b, moe_w1, moe_w3, moe_w2):
    B, L, D = x.shape
    Lc = ctx.shape[1]
    depth = w_mod.shape[0]
    E = router_w.shape[1]
    alpha = (2.0 * depth) ** 0.25
    gla_layers = list(range(1, depth, N_MIXERS))
    last_ctx = gla_layers[-1] if gla_layers else -1

    cvec = jnp.zeros((8, D), F32).at[:B].set(c).at[B].set(c_ctx)
    mod = _modulation(cvec, w_mod, b_mod)
    h = _add_pos(x.reshape(B * L, D), L)
    hc = ctx.reshape(B * Lc, D)
    rw_t = router_w.T
    rw_hi = rw_t.astype(BF16)
    rhl = jnp.concatenate([rw_hi, (rw_t - rw_hi.astype(F32)).astype(BF16)], axis=0)
    rb_b = jnp.broadcast_to(router_b[:, None], (E, ROW_TILE))
    zero_cnt = jnp.zeros((E, LANES), F32)
    row2 = lambda v: v.reshape(1, -1)

    for i in range(depth):
        kind, j = i % N_MIXERS, i // N_MIXERS
        ctx_full = i < last_ctx
        ctx_any = i <= last_ctx
        lat = [mod[i, :B, k * D:(k + 1) * D].reshape(B, 1, D) for k in range(6)]
        cm = [jnp.broadcast_to(mod[i, B, k * D:(k + 1) * D].reshape(1, 1, D), (B, 1, D)) for k in range(6)]
        streams_in = [(h, lat, L)] + ([(hc, cm, Lc)] if ctx_any else [])
        pre = []
        if kind == 0:
            w_in = hy_w_in[j].astype(BF16)
            for s_h, s_m, s_len in streams_in[:1 + int(ctx_full)]:
                zs = _hy_in(s_h, s_m[1], s_m[0], w_in, row2(hy_b_in[j]), hy_conv_w[j], row2(hy_conv_b[j]), s_len)
                taps = _hy_taps(s_len, D, hy_f_w1[j], hy_f_b1[j], hy_f_w2[j], hy_f_b2[j], hy_f_w3[j], hy_f_b3[j],
                                hy_f_w4[j])
                if s_len == L:
                    y2 = _hy_long_conv(zs, taps, hy_skip[j], B, s_len)
                else:
                    y2 = _hy_short_conv(zs, taps, hy_skip[j], B, s_len)
                pre.append(("hyena", (y2,), lambda tm, d: [pl.BlockSpec((tm, d), lambda t: (t, 0))]))
            w_out, b_out = hy_w_out[j].astype(BF16), row2(hy_b_out[j])
        elif kind == 1:
            w_main, w_lr, w_g, b_g = _gla_weights(gla_w_in[j], gla_w_gate[j], gla_b_gate[j], D)
            proj = [_gla_in(s_h, s_m[1], s_m[0], w_main, w_lr, w_g, b_g, s_len) for s_h, s_m, s_len in streams_in]
            dvh = D // GLA_HEADS
            dkh = (D // 2) // GLA_HEADS
            zero_state = jnp.zeros((B, GLA_HEADS, dvh, dkh), F32)
            if ctx_any:
                ocf, ocb, s_f, s_b = _gla_scan(*proj[1], zero_state, zero_state, B, Lc)
            else:
                s_f = s_b = zero_state
            o_f, o_b, _, _ = _gla_scan(*proj[0], s_f, s_b, B, L)
            gla_specs = lambda tm, d: [pl.BlockSpec((tm, d), lambda t: (t, 0))] * 3 + [pl.BlockSpec((1, d), lambda t: (0, 0))]
            pre.append(("gla", (o_f, o_b, proj[0][2], row2(gla_norm_g[j])), gla_specs))
            if ctx_full:
                pre.append(("gla", (ocf, ocb, proj[1][2], row2(gla_norm_g[j])), gla_specs))
            w_out, b_out = gla_w_out[j].astype(BF16), jnp.zeros((1, D), F32)
        else:
            w_in = gm_w_in[j].astype(BF16)
            ws = gm_ws[j].astype(BF16)
            bs_exp = jnp.repeat(gm_bs[j].T, D // GM_HEADS, axis=1)
            gm_specs = lambda tm, d: [pl.BlockSpec((tm, d), lambda t: (t, 0))] * 2 + [
                pl.BlockSpec((GM_HEADS, GM_CHUNK, GM_CHUNK), lambda t: (0, 0, 0)), pl.BlockSpec((GM_CHUNK, d), lambda t: (0, 0))]
            for s_h, s_m, s_len in streams_in[:1 + int(ctx_full)]:
                u, vn = _gm_in(s_h, s_m[1], s_m[0], w_in, row2(gm_b_in[j]), row2(gm_ln_g[j]), row2(gm_ln_b[j]), s_len)
                pre.append(("gmlp", (u, vn, ws, bs_exp), gm_specs))
            w_out, b_out = gm_w_out[j].astype(BF16), row2(gm_b_out[j])

        moe_streams = []
        cnt = zero_cnt
        for (s_h, s_m, s_len), (pk, pargs, pspecs) in zip(streams_in, pre):
            h1, tok, meta, cnt = _post(pk, pargs, pspecs, s_h, w_out, b_out, s_m[2], row2(ln_g[i, 0]), row2(ln_b[i, 0]),
                                       s_m[4], s_m[3], rhl, rb_b, cnt, s_len, alpha)
            moe_streams.append(dict(h1=h1, tok=tok, meta=meta, g2=s_m[5], lg=row2(ln_g[i, 1]), lb=row2(ln_b[i, 1]),
                                    seq=s_len))
        outs = _moe(moe_streams, cnt, moe_w1, moe_w3, moe_w2, i, alpha)
        h = outs[0]
        if ctx_full:
            hc = outs[1]
    return h.reshape(B, L, D)
```

```python
import functools
import math

import jax
import jax.numpy as jnp
from jax import lax
from jax.experimental import pallas as pl
from jax.experimental.pallas import tpu as pltpu

F32 = jnp.float32
BF16 = jnp.bfloat16
I32 = jnp.int32
U32 = jnp.uint32
HIGHEST = lax.Precision.HIGHEST

GRID_W = 64
N_MIXERS = 3
LN_EPS = 1e-5
HY_ORDER = 2
HY_DIRS = 2
HY_DECAY_TARGET = 1e-2
HY_DECAY_SHORT_PCT = 0.3
HY_DECAY_LONG_PCT = 1.5
GLA_HEADS = 4
GLA_RANK = 16
GLA_TAU = 16.0
GLA_CHUNK = 64
GM_CHUNK = 128
GM_HEADS = 4
N_GROUPS = 4
TOP_K = 2

LANES = 128
SUBLANES = 8
V7X_VMEM_LIMIT_BYTES = 56 * 1024 * 1024
MOE_ROWS = 256
ROW_TILE = 512
DMA_TILE = 256
DMA_UNROLL = 8
FFT_N2 = 128
FFT_GROUP = SUBLANES


def _params(n_grid):
    return pltpu.CompilerParams(dimension_semantics=("arbitrary",) * n_grid,
                                vmem_limit_bytes=V7X_VMEM_LIMIT_BYTES)


def _dot(a, b):
    return jnp.dot(a, b, preferred_element_type=F32)


def _dot_nt(a, b):
    return lax.dot_general(a, b, (((1,), (1,)), ((), ())), preferred_element_type=F32)


def _dot_tn(a, b):
    return lax.dot_general(a, b, (((0,), (0,)), ((), ())), preferred_element_type=F32)


def _rows_to_tiles(x, ref):
    for s in range(ref.shape[-2]):
        ref[:, s, :] = x[:, s * LANES:(s + 1) * LANES]


def _tiles_to_rows(ref):
    return jnp.concatenate([ref[:, s, :] for s in range(ref.shape[-2])], axis=1)


def _ln(x, g, b):
    mu = jnp.mean(x, -1, keepdims=True)
    xc = x - mu
    var = jnp.mean(xc * xc, -1, keepdims=True)
    return xc * lax.rsqrt(var + LN_EPS) * g + b


def _silu(x):
    return x * jax.nn.sigmoid(x)


def _gelu_tanh(x):
    return 0.5 * x * (1.0 + jnp.tanh(math.sqrt(2.0 / math.pi) * (x + 0.044715 * (x * x * x))))


def _mod_kernel(c_ref, w_ref, b_ref, o_ref):
    s = _silu(c_ref[...])
    o_ref[0] = _dot(s.astype(BF16), w_ref[0].astype(BF16)) + b_ref[0]


def _modulation(cvec, w_mod, b_mod):
    depth, d, n = w_mod.shape
    tn = n // 4
    return pl.pallas_call(
        _mod_kernel,
        out_shape=jax.ShapeDtypeStruct((depth, 8, n), F32),
        grid=(depth, n // tn),
        in_specs=[pl.BlockSpec((8, d), lambda i, j: (0, 0)),
                  pl.BlockSpec((1, d, tn), lambda i, j: (i, 0, j)),
                  pl.BlockSpec((1, 1, tn), lambda i, j: (i, 0, j))],
        out_specs=pl.BlockSpec((1, 8, tn), lambda i, j: (i, 0, j)),
        compiler_params=_params(2), name="modulation",
    )(cvec, w_mod, b_mod.reshape(depth, 1, n))


def _pos_table_kernel(o_ref, *, q):
    rows, cols = o_ref.shape
    p = lax.broadcasted_iota(I32, (rows, cols), 0).astype(F32)
    lane = lax.broadcasted_iota(I32, (rows, cols), 1)
    j = jnp.where(lane >= q, lane - q, lane).astype(F32)
    omega = jnp.exp(j * (-math.log(10000.0) / q))
    ang = p * omega
    o_ref[...] = jnp.where(lane >= q, jnp.cos(ang), jnp.sin(ang))


def _pos_table(n, d):
    q = d // 4
    return pl.pallas_call(functools.partial(_pos_table_kernel, q=q),
                          out_shape=jax.ShapeDtypeStruct((n, 2 * q), F32), name="pos_table")()


def _add_pos_kernel(x_ref, er_ref, ec_ref, o_ref, *, half):
    tm = x_ref.shape[0]
    reps = tm // GRID_W
    er = er_ref[...]
    er_rows = jnp.broadcast_to(er[:, None, :], (reps, GRID_W, half)).reshape(tm, half)
    ec_rows = jnp.broadcast_to(ec_ref[...][None], (reps, GRID_W, half)).reshape(tm, half)
    o_ref[:, :half] = x_ref[:, :half] + er_rows
    o_ref[:, half:] = x_ref[:, half:] + ec_rows


def _add_pos(x2d, seq):
    n, d = x2d.shape
    half = d // 2
    rows = seq // GRID_W
    er = _pos_table(rows, d)
    ec = _pos_table(GRID_W, d)
    tm = ROW_TILE
    reps = tm // GRID_W
    tps = seq // tm
    return pl.pallas_call(
        functools.partial(_add_pos_kernel, half=half),
        out_shape=jax.ShapeDtypeStruct((n, d), F32),
        grid=(n // tm,),
        in_specs=[pl.BlockSpec((tm, d), lambda i: (i, 0)),
                  pl.BlockSpec((reps, half), lambda i: (i % tps, 0)),
                  pl.BlockSpec((GRID_W, half), lambda i: (0, 0))],
        out_specs=pl.BlockSpec((tm, d), lambda i: (i, 0)),
        compiler_params=_params(1), name="add_pos",
    )(x2d, er, ec)


def _gm_in_kernel(h_ref, sc_ref, sh_ref, w_ref, b_ref, g_ref, bb_ref, u_ref, v_ref):
    d = h_ref.shape[1]
    a = (h_ref[...] * (1.0 + sc_ref[0]) + sh_ref[0]).astype(BF16)
    u_ref[...] = _gelu_tanh(_dot(a, w_ref[:, :d]) + b_ref[:, :d])
    v = _gelu_tanh(_dot(a, w_ref[:, d:]) + b_ref[:, d:])
    v_ref[...] = _ln(v, g_ref[...], bb_ref[...]).astype(BF16)


def _gm_in(h, sc, sh, w, b, g, bb, seq):
    n, d = h.shape
    tm = min(ROW_TILE, seq)
    tps = seq // tm
    row = lambda i: (i, 0)
    per_b = lambda i: (i // tps, 0, 0)
    fixed = lambda i: (0, 0)
    return pl.pallas_call(
        _gm_in_kernel,
        out_shape=(jax.ShapeDtypeStruct((n, d), F32), jax.ShapeDtypeStruct((n, d), BF16)),
        grid=(n // tm,),
        in_specs=[pl.BlockSpec((tm, d), row), pl.BlockSpec((1, 1, d), per_b), pl.BlockSpec((1, 1, d), per_b),
                  pl.BlockSpec((d, 2 * d), fixed), pl.BlockSpec((1, 2 * d), fixed),
                  pl.BlockSpec((1, d), fixed), pl.BlockSpec((1, d), fixed)],
        out_specs=(pl.BlockSpec((tm, d), row), pl.BlockSpec((tm, d), row)),
        compiler_params=_params(1), name="gmlp_in",
    )(h, sc, sh, w, b, g, bb)


def _gm_prologue(u_ref, v_ref, ws_ref, bs_ref):
    tm, d = u_ref.shape
    dh = d // GM_HEADS
    for c in range(tm // GM_CHUNK):
        rows = slice(c * GM_CHUNK, (c + 1) * GM_CHUNK)
        parts = [_dot(ws_ref[g], v_ref[rows, g * dh:(g + 1) * dh]) for g in range(GM_HEADS)]
        vm = jnp.concatenate(parts, axis=1) + bs_ref[...]
        yield rows, (u_ref[rows, :] * vm).astype(BF16)


def _gla_prologue(of_ref, ob_ref, r_ref, ng_ref):
    tm, d = of_ref.shape
    dh = d // GLA_HEADS
    o = of_ref[...] + ob_ref[...]
    parts = []
    for hd in range(GLA_HEADS):
        oh = o[:, hd * dh:(hd + 1) * dh]
        mu = jnp.mean(oh, -1, keepdims=True)
        oc = oh - mu
        var = jnp.mean(oc * oc, -1, keepdims=True)
        parts.append(oc * lax.rsqrt(var + LN_EPS) * ng_ref[:, hd * dh:(hd + 1) * dh])
    y = jnp.concatenate(parts, axis=1) * _silu(r_ref[...])
    yield slice(0, tm), y.astype(BF16)


def _hy_prologue(y_ref):
    yield slice(0, y_ref.shape[0]), y_ref[...].astype(BF16)


_PROLOGUES = {"gmlp": (_gm_prologue, 4), "gla": (_gla_prologue, 4), "hyena": (_hy_prologue, 1)}


def _route_t(scores, biased):
    n_experts, tm = scores.shape
    gsz = n_experts // N_GROUPS
    neg = jnp.float32(-jnp.inf)
    v3 = biased.reshape(N_GROUPS, gsz, tm)
    sub = lax.broadcasted_iota(I32, v3.shape, 1).astype(F32)
    m1 = jnp.max(v3, axis=1, keepdims=True)
    i1 = jnp.min(jnp.where(v3 == m1, sub, float(gsz)), axis=1, keepdims=True)
    v3b = jnp.where(sub == i1, neg, v3)
    m2 = jnp.max(v3b, axis=1, keepdims=True)
    i2 = jnp.min(jnp.where(v3b == m2, sub, float(gsz)), axis=1, keepdims=True)
    gscore = (m1 + m2).reshape(N_GROUPS, tm)
    i1 = i1.reshape(N_GROUPS, tm)
    i2 = i2.reshape(N_GROUPS, tm)
    best, e0, e1 = gscore[0:1], i1[0:1], i2[0:1]
    for g in range(1, N_GROUPS):
        better = gscore[g:g + 1] > best
        best = jnp.where(better, gscore[g:g + 1], best)
        e0 = jnp.where(better, i1[g:g + 1] + float(g * gsz), e0)
        e1 = jnp.where(better, i2[g:g + 1] + float(g * gsz), e1)
    row = lax.broadcasted_iota(I32, scores.shape, 0).astype(F32)
    oh0 = (row == e0).astype(F32)
    oh1 = (row == e1).astype(F32)
    w0 = jnp.sum(oh0 * scores, axis=0, keepdims=True)
    w1 = jnp.sum(oh1 * scores, axis=0, keepdims=True)
    den = w0 + w1
    return e0, e1, oh0, oh1, w0 / den, w1 / den


def _post_kernel(*refs, kind, alpha):
    prologue, n_pro = _PROLOGUES[kind]
    pro = refs[:n_pro]
    (h_ref, w_ref, b_ref, g1_ref, lg_ref, lb_ref, sc_ref, sh_ref, rhl_ref, rb_ref, cin_ref,
     h1_ref, tok_ref, meta_ref, cout_ref, cnt_scr) = refs[n_pro:]
    i = pl.program_id(0)
    tm = h_ref.shape[0]
    n_experts = rb_ref.shape[0]

    @pl.when(i == 0)
    def _():
        cnt_scr[...] = cin_ref[...]

    for rows, y in prologue(*pro):
        out = _dot(y, w_ref[...]) + b_ref[...]
        h1_ref[rows, :] = _ln(alpha * h_ref[rows, :] + g1_ref[0] * out, lg_ref[...], lb_ref[...])
    tok = h1_ref[...] * (1.0 + sc_ref[0]) + sh_ref[0]
    _rows_to_tiles(tok, tok_ref)
    t_hi = tok.astype(BF16)
    t_lo = (tok - t_hi.astype(F32)).astype(BF16)
    p_hi = _dot_nt(rhl_ref[...], t_hi)
    logits = p_hi[:n_experts] + p_hi[n_experts:] + _dot_nt(rhl_ref[0:n_experts, :], t_lo)
    scores = jax.nn.sigmoid(logits)
    e0, e1, oh0, oh1, w0, w1 = _route_t(scores, scores + rb_ref[...])
    r_i = lax.broadcasted_iota(I32, (tm, tm), 0)
    c_i = lax.broadcasted_iota(I32, (tm, tm), 1)
    upper = (r_i < c_i).astype(BF16)
    oh = oh0 + oh1
    carry = cnt_scr[...]
    before = _dot(oh.astype(BF16), upper) + jnp.concatenate([carry] * (tm // LANES), axis=1)
    rank0 = jnp.sum(oh0 * before, axis=0, keepdims=True)
    rank1 = jnp.sum(oh1 * before, axis=0, keepdims=True)
    cnt_scr[...] = carry + jnp.sum(oh, axis=1, keepdims=True)
    zero = jnp.zeros_like(w0)
    meta_ref[0] = jnp.concatenate([e0, e1, rank0, rank1, w0, w1, zero, zero], axis=0)
    cout_ref[...] = cnt_scr[...]


def _post(kind, pro_args, pro_specs, h, w, b, g1, lg, lb, sc, sh, rhl, rb, cnt_in, seq, alpha):
    n, d = h.shape
    tm = min(ROW_TILE, seq)
    tps = seq // tm
    n_experts = rb.shape[0]
    row = lambda i: (i, 0)
    per_b = lambda i: (i // tps, 0, 0)
    fixed = lambda i: (0, 0)
    in_specs = list(pro_specs(tm, d)) + [
        pl.BlockSpec((tm, d), row), pl.BlockSpec((d, d), fixed), pl.BlockSpec((1, d), fixed),
        pl.BlockSpec((1, 1, d), per_b), pl.BlockSpec((1, d), fixed), pl.BlockSpec((1, d), fixed),
        pl.BlockSpec((1, 1, d), per_b), pl.BlockSpec((1, 1, d), per_b),
        pl.BlockSpec((2 * n_experts, d), fixed), pl.BlockSpec((n_experts, tm), fixed),
        pl.BlockSpec((n_experts, LANES), fixed)]
    return pl.pallas_call(
        functools.partial(_post_kernel, kind=kind, alpha=alpha),
        out_shape=(jax.ShapeDtypeStruct((n, d), F32), jax.ShapeDtypeStruct((n, d // LANES, LANES), F32),
                   jax.ShapeDtypeStruct((n // tm, SUBLANES, tm), F32),
                   jax.ShapeDtypeStruct((n_experts, LANES), F32)),
        grid=(n // tm,),
        in_specs=in_specs,
        out_specs=(pl.BlockSpec((tm, d), row), pl.BlockSpec((tm, d // LANES, LANES), lambda i: (i, 0, 0)),
                   pl.BlockSpec((1, SUBLANES, tm), lambda i: (i, 0, 0)), pl.BlockSpec((n_experts, LANES), fixed)),
        scratch_shapes=[pltpu.VMEM((n_experts, LANES), F32)],
        compiler_params=_params(1), name="post_" + kind,
    )(*pro_args, h, w, b, g1, lg, lb, sc, sh, rhl, rb[:, :tm], cnt_in)


def _dest_kernel(meta_ref, pstart_ref, o_ref):
    meta = meta_ref[0]
    n_experts = pstart_ref.shape[0]
    tm = meta.shape[1]
    row = lax.broadcasted_iota(I32, (n_experts, tm), 0).astype(F32)
    rows = []
    for k in range(TOP_K):
        off = jnp.sum(jnp.where(row == meta[k:k + 1], pstart_ref[...], 0.0), axis=0, keepdims=True)
        rows.append(off + meta[TOP_K + k:TOP_K + k + 1])
    rows += [jnp.zeros_like(rows[0])] * (SUBLANES - TOP_K)
    o_ref[0] = jnp.concatenate(rows, axis=0).astype(I32)


def _dest(meta, pstart_b):
    ntile, _, tmeta = meta.shape
    tm = DMA_TILE
    per = tmeta // tm
    n_experts = pstart_b.shape[0]
    return pl.pallas_call(
        _dest_kernel,
        out_shape=jax.ShapeDtypeStruct((ntile * per, SUBLANES, tm), I32),
        grid=(ntile * per,),
        in_specs=[pl.BlockSpec((1, SUBLANES, tm), lambda i: (i // per, 0, i % per)),
                  pl.BlockSpec((n_experts, tm), lambda i: (0, 0))],
        out_specs=pl.BlockSpec((1, SUBLANES, tm), lambda i: (i, 0, 0)),
        compiler_params=_params(1), name="moe_dest",
    )(meta, pstart_b[:, :tm])


def _row_loop(tm, body):
    def step(it, c):
        for u in range(DMA_UNROLL):
            body(it * DMA_UNROLL + u, u)
        return c
    lax.fori_loop(0, tm // DMA_UNROLL, step, 0)


def _dispatch_kernel(lb_ref, dest_ref, tok_ref, *rest, fill):
    if fill:
        xs_ref, idx_scr, tok_scr, zero_scr, isem, lsem, rsem, zsem = rest
    else:
        _, xs_ref, idx_scr, tok_scr, zero_scr, isem, lsem, rsem, zsem = rest
    i = pl.program_id(0)
    nstep = pl.num_programs(0)
    tm = idx_scr.shape[2]
    slot = i % 2

    def load(step, to):
        return pltpu.make_async_copy(tok_ref.at[pl.ds(step * tm, tm)], tok_scr.at[to], lsem.at[to])

    @pl.when(i == 0)
    def _():
        load(0, 0).start()

    if fill:
        @pl.when(i == 0)
        def _():
            zero_scr[...] = jnp.zeros_like(zero_scr)

            def fill_copy(e):
                start_row = pl.multiple_of(jnp.maximum(lb_ref[e], 0), MOE_ROWS)
                return pltpu.make_async_copy(zero_scr, xs_ref.at[pl.ds(start_row, MOE_ROWS)], zsem)

            def start(e, c):
                @pl.when(lb_ref[e] >= 0)
                def _():
                    fill_copy(e).start()
                return c

            def wait(e, c):
                @pl.when(lb_ref[e] >= 0)
                def _():
                    fill_copy(e).wait()
                return c

            lax.fori_loop(0, lb_ref.shape[0], start, 0)
            lax.fori_loop(0, lb_ref.shape[0], wait, 0)

    for k in range(TOP_K):
        pltpu.make_async_copy(dest_ref.at[i, k], idx_scr.at[slot, k], isem).start()
    for k in range(TOP_K):
        pltpu.make_async_copy(dest_ref.at[i, k], idx_scr.at[slot, k], isem).wait()

    def row_copy(s, r, k):
        return pltpu.make_async_copy(tok_scr.at[s, r], xs_ref.at[idx_scr[s, k, r]], rsem.at[s])

    def start_row(r, u):
        for k in range(TOP_K):
            row_copy(slot, r, k).start(priority=(u + k) % 2)

    def wait_rows(s):
        def wait_row(r, u):
            for k in range(TOP_K):
                row_copy(s, r, k).wait()
        _row_loop(tm, wait_row)

    load(i, slot).wait()
    _row_loop(tm, start_row)

    @pl.when(i > 0)
    def _():
        wait_rows(1 - slot)

    @pl.when(i + 1 < nstep)
    def _():
        load(i + 1, 1 - slot).start()

    @pl.when(i == nstep - 1)
    def _():
        wait_rows(slot)


def _dispatch(last_blk, dest, tok, xs, p):
    n, ts, _ = tok.shape
    tm = DMA_TILE
    fill = xs is None
    any_spec = pl.BlockSpec(memory_space=pl.ANY)
    args = [last_blk, dest, tok] + ([] if fill else [xs])
    return pl.pallas_call(
        functools.partial(_dispatch_kernel, fill=fill),
        out_shape=jax.ShapeDtypeStruct((p, ts, LANES), F32),
        grid_spec=pltpu.PrefetchScalarGridSpec(
            num_scalar_prefetch=1, grid=(n // tm,), in_specs=[any_spec] * (len(args) - 1), out_specs=any_spec,
            scratch_shapes=[pltpu.SMEM((2, TOP_K, tm), I32), pltpu.VMEM((2, tm, ts, LANES), F32),
                            pltpu.VMEM((MOE_ROWS, ts, LANES), F32), pltpu.SemaphoreType.DMA,
                            pltpu.SemaphoreType.DMA((2,)), pltpu.SemaphoreType.DMA((2,)), pltpu.SemaphoreType.DMA]),
        input_output_aliases={} if fill else {3: 0},
        compiler_params=_params(1), name="moe_dispatch",
    )(*args)


def _expert_kernel(be_ref, nu_ref, x_ref, w1_ref, w3_ref, w2_ref, y_ref, w1_scr, w3_scr, w2_scr):
    j = pl.program_id(0)
    e = be_ref[j]
    e_prev = be_ref[jnp.maximum(j - 1, 0)]

    @pl.when(j < nu_ref[0])
    def _():
        @pl.when((j == 0) | (e != e_prev))
        def _():
            w1_scr[...] = w1_ref[0, 0].astype(BF16)
            w3_scr[...] = w3_ref[0, 0].astype(BF16)
            w2_scr[...] = w2_ref[0, 0].astype(BF16)

        x = _tiles_to_rows(x_ref).astype(BF16)
        hid = _silu(_dot(x, w1_scr[...])) * _dot(x, w3_scr[...])
        _rows_to_tiles(_dot(hid.astype(BF16), w2_scr[...]), y_ref)

    @pl.when(j >= nu_ref[0])
    def _():
        y_ref[...] = jnp.zeros_like(y_ref)


def _experts(blk_e, n_used, xs, w1, w3, w2, layer):
    p, ts, _ = xs.shape
    d = ts * LANES
    de = w1.shape[-1]
    nblk = p // MOE_ROWS
    wmap = lambda j, be, nu: (layer, be[j], 0, 0)
    xmap = lambda j, be, nu: (jnp.minimum(j, nu[0] - 1), 0, 0)
    return pl.pallas_call(
        _expert_kernel,
        out_shape=jax.ShapeDtypeStruct((p, ts, LANES), F32),
        grid_spec=pltpu.PrefetchScalarGridSpec(
            num_scalar_prefetch=2, grid=(nblk,),
            in_specs=[pl.BlockSpec((MOE_ROWS, ts, LANES), xmap),
                      pl.BlockSpec((1, 1, d, de), wmap), pl.BlockSpec((1, 1, d, de), wmap),
                      pl.BlockSpec((1, 1, de, d), wmap)],
            out_specs=pl.BlockSpec((MOE_ROWS, ts, LANES), lambda j, be, nu: (j, 0, 0)),
            scratch_shapes=[pltpu.VMEM((d, de), BF16), pltpu.VMEM((d, de), BF16), pltpu.VMEM((de, d), BF16)]),
        compiler_params=_params(1), name="moe_experts",
    )(blk_e, n_used, xs, w1, w3, w2)


def _combine_kernel(dest_ref, ys_ref, meta_ref, h_ref, g2_ref, lg_ref, lb_ref, o_ref,
                    idx_scr, buf_scr, isem, rsem, *, alpha):
    i = pl.program_id(0)
    nstep = pl.num_programs(0)
    tm = h_ref.shape[0]

    def gather(step, slot):
        for k in range(TOP_K):
            pltpu.make_async_copy(dest_ref.at[step, k], idx_scr.at[slot, k], isem).start()
        for k in range(TOP_K):
            pltpu.make_async_copy(dest_ref.at[step, k], idx_scr.at[slot, k], isem).wait()

        def start_row(r, u):
            for k in range(TOP_K):
                pltpu.make_async_copy(ys_ref.at[idx_scr[slot, k, r]], buf_scr.at[slot, k, r],
                                      rsem.at[slot]).start(priority=(u + k) % 2)

        _row_loop(tm, start_row)

    slot = i % 2

    @pl.when(i == 0)
    def _():
        gather(0, 0)

    @pl.when(i + 1 < nstep)
    def _():
        gather(i + 1, 1 - slot)

    def wait_row(r, u):
        for k in range(TOP_K):
            pltpu.make_async_copy(ys_ref.at[idx_scr[slot, k, r]], buf_scr.at[slot, k, r], rsem.at[slot]).wait()

    _row_loop(tm, wait_row)
    meta = meta_ref[0]
    wt = jnp.concatenate([meta, jnp.zeros((LANES - SUBLANES, tm), F32)], axis=0).T
    f = wt[:, 2 * TOP_K:2 * TOP_K + 1] * _tiles_to_rows(buf_scr.at[slot, 0])
    for k in range(1, TOP_K):
        f = f + wt[:, 2 * TOP_K + k:2 * TOP_K + k + 1] * _tiles_to_rows(buf_scr.at[slot, k])
    o_ref[...] = _ln(alpha * h_ref[...] + g2_ref[0] * f, lg_ref[...], lb_ref[...])


def _combine(dest, ys, meta, h, g2, lg, lb, seq, alpha):
    n, d = h.shape
    tm = DMA_TILE
    tps = seq // tm
    per = meta.shape[2] // tm
    row = lambda i: (i, 0)
    fixed = lambda i: (0, 0)
    return pl.pallas_call(
        functools.partial(_combine_kernel, alpha=alpha),
        out_shape=jax.ShapeDtypeStruct((n, d), F32),
        grid=(n // tm,),
        in_specs=[pl.BlockSpec(memory_space=pl.ANY), pl.BlockSpec(memory_space=pl.ANY),
                  pl.BlockSpec((1, SUBLANES, tm), lambda i: (i // per, 0, i % per)), pl.BlockSpec((tm, d), row),
                  pl.BlockSpec((1, 1, d), lambda i: (i // tps, 0, 0)),
                  pl.BlockSpec((1, d), fixed), pl.BlockSpec((1, d), fixed)],
        out_specs=pl.BlockSpec((tm, d), row),
        scratch_shapes=[pltpu.SMEM((2, TOP_K, tm), I32), pltpu.VMEM((2, TOP_K, tm, d // LANES, LANES), F32),
                        pltpu.SemaphoreType.DMA, pltpu.SemaphoreType.DMA((2,))],
        compiler_params=_params(1), name="moe_combine",
    )(dest, ys, meta, h, g2, lg, lb)


def _moe(streams, counts, w1, w3, w2, layer, alpha):
    n_experts = w1.shape[1]
    n_assign = TOP_K * sum(s["tok"].shape[0] for s in streams)
    p = n_assign + n_experts * MOE_ROWS
    nblk = p // MOE_ROWS
    cnt = counts[:, 0].astype(I32)
    padded = (cnt + MOE_ROWS - 1) // MOE_ROWS * MOE_ROWS
    pend = jnp.cumsum(padded)
    pstart = pend - padded
    n_used = (pend[-1] // MOE_ROWS).astype(I32).reshape(1)
    blk = jnp.arange(nblk, dtype=I32) * MOE_ROWS
    blk_e = jnp.sum((pend[None, :] <= blk[:, None]).astype(I32), axis=1)
    last_e = jnp.sum((pend <= jnp.maximum(pend[-1] - 1, 0)).astype(I32))
    blk_e = jnp.minimum(blk_e, last_e).astype(I32)
    last_blk = jnp.where(padded > 0, pend - MOE_ROWS, -1).astype(I32)
    pstart_b = jnp.broadcast_to(pstart.astype(F32)[:, None], (n_experts, ROW_TILE))
    dests = [_dest(s["meta"], pstart_b) for s in streams]
    xs = None
    for s, dst in zip(streams, dests):
        xs = _dispatch(last_blk, dst, s["tok"], xs, p)
    ys = _experts(blk_e, n_used, xs, w1, w3, w2, layer)
    return [_combine(dst, ys, s["meta"], s["h1"], s["g2"], s["lg"], s["lb"], s["seq"], alpha)
            for s, dst in zip(streams, dests)]


def _log_sigmoid(x):
    return jnp.minimum(x, 0.0) - jnp.log(1.0 + jnp.exp(-jnp.abs(x)))


def _gla_in_kernel(h_ref, sc_ref, sh_ref, w_ref, wl_ref, wg_ref, bg_ref, qk_ref, v_ref, r_ref, g_ref, *, qscale):
    tm, d = h_ref.shape
    a = (h_ref[...] * (1.0 + sc_ref[0]) + sh_ref[0]).astype(BF16)
    qk = _dot(a, w_ref[:, :d])
    lane = lax.broadcasted_iota(I32, (tm, d), 1)
    qk_ref[...] = jnp.where(lane < d // 2, qk * qscale, qk)
    v_ref[...] = _dot(a, w_ref[:, d:2 * d]).astype(BF16)
    r_ref[...] = _dot(a, w_ref[:, 2 * d:])
    lr = _dot(a, wl_ref[...])
    gpre = _dot(lr.astype(BF16), wg_ref[...]) + bg_ref[...]
    g_ref[...] = _log_sigmoid(gpre) * (1.0 / GLA_TAU)


def _gla_weights(w_in, w_gate, b_gate, d):
    dk = d // 2
    n_lr = HY_DIRS * GLA_RANK
    w_main = w_in[:, :2 * dk + 2 * d].astype(BF16)
    w_lr = jnp.zeros((d, LANES), F32).at[:, :n_lr].set(w_in[:, 2 * dk + 2 * d:]).astype(BF16)
    w_g = jnp.zeros((LANES, HY_DIRS * dk), F32)
    for dr in range(HY_DIRS):
        w_g = w_g.at[dr * GLA_RANK:(dr + 1) * GLA_RANK, dr * dk:(dr + 1) * dk].set(w_gate[dr])
    return w_main, w_lr, w_g.astype(BF16), b_gate.reshape(1, HY_DIRS * dk)


def _gla_in(h, sc, sh, w_main, w_lr, w_g, b_g, seq):
    n, d = h.shape
    tm = min(ROW_TILE, seq)
    tps = seq // tm
    row = lambda i: (i, 0)
    per_b = lambda i: (i // tps, 0, 0)
    fixed = lambda i: (0, 0)
    qscale = float(((d // 2) // GLA_HEADS) ** -0.5)
    return pl.pallas_call(
        functools.partial(_gla_in_kernel, qscale=qscale),
        out_shape=(jax.ShapeDtypeStruct((n, d), F32), jax.ShapeDtypeStruct((n, d), BF16),
                   jax.ShapeDtypeStruct((n, d), F32), jax.ShapeDtypeStruct((n, d), F32)),
        grid=(n // tm,),
        in_specs=[pl.BlockSpec((tm, d), row), pl.BlockSpec((1, 1, d), per_b), pl.BlockSpec((1, 1, d), per_b),
                  pl.BlockSpec((d, 3 * d), fixed), pl.BlockSpec((d, LANES), fixed),
                  pl.BlockSpec((LANES, d), fixed), pl.BlockSpec((1, d), fixed)],
        out_specs=tuple(pl.BlockSpec((tm, d), row) for _ in range(4)),
        compiler_params=_params(1), name="gla_in",
    )(h, sc, sh, w_main, w_lr, w_g, b_g)


def _gla_direction(q_ref, k_ref, v_ref, g_ref, st_scr, o_ref, reverse):
    rt, dk = q_ref.shape
    cs = GLA_CHUNK
    nch = rt // cs
    g = g_ref[...]
    pos = lax.broadcasted_iota(I32, (rt, dk), 0) % cs
    b = g
    sh = 1
    while sh < cs:
        if reverse:
            b = b + jnp.where(pos < cs - sh, pltpu.roll(b, rt - sh, 0), 0.0)
        else:
            b = b + jnp.where(pos >= sh, pltpu.roll(b, sh, 0), 0.0)
        sh *= 2
    edge = 0 if reverse else cs - 1
    b3 = b.reshape(nch, cs, dk)
    b_edge = b3[:, edge:edge + 1, :]
    q = q_ref[...]
    k = k_ref[...]
    qe = (q * jnp.exp(b)).astype(BF16)
    ke = (k * jnp.exp(-b)).astype(BF16)
    kd = (k.reshape(nch, cs, dk) * jnp.exp(b_edge - b3)).astype(BF16)
    decay = jnp.exp(b_edge)
    r_i = lax.broadcasted_iota(I32, (rt, rt), 0)
    c_i = lax.broadcasted_iota(I32, (rt, rt), 1)
    same = (r_i // cs) == (c_i // cs)
    tri = (c_i >= r_i) if reverse else (c_i <= r_i)
    att = jnp.where(same & tri, _dot_nt(qe, ke), 0.0).astype(BF16)
    v = v_ref[...]
    intra = _dot(att, v)
    st = st_scr[...]
    order = range(nch - 1, -1, -1) if reverse else range(nch)
    for j in order:
        rows = slice(j * cs, (j + 1) * cs)
        o_ref[rows, :] = intra[rows, :] + _dot_nt(qe[rows, :], st.astype(BF16))
        st = st * decay[j] + _dot_tn(v[rows, :], kd[j])
    st_scr[...] = st


def _gla_scan_kernel(qf, kf, vf, gf, qb, kb, vb, gb, s0f, s0b, of, ob, sfo, sbo, sf_scr, sb_scr):
    c = pl.program_id(2)

    @pl.when(c == 0)
    def _():
        sf_scr[...] = s0f[0, 0]
        sb_scr[...] = s0b[0, 0]

    _gla_direction(qf, kf, vf, gf, sf_scr, of, False)
    _gla_direction(qb, kb, vb, gb, sb_scr, ob, True)
    sfo[0, 0] = sf_scr[...]
    sbo[0, 0] = sb_scr[...]


def _gla_scan(qk, v, r, g, s0f, s0b, nb, seq):
    del r
    n, d = qk.shape
    nh = GLA_HEADS
    dkh, dvh = (d // 2) // nh, d // nh
    rt = min(ROW_TILE, seq)
    npb = seq // rt
    fw = lambda col: (lambda b, hd, c: (b * npb + c, col(hd)))
    bw = lambda col: (lambda b, hd, c: (b * npb + npb - 1 - c, col(hd)))
    qcol, kcol = (lambda hd: hd), (lambda hd: nh + hd)
    st = lambda b, hd, c: (b, hd, 0, 0)
    state = jax.ShapeDtypeStruct((nb, nh, dvh, dkh), F32)
    return pl.pallas_call(
        _gla_scan_kernel,
        out_shape=(jax.ShapeDtypeStruct((n, d), F32), jax.ShapeDtypeStruct((n, d), F32), state, state),
        grid=(nb, nh, npb),
        in_specs=[pl.BlockSpec((rt, dkh), fw(qcol)), pl.BlockSpec((rt, dkh), fw(kcol)),
                  pl.BlockSpec((rt, dvh), fw(qcol)), pl.BlockSpec((rt, dkh), fw(qcol)),
                  pl.BlockSpec((rt, dkh), bw(qcol)), pl.BlockSpec((rt, dkh), bw(kcol)),
                  pl.BlockSpec((rt, dvh), bw(qcol)), pl.BlockSpec((rt, dkh), bw(kcol)),
                  pl.BlockSpec((1, 1, dvh, dkh), st), pl.BlockSpec((1, 1, dvh, dkh), st)],
        out_specs=(pl.BlockSpec((rt, dvh), fw(qcol)), pl.BlockSpec((rt, dvh), bw(qcol)),
                   pl.BlockSpec((1, 1, dvh, dkh), st), pl.BlockSpec((1, 1, dvh, dkh), st)),
        scratch_shapes=[pltpu.VMEM((dvh, dkh), F32), pltpu.VMEM((dvh, dkh), F32)],
        compiler_params=_params(3), name="gla_scan",
    )(qk, qk, v, g, qk, qk, v, g, s0f, s0b)


def _hy_in_kernel(h_ref, hp_ref, hn_ref, sc_ref, sh_ref, w_ref, b_ref, cw_ref, cb_ref, o_ref, a_scr, z_scr,
                  *, tps, halo):
    i = pl.program_id(0)
    tm, d = h_ref.shape
    sc = 1.0 + sc_ref[0]
    sh = sh_ref[0]
    a_scr[0:halo, :] = (hp_ref[...] * sc + sh).astype(BF16)
    a_scr[halo:halo + tm, :] = (h_ref[...] * sc + sh).astype(BF16)
    a_scr[halo + tm:, :] = (hn_ref[...] * sc + sh).astype(BF16)
    row = lax.broadcasted_iota(I32, (tm, 1), 0)
    first = jnp.where(i % tps == 0, 0, -1)
    last = jnp.where(i % tps == tps - 1, tm - 1, -1)
    for c in range(o_ref.shape[0]):
        cols = slice(c * d, (c + 1) * d)
        z_scr[...] = _dot(a_scr[...], w_ref[:, cols]) + b_ref[:, cols]
        zp = jnp.where(row == first, 0.0, z_scr[pl.ds(halo - 1, tm), :])
        zc = z_scr[pl.ds(halo, tm), :]
        zn = jnp.where(row == last, 0.0, z_scr[pl.ds(halo + 1, tm), :])
        o_ref[c] = cw_ref[0:1, cols] * zp + cw_ref[1:2, cols] * zc + cw_ref[2:3, cols] * zn + cb_ref[:, cols]


def _hy_in(h, sc, sh, w, b, cw, cb, seq):
    n, d = h.shape
    nsplit = w.shape[1] // d
    halo = 16
    tm = min(ROW_TILE, seq)
    tps = seq // tm
    hb = tm // halo
    row = lambda i: (i, 0)
    per_b = lambda i: (i // tps, 0, 0)
    fixed = lambda i: (0, 0)
    return pl.pallas_call(
        functools.partial(_hy_in_kernel, tps=tps, halo=halo),
        out_shape=jax.ShapeDtypeStruct((nsplit, n, d), F32),
        grid=(n // tm,),
        in_specs=[pl.BlockSpec((tm, d), row),
                  pl.BlockSpec((halo, d), lambda i: (jnp.maximum(i * hb - 1, 0), 0)),
                  pl.BlockSpec((halo, d), lambda i: (jnp.minimum((i + 1) * hb, n // halo - 1), 0)),
                  pl.BlockSpec((1, 1, d), per_b), pl.BlockSpec((1, 1, d), per_b),
                  pl.BlockSpec((d, nsplit * d), fixed), pl.BlockSpec((1, nsplit * d), fixed),
                  pl.BlockSpec((cw.shape[0], nsplit * d), fixed), pl.BlockSpec((1, nsplit * d), fixed)],
        out_specs=pl.BlockSpec((nsplit, tm, d), lambda i: (0, i, 0)),
        scratch_shapes=[pltpu.VMEM((tm + 2 * halo, d), BF16), pltpu.VMEM((tm + 2 * halo, d), F32)],
        compiler_params=_params(1), name="hyena_in",
    )(h, h, h, sc, sh, w, b, cw, cb)


def _hy_filter_rows(t, seq, w1_ref, b1_ref, w2_ref, b2_ref, w3_ref, b3_ref, n_bands):
    rows = t.shape[0]
    tf = t.astype(F32)
    t_lin = tf * (1.0 / (seq - 1))
    wpos = tf * (2.0 * math.pi / seq)
    lane = lax.broadcasted_iota(I32, (rows, LANES), 1)
    jb = jnp.where(lane > n_bands, lane - n_bands - 1, lane - 1).astype(F32)
    band = 1e-4 + jb * ((n_bands - 1 - 1e-4) / (n_bands - 1))
    ang = band * wpos
    z = jnp.where(lane == 0, t_lin,
                  jnp.where(lane <= n_bands, jnp.cos(ang), jnp.where(lane <= 2 * n_bands, -jnp.sin(ang), 0.0)))
    hf = jnp.sin(_dot(z.astype(BF16), w1_ref[...]) + b1_ref[...])
    hf = jnp.sin(_dot(hf.astype(BF16), w2_ref[...]) + b2_ref[...])
    hf = jnp.sin(_dot(hf.astype(BF16), w3_ref[...]) + b3_ref[...])
    return hf.astype(BF16), t_lin


def _hy_taps_kernel(w1_ref, b1_ref, w2_ref, b2_ref, w3_ref, b3_ref, w4_ref, o_ref, *, seq, n_bands):
    i = pl.program_id(0)
    n_ord, tr, d = o_ref.shape
    mlp = (w1_ref, b1_ref, w2_ref, b2_ref, w3_ref, b3_ref)
    r0 = i * tr
    second = (r0 >= seq).astype(I32)
    n_idx = r0 + lax.broadcasted_iota(I32, (tr, 1), 0)
    t = jnp.where(second == 1, 2 * seq - n_idx, n_idx)
    hf, t_lin = _hy_filter_rows(t, seq, *mlp, n_bands)
    c_idx = lax.broadcasted_iota(I32, (1, d), 1).astype(F32)
    min_decay = math.log(HY_DECAY_TARGET) / HY_DECAY_LONG_PCT
    max_decay = math.log(HY_DECAY_TARGET) / HY_DECAY_SHORT_PCT
    delta = jnp.abs(min_decay + c_idx * ((max_decay - min_decay) / (d - 1)))
    window = jnp.where(n_idx == seq, 0.0, jnp.exp(-t_lin * delta))
    for o in range(n_ord):
        o_ref[o] = _dot(hf, w4_ref[second, o]) * window

    @pl.when(i == 0)
    def _():
        hf0, _ = _hy_filter_rows(jnp.zeros((8, 1), I32), seq, *mlp, n_bands)
        first = lax.broadcasted_iota(I32, (8, 1), 0) == 0
        for o in range(n_ord):
            o_ref[o, 0:8, :] = o_ref[o, 0:8, :] + jnp.where(first, _dot(hf0, w4_ref[1, o]), 0.0)


def _hy_taps(seq, d, w1, b1, w2, b2, w3, b3, w4):
    emb, ff = w1.shape
    n_bands = (emb - 1) // 2
    w1p = jnp.zeros((LANES, ff), F32).at[:emb].set(w1).astype(BF16)
    w4r = w4.reshape(ff, HY_ORDER, HY_DIRS, d).transpose(2, 1, 0, 3).astype(BF16)
    tr = min(ROW_TILE, seq)
    fixed = lambda i: (0, 0)
    return pl.pallas_call(
        functools.partial(_hy_taps_kernel, seq=seq, n_bands=n_bands),
        out_shape=jax.ShapeDtypeStruct((HY_ORDER, 2 * seq, d), F32),
        grid=(2 * seq // tr,),
        in_specs=[pl.BlockSpec((LANES, ff), fixed), pl.BlockSpec((1, ff), fixed),
                  pl.BlockSpec((ff, ff), fixed), pl.BlockSpec((1, ff), fixed),
                  pl.BlockSpec((ff, ff), fixed), pl.BlockSpec((1, ff), fixed),
                  pl.BlockSpec((HY_DIRS, HY_ORDER, ff, d), lambda i: (0, 0, 0, 0))],
        out_specs=pl.BlockSpec((HY_ORDER, tr, d), lambda i: (0, i, 0)),
        compiler_params=_params(1), name="hyena_taps",
    )(w1p, b1.reshape(1, ff), w2.astype(BF16), b2.reshape(1, ff), w3.astype(BF16), b3.reshape(1, ff), w4r)


def _cos_sin(num, den):
    ang = (num % den).astype(F32) * (2.0 * math.pi / den)
    return jnp.cos(ang), jnp.sin(ang)


def _fft_tables(seq, n2):
    n = 2 * seq
    n1 = n // n2
    half = seq // n2
    ar = jnp.arange(n1, dtype=I32)
    ca, sa = _cos_sin(ar[:, None] * ar[None, :], n1)
    fa_half = jnp.concatenate([ca[:, :half], sa[:, :half]], 0).astype(BF16)
    fa_full = jnp.concatenate([ca, sa], 0).astype(BF16)
    ia = (jnp.concatenate([ca[:half], sa[:half]], 0) * (1.0 / n)).astype(BF16)
    k = ar[:, None, None] + n1 * jnp.arange(n2, dtype=I32)[None, :, None]
    cb, sb = _cos_sin(k * jnp.arange(n2, dtype=I32)[None, None, :], n)
    fb = jnp.concatenate([cb, sb], 1).astype(BF16)
    gb = jnp.concatenate([cb.transpose(0, 2, 1), sb.transpose(0, 2, 1)], 1).astype(BF16)
    return fa_half, fa_full, ia, fb, gb


def _pack_c(re, im):
    hi = pltpu.bitcast(re.astype(BF16).astype(F32), U32)
    lo = pltpu.bitcast(im.astype(BF16).astype(F32), U32)
    return hi | (lo >> 16)


def _unpack_c(w):
    re = pltpu.bitcast(w & jnp.uint32(0xFFFF0000), F32)
    im = pltpu.bitcast(w << 16, F32)
    return jnp.concatenate([re, im], axis=1).astype(BF16)


def _cmul_split(p, rows, cols, conj):
    a, b, c, d = p[:rows, :cols], p[:rows, cols:], p[rows:, :cols], p[rows:, cols:]
    return (a - d, b + c) if conj else (a + d, b - c)


def _fft_a_kernel(x_ref, f_ref, o_ref, *, cplx):
    n1 = f_ref.shape[0] // 2
    g, d = x_ref.shape[-2:]
    for s in range(g):
        if cplx:
            z = jnp.concatenate([x_ref[0, 0, :, s, :], x_ref[0, 1, :, s, :]], axis=1).astype(BF16)
            ar, ai = _cmul_split(_dot(f_ref[...], z), n1, d, False)
        else:
            p = _dot(f_ref[...], x_ref[0, 0, :, s, :].astype(BF16))
            ar, ai = p[:n1], -p[n1:]
        o_ref[0, :, s, :] = _pack_c(ar, ai)


def _fft_a(x5, which, f, cplx):
    _, nb, rows, n2, d = x5.shape
    n1 = f.shape[0] // 2
    per = 2 if cplx else 1
    groups = nb // per
    g = FFT_GROUP
    return pl.pallas_call(
        functools.partial(_fft_a_kernel, cplx=cplx),
        out_shape=jax.ShapeDtypeStruct((groups, n1, n2, d), U32),
        grid=(groups, n2 // g),
        in_specs=[pl.BlockSpec((1, per, rows, g, d), lambda p, j: (which, p, 0, j, 0)),
                  pl.BlockSpec((2 * n1, rows), lambda p, j: (0, 0))],
        out_specs=pl.BlockSpec((1, n1, g, d), lambda p, j: (p, 0, j, 0)),
        compiler_params=_params(2), name="fft_stage_a",
    )(x5, f)


def _fft_b_kernel(*refs, conv):
    if conv:
        a_ref, fb_ref, k_ref, gb_ref, o_ref = refs
    else:
        a_ref, fb_ref, o_ref = refs
    g, n2, d = a_ref.shape[1:]
    for s in range(g):
        xr, xi = _cmul_split(_dot(fb_ref[s], _unpack_c(a_ref[0, s])), n2, d, False)
        if not conv:
            o_ref[0, 0, s] = xr
            o_ref[0, 1, s] = xi
            continue
        kr, ki = k_ref[0, 0, s], k_ref[0, 1, s]
        y = jnp.concatenate([xr * kr - xi * ki, xr * ki + xi * kr], axis=1).astype(BF16)
        yr, yi = _cmul_split(_dot(gb_ref[s], y), n2, d, True)
        o_ref[0, :, s, :] = _pack_c(yr, yi)


def _fft_b(a4, fb, kspec=None, which=0, gb=None):
    groups, n1, n2, d = a4.shape
    conv = kspec is not None
    g = FFT_GROUP
    slab = pl.BlockSpec((1, g, n2, d), lambda k, p: (p, k, 0, 0))
    tab = pl.BlockSpec((g, 2 * n2, n2), lambda k, p: (k, 0, 0))
    if conv:
        in_specs = [slab, tab, pl.BlockSpec((1, 2, g, n2, d), lambda k, p: (which, 0, k, 0, 0)), tab]
        args = [a4, fb, kspec, gb]
        out_shape = jax.ShapeDtypeStruct((groups, n2, n1, d), U32)
        out_spec = pl.BlockSpec((1, n2, g, d), lambda k, p: (p, 0, k, 0))
    else:
        in_specs = [slab, tab]
        args = [a4, fb]
        out_shape = jax.ShapeDtypeStruct((groups, 2, n1, n2, d), F32)
        out_spec = pl.BlockSpec((1, 2, g, n2, d), lambda k, p: (p, 0, k, 0, 0))
    return pl.pallas_call(
        functools.partial(_fft_b_kernel, conv=conv),
        out_shape=out_shape,
        grid=(n1 // g, groups),
        in_specs=in_specs, out_specs=out_spec,
        compiler_params=_params(2), name="fft_stage_b_conv" if conv else "fft_stage_b",
    )(*args)


def _fft_c_kernel(y_ref, ia_ref, u_ref, g_ref, skip_ref, o_ref):
    half = ia_ref.shape[0] // 2
    g, _, d = y_ref.shape[1:]
    for s in range(g):
        cr, ci = _cmul_split(_dot(ia_ref[...], _unpack_c(y_ref[0, s])), half, d, True)
        o_ref[0, :, s, :] = cr
        o_ref[1, :, s, :] = ci
    skip = skip_ref[...][None]
    for b in range(2):
        o_ref[b] = g_ref[0, b] * (o_ref[b] + u_ref[0, b] * skip)


def _fft_c(y4, ia, u5, u_which, g5, g_which, skip):
    groups, n2, n1, d = y4.shape
    half = ia.shape[0] // 2
    g = FFT_GROUP
    return pl.pallas_call(
        _fft_c_kernel,
        out_shape=jax.ShapeDtypeStruct((2 * groups, half, n2, d), F32),
        grid=(groups, n2 // g),
        in_specs=[pl.BlockSpec((1, g, n1, d), lambda p, j: (p, j, 0, 0)),
                  pl.BlockSpec((2 * half, n1), lambda p, j: (0, 0)),
                  pl.BlockSpec((1, 2, half, g, d), lambda p, j: (u_which, p, 0, j, 0)),
                  pl.BlockSpec((1, 2, half, g, d), lambda p, j: (g_which, p, 0, j, 0)),
                  pl.BlockSpec((1, d), lambda p, j: (0, 0))],
        out_specs=pl.BlockSpec((2, half, g, d), lambda p, j: (p, 0, j, 0)),
        compiler_params=_params(2), name="fft_stage_c",
    )(y4, ia, u5, g5, skip.reshape(1, d))


def _hy_long_conv(zs, taps, skip, nb, seq):
    nsplit, n, d = zs.shape
    n2 = FFT_N2
    half = seq // n2
    n1 = 2 * half
    fa_half, fa_full, ia, fb, gb = _fft_tables(seq, n2)
    kspec = _fft_b(_fft_a(taps.reshape(1, HY_ORDER, n1, n2, d), 0, fa_full, False), fb)
    zs5 = zs.reshape(nsplit, nb, half, n2, d)
    u5, u_which = zs5, 0
    for o in range(HY_ORDER):
        a = _fft_a(u5, u_which, fa_half, True)
        y = _fft_b(a, fb, kspec, o, gb)
        out = _fft_c(y, ia, u5, u_which, zs5, o + 1, skip[o])
        u5, u_which = out[None], 0
    return out.reshape(n, d)


def _short_conv_kernel(u_ref, g_ref, t_ref, skip_ref, ff_ref, tf_ref, fi_ref, o_ref):
    seq, ct = u_ref.shape[-2:]
    n = 2 * seq
    z = jnp.concatenate([u_ref[0, 0], u_ref[0, 1]], axis=1).astype(BF16)
    xr, xi = _cmul_split(_dot(ff_ref[...], z), n, ct, False)
    pt = _dot(tf_ref[...], t_ref[0].astype(BF16))
    kr, ki = pt[:n], -pt[n:]
    y = jnp.concatenate([xr * kr - xi * ki, xr * ki + xi * kr], axis=1).astype(BF16)
    cr, ci = _cmul_split(_dot(fi_ref[...], y), seq, ct, True)
    skip = skip_ref[...]
    o_ref[0] = g_ref[0, 0] * (cr + u_ref[0, 0] * skip)
    o_ref[1] = g_ref[0, 1] * (ci + u_ref[0, 1] * skip)


def _hy_short_conv(zs, taps, skip, nb, seq):
    nsplit, n, d = zs.shape
    nn = 2 * seq
    ar = jnp.arange(nn, dtype=I32)
    cm, sm = _cos_sin(ar[:, None] * ar[None, :], nn)
    ff = jnp.concatenate([cm[:, :seq], sm[:, :seq]], 0).astype(BF16)
    tf = jnp.concatenate([cm, sm], 0).astype(BF16)
    fi = (jnp.concatenate([cm[:seq], sm[:seq]], 0) * (1.0 / nn)).astype(BF16)
    ct = d // 2
    zs4 = zs.reshape(nsplit, nb, seq, d)
    u4, u_which = zs4, 0
    for o in range(HY_ORDER):
        out = pl.pallas_call(
            _short_conv_kernel,
            out_shape=jax.ShapeDtypeStruct((nb, seq, d), F32),
            grid=(nb // 2, d // ct),
            in_specs=[pl.BlockSpec((1, 2, seq, ct), lambda p, j, w=u_which: (w, p, 0, j)),
                      pl.BlockSpec((1, 2, seq, ct), lambda p, j, w=o + 1: (w, p, 0, j)),
                      pl.BlockSpec((1, nn, ct), lambda p, j, w=o: (w, 0, j)),
                      pl.BlockSpec((1, ct), lambda p, j: (0, j)),
                      pl.BlockSpec((2 * nn, seq), lambda p, j: (0, 0)),
                      pl.BlockSpec((2 * nn, nn), lambda p, j: (0, 0)),
                      pl.BlockSpec((2 * seq, nn), lambda p, j: (0, 0))],
            out_specs=pl.BlockSpec((2, seq, ct), lambda p, j: (p, 0, j)),
            compiler_params=_params(2), name="hyena_short_conv",
        )(u4, zs4, taps, skip[o].reshape(1, d), ff, tf, fi)
        u4, u_which = out[None], 0
    return out.reshape(n, d)


def kernel(x, c, ctx, c_ctx, w_mod, b_mod, ln_g, ln_b, hy_w_in, hy_b_in, hy_conv_w, hy_conv_b, hy_f_w1, hy_f_b1, hy_f_w2, hy_f_b2, hy_f_w3, hy_f_b3, hy_f_w4, hy_skip, hy_w_out, hy_b_out, gla_w_in, gla_w_gate, gla_b_gate, gla_norm_g, gla_w_out, gm_w_in, gm_b_in, gm_ln_g, gm_ln_b, gm_ws, gm_bs, gm_w_out, gm_b_out, router_w, router_b, moe_w1, moe_w3, moe_w2):
    B, L, D = x.shape
    Lc = ctx.shape[1]
    depth = w_mod.shape[0]
    E = router_w.shape[1]
    alpha = (2.0 * depth) ** 0.25
    gla_layers = list(range(1, depth, N_MIXERS))
    last_ctx = gla_layers[-1] if gla_layers else -1

    cvec = jnp.zeros((8, D), F32).at[:B].set(c).at[B].set(c_ctx)
    mod = _modulation(cvec, w_mod, b_mod)
    h = _add_pos(x.reshape(B * L, D), L)
    hc = ctx.reshape(B * Lc, D)
    rw_t = router_w.T
    rw_hi = rw_t.astype(BF16)
    rhl = jnp.concatenate([rw_hi, (rw_t - rw_hi.astype(F32)).astype(BF16)], axis=0)
    rb_b = jnp.broadcast_to(router_b[:, None], (E, ROW_TILE))
    zero_cnt = jnp.zeros((E, LANES), F32)
    row2 = lambda v: v.reshape(1, -1)

    for i in range(depth):
        kind, j = i % N_MIXERS, i // N_MIXERS
        ctx_full = i < last_ctx
        ctx_any = i <= last_ctx
        lat = [mod[i, :B, k * D:(k + 1) * D].reshape(B, 1, D) for k in range(6)]
        cm = [jnp.broadcast_to(mod[i, B, k * D:(k + 1) * D].reshape(1, 1, D), (B, 1, D)) for k in range(6)]
        streams_in = [(h, lat, L)] + ([(hc, cm, Lc)] if ctx_any else [])
        pre = []
        if kind == 0:
            w_in = hy_w_in[j].astype(BF16)
            for s_h, s_m, s_len in streams_in[:1 + int(ctx_full)]:
                zs = _hy_in(s_h, s_m[1], s_m[0], w_in, row2(hy_b_in[j]), hy_conv_w[j], row2(hy_conv_b[j]), s_len)
                taps = _hy_taps(s_len, D, hy_f_w1[j], hy_f_b1[j], hy_f_w2[j], hy_f_b2[j], hy_f_w3[j], hy_f_b3[j],
                                hy_f_w4[j])
                if s_len == L:
                    y2 = _hy_long_conv(zs, taps, hy_skip[j], B, s_len)
                else:
                    y2 = _hy_short_conv(zs, taps, hy_skip[j], B, s_len)
                pre.append(("hyena", (y2,), lambda tm, d: [pl.BlockSpec((tm, d), lambda t: (t, 0))]))
            w_out, b_out = hy_w_out[j].astype(BF16), row2(hy_b_out[j])
        elif kind == 1:
            w_main, w_lr, w_g, b_g = _gla_weights(gla_w_in[j], gla_w_gate[j], gla_b_gate[j], D)
            proj = [_gla_in(s_h, s_m[1], s_m[0], w_main, w_lr, w_g, b_g, s_len) for s_h, s_m, s_len in streams_in]
            dvh = D // GLA_HEADS
            dkh = (D // 2) // GLA_HEADS
            zero_state = jnp.zeros((B, GLA_HEADS, dvh, dkh), F32)
            if ctx_any:
                ocf, ocb, s_f, s_b = _gla_scan(*proj[1], zero_state, zero_state, B, Lc)
            else:
                s_f = s_b = zero_state
            o_f, o_b, _, _ = _gla_scan(*proj[0], s_f, s_b, B, L)
            gla_specs = lambda tm, d: [pl.BlockSpec((tm, d), lambda t: (t, 0))] * 3 + [pl.BlockSpec((1, d), lambda t: (0, 0))]
            pre.append(("gla", (o_f, o_b, proj[0][2], row2(gla_norm_g[j])), gla_specs))
            if ctx_full:
                pre.append(("gla", (ocf, ocb, proj[1][2], row2(gla_norm_g[j])), gla_specs))
            w_out, b_out = gla_w_out[j].astype(BF16), jnp.zeros((1, D), F32)
        else:
            w_in = gm_w_in[j].astype(BF16)
            ws = gm_ws[j].astype(BF16)
            bs_exp = jnp.repeat(gm_bs[j].T, D // GM_HEADS, axis=1)
            gm_specs = lambda tm, d: [pl.BlockSpec((tm, d), lambda t: (t, 0))] * 2 + [
                pl.BlockSpec((GM_HEADS, GM_CHUNK, GM_CHUNK), lambda t: (0, 0, 0)), pl.BlockSpec((GM_CHUNK, d), lambda t: (0, 0))]
            for s_h, s_m, s_len in streams_in[:1 + int(ctx_full)]:
                u, vn = _gm_in(s_h, s_m[1], s_m[0], w_in, row2(gm_b_in[j]), row2(gm_ln_g[j]), row2(gm_ln_b[j]), s_len)
                pre.append(("gmlp", (u, vn, ws, bs_exp), gm_specs))
            w_out, b_out = gm_w_out[j].astype(BF16), row2(gm_b_out[j])

        moe_streams = []
        cnt = zero_cnt
        for (s_h, s_m, s_len), (pk, pargs, pspecs) in zip(streams_in, pre):
            h1, tok, meta, cnt = _post(pk, pargs, pspecs, s_h, w_out, b_out, s_m[2], row2(ln_g[i, 0]), row2(ln_b[i, 0]),
                                       s_m[4], s_m[3], rhl, rb_b, cnt, s_len, alpha)
            moe_streams.append(dict(h1=h1, tok=tok, meta=meta, g2=s_m[5], lg=row2(ln_g[i, 1]), lb=row2(ln_b[i, 1]),
                                    seq=s_len))
        outs = _moe(moe_streams, cnt, moe_w1, moe_w3, moe_w2, i, alpha)
        h = outs[0]
        if ctx_full:
            hc = outs[1]
    return h.reshape(B, L, D)
```

```python
import functools
import math

import jax
import jax.numpy as jnp
from jax import lax
from jax.experimental import pallas as pl
from jax.experimental.pallas import tpu as pltpu

F32 = jnp.float32
BF16 = jnp.bfloat16
I32 = jnp.int32
U32 = jnp.uint32
HIGHEST = lax.Precision.HIGHEST

GRID_W = 64
N_MIXERS = 3
LN_EPS = 1e-5
HY_ORDER = 2
HY_DIRS = 2
HY_DECAY_TARGET = 1e-2
HY_DECAY_SHORT_PCT = 0.3
HY_DECAY_LONG_PCT = 1.5
GLA_HEADS = 4
GLA_RANK = 16
GLA_TAU = 16.0
GLA_CHUNK = 64
GM_CHUNK = 128
GM_HEADS = 4
N_GROUPS = 4
TOP_K = 2

LANES = 128
SUBLANES = 8
V7X_VMEM_LIMIT_BYTES = 56 * 1024 * 1024
MOE_ROWS = 256
ROW_TILE = 512
DMA_TILE = 256
DMA_UNROLL = 8
FFT_N2 = 128
FFT_GROUP = SUBLANES


def _params(n_grid):
    return pltpu.CompilerParams(dimension_semantics=("arbitrary",) * n_grid,
                                vmem_limit_bytes=V7X_VMEM_LIMIT_BYTES)


def _dot(a, b):
    return jnp.dot(a, b, preferred_element_type=F32)


def _dot_nt(a, b):
    return lax.dot_general(a, b, (((1,), (1,)), ((), ())), preferred_element_type=F32)


def _dot_tn(a, b):
    return lax.dot_general(a, b, (((0,), (0,)), ((), ())), preferred_element_type=F32)


def _rows_to_tiles(x):
    rows, d = x.shape
    nsub = d // LANES
    parts = [x[:, s * LANES:(s + 1) * LANES].reshape(rows // SUBLANES, SUBLANES, LANES) for s in range(nsub)]
    return jnp.swapaxes(jnp.stack(parts, axis=1), 1, 2).reshape(rows, nsub, LANES)


def _tiles_to_rows(x):
    rows, nsub, _ = x.shape
    y = jnp.swapaxes(x.reshape(rows // SUBLANES, SUBLANES, nsub, LANES), 1, 2)
    return jnp.concatenate([y[:, s].reshape(rows, LANES) for s in range(nsub)], axis=1)


def _ln(x, g, b):
    mu = jnp.mean(x, -1, keepdims=True)
    xc = x - mu
    var = jnp.mean(xc * xc, -1, keepdims=True)
    return xc * lax.rsqrt(var + LN_EPS) * g + b


def _silu(x):
    return x * jax.nn.sigmoid(x)


def _gelu_tanh(x):
    return 0.5 * x * (1.0 + jnp.tanh(math.sqrt(2.0 / math.pi) * (x + 0.044715 * (x * x * x))))


def _mod_kernel(c_ref, w_ref, b_ref, o_ref):
    s = _silu(c_ref[...])
    o_ref[0] = _dot(s.astype(BF16), w_ref[0].astype(BF16)) + b_ref[0]


def _modulation(cvec, w_mod, b_mod):
    depth, d, n = w_mod.shape
    tn = n // 4
    return pl.pallas_call(
        _mod_kernel,
        out_shape=jax.ShapeDtypeStruct((depth, 8, n), F32),
        grid=(depth, n // tn),
        in_specs=[pl.BlockSpec((8, d), lambda i, j: (0, 0)),
                  pl.BlockSpec((1, d, tn), lambda i, j: (i, 0, j)),
                  pl.BlockSpec((1, 1, tn), lambda i, j: (i, 0, j))],
        out_specs=pl.BlockSpec((1, 8, tn), lambda i, j: (i, 0, j)),
        compiler_params=_params(2), name="modulation",
    )(cvec, w_mod, b_mod.reshape(depth, 1, n))


def _pos_table_kernel(o_ref, *, q):
    rows, cols = o_ref.shape
    p = lax.broadcasted_iota(I32, (rows, cols), 0).astype(F32)
    lane = lax.broadcasted_iota(I32, (rows, cols), 1)
    j = jnp.where(lane >= q, lane - q, lane).astype(F32)
    omega = jnp.exp(j * (-math.log(10000.0) / q))
    ang = p * omega
    o_ref[...] = jnp.where(lane >= q, jnp.cos(ang), jnp.sin(ang))


def _pos_table(n, d):
    q = d // 4
    return pl.pallas_call(functools.partial(_pos_table_kernel, q=q),
                          out_shape=jax.ShapeDtypeStruct((n, 2 * q), F32), name="pos_table")()


def _add_pos_kernel(x_ref, er_ref, ec_ref, o_ref, *, half):
    tm = x_ref.shape[0]
    reps = tm // GRID_W
    er = er_ref[...]
    er_rows = jnp.broadcast_to(er[:, None, :], (reps, GRID_W, half)).reshape(tm, half)
    ec_rows = jnp.broadcast_to(ec_ref[...][None], (reps, GRID_W, half)).reshape(tm, half)
    o_ref[:, :half] = x_ref[:, :half] + er_rows
    o_ref[:, half:] = x_ref[:, half:] + ec_rows


def _add_pos(x2d, seq):
    n, d = x2d.shape
    half = d // 2
    rows = seq // GRID_W
    er = _pos_table(rows, d)
    ec = _pos_table(GRID_W, d)
    tm = ROW_TILE
    reps = tm // GRID_W
    tps = seq // tm
    return pl.pallas_call(
        functools.partial(_add_pos_kernel, half=half),
        out_shape=jax.ShapeDtypeStruct((n, d), F32),
        grid=(n // tm,),
        in_specs=[pl.BlockSpec((tm, d), lambda i: (i, 0)),
                  pl.BlockSpec((reps, half), lambda i: (i % tps, 0)),
                  pl.BlockSpec((GRID_W, half), lambda i: (0, 0))],
        out_specs=pl.BlockSpec((tm, d), lambda i: (i, 0)),
        compiler_params=_params(1), name="add_pos",
    )(x2d, er, ec)


def _gm_in_kernel(h_ref, sc_ref, sh_ref, w_ref, b_ref, g_ref, bb_ref, u_ref, v_ref):
    d = h_ref.shape[1]
    a = (h_ref[...] * (1.0 + sc_ref[0]) + sh_ref[0]).astype(BF16)
    u_ref[...] = _gelu_tanh(_dot(a, w_ref[:, :d]) + b_ref[:, :d])
    v = _gelu_tanh(_dot(a, w_ref[:, d:]) + b_ref[:, d:])
    v_ref[...] = _ln(v, g_ref[...], bb_ref[...]).astype(BF16)


def _gm_in(h, sc, sh, w, b, g, bb, seq):
    n, d = h.shape
    tm = min(ROW_TILE, seq)
    tps = seq // tm
    row = lambda i: (i, 0)
    per_b = lambda i: (i // tps, 0, 0)
    fixed = lambda i: (0, 0)
    return pl.pallas_call(
        _gm_in_kernel,
        out_shape=(jax.ShapeDtypeStruct((n, d), F32), jax.ShapeDtypeStruct((n, d), BF16)),
        grid=(n // tm,),
        in_specs=[pl.BlockSpec((tm, d), row), pl.BlockSpec((1, 1, d), per_b), pl.BlockSpec((1, 1, d), per_b),
                  pl.BlockSpec((d, 2 * d), fixed), pl.BlockSpec((1, 2 * d), fixed),
                  pl.BlockSpec((1, d), fixed), pl.BlockSpec((1, d), fixed)],
        out_specs=(pl.BlockSpec((tm, d), row), pl.BlockSpec((tm, d), row)),
        compiler_params=_params(1), name="gmlp_in",
    )(h, sc, sh, w, b, g, bb)


def _gm_prologue(u_ref, v_ref, ws_ref, bs_ref):
    tm, d = u_ref.shape
    dh = d // GM_HEADS
    for c in range(tm // GM_CHUNK):
        rows = slice(c * GM_CHUNK, (c + 1) * GM_CHUNK)
        parts = [_dot(ws_ref[g], v_ref[rows, g * dh:(g + 1) * dh]) for g in range(GM_HEADS)]
        vm = jnp.concatenate(parts, axis=1) + bs_ref[...]
        yield rows, (u_ref[rows, :] * vm).astype(BF16)


def _gla_prologue(of_ref, ob_ref, r_ref, ng_ref):
    tm, d = of_ref.shape
    dh = d // GLA_HEADS
    o = of_ref[...] + ob_ref[...]
    parts = []
    for hd in range(GLA_HEADS):
        oh = o[:, hd * dh:(hd + 1) * dh]
        mu = jnp.mean(oh, -1, keepdims=True)
        oc = oh - mu
        var = jnp.mean(oc * oc, -1, keepdims=True)
        parts.append(oc * lax.rsqrt(var + LN_EPS) * ng_ref[:, hd * dh:(hd + 1) * dh])
    y = jnp.concatenate(parts, axis=1) * _silu(r_ref[...])
    yield slice(0, tm), y.astype(BF16)


def _hy_prologue(y_ref):
    yield slice(0, y_ref.shape[0]), y_ref[...].astype(BF16)


_PROLOGUES = {"gmlp": (_gm_prologue, 4), "gla": (_gla_prologue, 4), "hyena": (_hy_prologue, 1)}


def _route_t(scores, biased):
    n_experts, tm = scores.shape
    gsz = n_experts // N_GROUPS
    neg = jnp.float32(-jnp.inf)
    v3 = biased.reshape(N_GROUPS, gsz, tm)
    sub = lax.broadcasted_iota(I32, v3.shape, 1).astype(F32)
    m1 = jnp.max(v3, axis=1, keepdims=True)
    i1 = jnp.min(jnp.where(v3 == m1, sub, float(gsz)), axis=1, keepdims=True)
    v3b = jnp.where(sub == i1, neg, v3)
    m2 = jnp.max(v3b, axis=1, keepdims=True)
    i2 = jnp.min(jnp.where(v3b == m2, sub, float(gsz)), axis=1, keepdims=True)
    gscore = (m1 + m2).reshape(N_GROUPS, tm)
    i1 = i1.reshape(N_GROUPS, tm)
    i2 = i2.reshape(N_GROUPS, tm)
    best, e0, e1 = gscore[0:1], i1[0:1], i2[0:1]
    for g in range(1, N_GROUPS):
        better = gscore[g:g + 1] > best
        best = jnp.where(better, gscore[g:g + 1], best)
        e0 = jnp.where(better, i1[g:g + 1] + float(g * gsz), e0)
        e1 = jnp.where(better, i2[g:g + 1] + float(g * gsz), e1)
    row = lax.broadcasted_iota(I32, scores.shape, 0).astype(F32)
    oh0 = (row == e0).astype(F32)
    oh1 = (row == e1).astype(F32)
    w0 = jnp.sum(oh0 * scores, axis=0, keepdims=True)
    w1 = jnp.sum(oh1 * scores, axis=0, keepdims=True)
    den = w0 + w1
    return e0, e1, oh0, oh1, w0 / den, w1 / den


def _post_kernel(*refs, kind, alpha):
    prologue, n_pro = _PROLOGUES[kind]
    pro = refs[:n_pro]
    (h_ref, w_ref, b_ref, g1_ref, lg_ref, lb_ref, sc_ref, sh_ref, rhl_ref, rb_ref, cin_ref,
     h1_ref, tok_ref, meta_ref, cout_ref, cnt_scr) = refs[n_pro:]
    i = pl.program_id(0)
    tm = h_ref.shape[0]
    n_experts = rb_ref.shape[0]

    @pl.when(i == 0)
    def _():
        cnt_scr[...] = cin_ref[...]

    for rows, y in prologue(*pro):
        out = _dot(y, w_ref[...]) + b_ref[...]
        h1_ref[rows, :] = _ln(alpha * h_ref[rows, :] + g1_ref[0] * out, lg_ref[...], lb_ref[...])
    tok = h1_ref[...] * (1.0 + sc_ref[0]) + sh_ref[0]
    tok_ref[...] = _rows_to_tiles(tok)
    t_hi = tok.astype(BF16)
    t_lo = (tok - t_hi.astype(F32)).astype(BF16)
    p_hi = _dot_nt(rhl_ref[...], t_hi)
    logits = p_hi[:n_experts] + p_hi[n_experts:] + _dot_nt(rhl_ref[0:n_experts, :], t_lo)
    scores = jax.nn.sigmoid(logits)
    e0, e1, oh0, oh1, w0, w1 = _route_t(scores, scores + rb_ref[...])
    r_i = lax.broadcasted_iota(I32, (tm, tm), 0)
    c_i = lax.broadcasted_iota(I32, (tm, tm), 1)
    upper = (r_i < c_i).astype(BF16)
    oh = oh0 + oh1
    carry = cnt_scr[...]
    before = _dot(oh.astype(BF16), upper) + jnp.concatenate([carry] * (tm // LANES), axis=1)
    rank0 = jnp.sum(oh0 * before, axis=0, keepdims=True)
    rank1 = jnp.sum(oh1 * before, axis=0, keepdims=True)
    cnt_scr[...] = carry + jnp.sum(oh, axis=1, keepdims=True)
    zero = jnp.zeros_like(w0)
    meta_ref[0] = jnp.concatenate([e0, e1, rank0, rank1, w0, w1, zero, zero], axis=0)
    cout_ref[...] = cnt_scr[...]


def _post(kind, pro_args, pro_specs, h, w, b, g1, lg, lb, sc, sh, rhl, rb, cnt_in, seq, alpha):
    n, d = h.shape
    tm = min(ROW_TILE, seq)
    tps = seq // tm
    n_experts = rb.shape[0]
    row = lambda i: (i, 0)
    per_b = lambda i: (i // tps, 0, 0)
    fixed = lambda i: (0, 0)
    in_specs = list(pro_specs(tm, d)) + [
        pl.BlockSpec((tm, d), row), pl.BlockSpec((d, d), fixed), pl.BlockSpec((1, d), fixed),
        pl.BlockSpec((1, 1, d), per_b), pl.BlockSpec((1, d), fixed), pl.BlockSpec((1, d), fixed),
        pl.BlockSpec((1, 1, d), per_b), pl.BlockSpec((1, 1, d), per_b),
        pl.BlockSpec((2 * n_experts, d), fixed), pl.BlockSpec((n_experts, tm), fixed),
        pl.BlockSpec((n_experts, LANES), fixed)]
    return pl.pallas_call(
        functools.partial(_post_kernel, kind=kind, alpha=alpha),
        out_shape=(jax.ShapeDtypeStruct((n, d), F32), jax.ShapeDtypeStruct((n, d // LANES, LANES), F32),
                   jax.ShapeDtypeStruct((n // tm, SUBLANES, tm), F32),
                   jax.ShapeDtypeStruct((n_experts, LANES), F32)),
        grid=(n // tm,),
        in_specs=in_specs,
        out_specs=(pl.BlockSpec((tm, d), row), pl.BlockSpec((tm, d // LANES, LANES), lambda i: (i, 0, 0)),
                   pl.BlockSpec((1, SUBLANES, tm), lambda i: (i, 0, 0)), pl.BlockSpec((n_experts, LANES), fixed)),
        scratch_shapes=[pltpu.VMEM((n_experts, LANES), F32)],
        compiler_params=_params(1), name="post_" + kind,
    )(*pro_args, h, w, b, g1, lg, lb, sc, sh, rhl, rb[:, :tm], cnt_in)


def _dest_kernel(meta_ref, pstart_ref, o_ref):
    meta = meta_ref[0]
    n_experts = pstart_ref.shape[0]
    tm = meta.shape[1]
    row = lax.broadcasted_iota(I32, (n_experts, tm), 0).astype(F32)
    rows = []
    for k in range(TOP_K):
        off = jnp.sum(jnp.where(row == meta[k:k + 1], pstart_ref[...], 0.0), axis=0, keepdims=True)
        rows.append(off + meta[TOP_K + k:TOP_K + k + 1])
    rows += [jnp.zeros_like(rows[0])] * (SUBLANES - TOP_K)
    o_ref[0] = jnp.concatenate(rows, axis=0).astype(I32)


def _dest(meta, pstart_b):
    ntile, _, tmeta = meta.shape
    tm = DMA_TILE
    per = tmeta // tm
    n_experts = pstart_b.shape[0]
    return pl.pallas_call(
        _dest_kernel,
        out_shape=jax.ShapeDtypeStruct((ntile * per, SUBLANES, tm), I32),
        grid=(ntile * per,),
        in_specs=[pl.BlockSpec((1, SUBLANES, tm), lambda i: (i // per, 0, i % per)),
                  pl.BlockSpec((n_experts, tm), lambda i: (0, 0))],
        out_specs=pl.BlockSpec((1, SUBLANES, tm), lambda i: (i, 0, 0)),
        compiler_params=_params(1), name="moe_dest",
    )(meta, pstart_b[:, :tm])


def _row_loop(tm, body):
    def step(it, c):
        for u in range(DMA_UNROLL):
            body(it * DMA_UNROLL + u, u)
        return c
    lax.fori_loop(0, tm // DMA_UNROLL, step, 0)


def _dispatch_kernel(lb_ref, dest_ref, tok_ref, *rest, fill):
    if fill:
        xs_ref, idx_scr, tok_scr, zero_scr, isem, lsem, rsem, zsem = rest
    else:
        _, xs_ref, idx_scr, tok_scr, zero_scr, isem, lsem, rsem, zsem = rest
    i = pl.program_id(0)
    nstep = pl.num_programs(0)
    tm = idx_scr.shape[2]
    slot = i % 2

    def load(step, to):
        return pltpu.make_async_copy(tok_ref.at[pl.ds(step * tm, tm)], tok_scr.at[to], lsem.at[to])

    @pl.when(i == 0)
    def _():
        load(0, 0).start()

    if fill:
        @pl.when(i == 0)
        def _():
            zero_scr[...] = jnp.zeros_like(zero_scr)

            def fill_copy(e):
                start_row = pl.multiple_of(jnp.maximum(lb_ref[e], 0), MOE_ROWS)
                return pltpu.make_async_copy(zero_scr, xs_ref.at[pl.ds(start_row, MOE_ROWS)], zsem)

            def start(e, c):
                @pl.when(lb_ref[e] >= 0)
                def _():
                    fill_copy(e).start()
                return c

            def wait(e, c):
                @pl.when(lb_ref[e] >= 0)
                def _():
                    fill_copy(e).wait()
                return c

            lax.fori_loop(0, lb_ref.shape[0], start, 0)
            lax.fori_loop(0, lb_ref.shape[0], wait, 0)

    for k in range(TOP_K):
        pltpu.make_async_copy(dest_ref.at[i, k], idx_scr.at[slot, k], isem).start()
    for k in range(TOP_K):
        pltpu.make_async_copy(dest_ref.at[i, k], idx_scr.at[slot, k], isem).wait()

    def row_copy(s, r, k):
        return pltpu.make_async_copy(tok_scr.at[s, r], xs_ref.at[idx_scr[s, k, r]], rsem.at[s])

    def start_row(r, u):
        for k in range(TOP_K):
            row_copy(slot, r, k).start(priority=(u + k) % 2)

    def wait_rows(s):
        def wait_row(r, u):
            for k in range(TOP_K):
                row_copy(s, r, k).wait()
        _row_loop(tm, wait_row)

    load(i, slot).wait()
    _row_loop(tm, start_row)

    @pl.when(i > 0)
    def _():
        wait_rows(1 - slot)

    @pl.when(i + 1 < nstep)
    def _():
        load(i + 1, 1 - slot).start()

    @pl.when(i == nstep - 1)
    def _():
        wait_rows(slot)


def _dispatch(last_blk, dest, tok, xs, p):
    n, ts, _ = tok.shape
    tm = DMA_TILE
    fill = xs is None
    any_spec = pl.BlockSpec(memory_space=pl.ANY)
    args = [last_blk, dest, tok] + ([] if fill else [xs])
    return pl.pallas_call(
        functools.partial(_dispatch_kernel, fill=fill),
        out_shape=jax.ShapeDtypeStruct((p, ts, LANES), F32),
        grid_spec=pltpu.PrefetchScalarGridSpec(
            num_scalar_prefetch=1, grid=(n // tm,), in_specs=[any_spec] * (len(args) - 1), out_specs=any_spec,
            scratch_shapes=[pltpu.SMEM((2, TOP_K, tm), I32), pltpu.VMEM((2, tm, ts, LANES), F32),
                            pltpu.VMEM((MOE_ROWS, ts, LANES), F32), pltpu.SemaphoreType.DMA,
                            pltpu.SemaphoreType.DMA((2,)), pltpu.SemaphoreType.DMA((2,)), pltpu.SemaphoreType.DMA]),
        input_output_aliases={} if fill else {3: 0},
        compiler_params=_params(1), name="moe_dispatch",
    )(*args)


def _expert_kernel(be_ref, nu_ref, x_ref, w1_ref, w3_ref, w2_ref, y_ref, w1_scr, w3_scr, w2_scr):
    j = pl.program_id(0)
    e = be_ref[j]
    e_prev = be_ref[jnp.maximum(j - 1, 0)]

    @pl.when(j < nu_ref[0])
    def _():
        @pl.when((j == 0) | (e != e_prev))
        def _():
            w1_scr[...] = w1_ref[0, 0].astype(BF16)
            w3_scr[...] = w3_ref[0, 0].astype(BF16)
            w2_scr[...] = w2_ref[0, 0].astype(BF16)

        x = _tiles_to_rows(x_ref[...]).astype(BF16)
        hid = _silu(_dot(x, w1_scr[...])) * _dot(x, w3_scr[...])
        y_ref[...] = _rows_to_tiles(_dot(hid.astype(BF16), w2_scr[...]))

    @pl.when(j >= nu_ref[0])
    def _():
        y_ref[...] = jnp.zeros_like(y_ref)


def _experts(blk_e, n_used, xs, w1, w3, w2, layer):
    p, ts, _ = xs.shape
    d = ts * LANES
    de = w1.shape[-1]
    nblk = p // MOE_ROWS
    wmap = lambda j, be, nu: (layer, be[j], 0, 0)
    xmap = lambda j, be, nu: (jnp.minimum(j, nu[0] - 1), 0, 0)
    return pl.pallas_call(
        _expert_kernel,
        out_shape=jax.ShapeDtypeStruct((p, ts, LANES), F32),
        grid_spec=pltpu.PrefetchScalarGridSpec(
            num_scalar_prefetch=2, grid=(nblk,),
            in_specs=[pl.BlockSpec((MOE_ROWS, ts, LANES), xmap),
                      pl.BlockSpec((1, 1, d, de), wmap), pl.BlockSpec((1, 1, d, de), wmap),
                      pl.BlockSpec((1, 1, de, d), wmap)],
            out_specs=pl.BlockSpec((MOE_ROWS, ts, LANES), lambda j, be, nu: (j, 0, 0)),
            scratch_shapes=[pltpu.VMEM((d, de), BF16), pltpu.VMEM((d, de), BF16), pltpu.VMEM((de, d), BF16)]),
        compiler_params=_params(1), name="moe_experts",
    )(blk_e, n_used, xs, w1, w3, w2)


def _combine_kernel(dest_ref, ys_ref, meta_ref, h_ref, g2_ref, lg_ref, lb_ref, o_ref,
                    idx_scr, buf_scr, isem, rsem, *, alpha):
    i = pl.program_id(0)
    nstep = pl.num_programs(0)
    tm = h_ref.shape[0]

    def gather(step, slot):
        for k in range(TOP_K):
            pltpu.make_async_copy(dest_ref.at[step, k], idx_scr.at[slot, k], isem).start()
        for k in range(TOP_K):
            pltpu.make_async_copy(dest_ref.at[step, k], idx_scr.at[slot, k], isem).wait()

        def start_row(r, u):
            for k in range(TOP_K):
                pltpu.make_async_copy(ys_ref.at[idx_scr[slot, k, r]], buf_scr.at[slot, k, r],
                                      rsem.at[slot]).start(priority=(u + k) % 2)

        _row_loop(tm, start_row)

    slot = i % 2

    @pl.when(i == 0)
    def _():
        gather(0, 0)

    @pl.when(i + 1 < nstep)
    def _():
        gather(i + 1, 1 - slot)

    def wait_row(r, u):
        for k in range(TOP_K):
            pltpu.make_async_copy(ys_ref.at[idx_scr[slot, k, r]], buf_scr.at[slot, k, r], rsem.at[slot]).wait()

    _row_loop(tm, wait_row)
    meta = meta_ref[0]
    wt = jnp.concatenate([meta, jnp.zeros((LANES - SUBLANES, tm), F32)], axis=0).T
    f = wt[:, 2 * TOP_K:2 * TOP_K + 1] * _tiles_to_rows(buf_scr[slot, 0])
    for k in range(1, TOP_K):
        f = f + wt[:, 2 * TOP_K + k:2 * TOP_K + k + 1] * _tiles_to_rows(buf_scr[slot, k])
    o_ref[...] = _ln(alpha * h_ref[...] + g2_ref[0] * f, lg_ref[...], lb_ref[...])


def _combine(dest, ys, meta, h, g2, lg, lb, seq, alpha):
    n, d = h.shape
    tm = DMA_TILE
    tps = seq // tm
    per = meta.shape[2] // tm
    row = lambda i: (i, 0)
    fixed = lambda i: (0, 0)
    return pl.pallas_call(
        functools.partial(_combine_kernel, alpha=alpha),
        out_shape=jax.ShapeDtypeStruct((n, d), F32),
        grid=(n // tm,),
        in_specs=[pl.BlockSpec(memory_space=pl.ANY), pl.BlockSpec(memory_space=pl.ANY),
                  pl.BlockSpec((1, SUBLANES, tm), lambda i: (i // per, 0, i % per)), pl.BlockSpec((tm, d), row),
                  pl.BlockSpec((1, 1, d), lambda i: (i // tps, 0, 0)),
                  pl.BlockSpec((1, d), fixed), pl.BlockSpec((1, d), fixed)],
        out_specs=pl.BlockSpec((tm, d), row),
        scratch_shapes=[pltpu.SMEM((2, TOP_K, tm), I32), pltpu.VMEM((2, TOP_K, tm, d // LANES, LANES), F32),
                        pltpu.SemaphoreType.DMA, pltpu.SemaphoreType.DMA((2,))],
        compiler_params=_params(1), name="moe_combine",
    )(dest, ys, meta, h, g2, lg, lb)


def _moe(streams, counts, w1, w3, w2, layer, alpha):
    n_experts = w1.shape[1]
    n_assign = TOP_K * sum(s["tok"].shape[0] for s in streams)
    p = n_assign + n_experts * MOE_ROWS
    nblk = p // MOE_ROWS
    cnt = counts[:, 0].astype(I32)
    padded = (cnt + MOE_ROWS - 1) // MOE_ROWS * MOE_ROWS
    pend = jnp.cumsum(padded)
    pstart = pend - padded
    n_used = (pend[-1] // MOE_ROWS).astype(I32).reshape(1)
    blk = jnp.arange(nblk, dtype=I32) * MOE_ROWS
    blk_e = jnp.sum((pend[None, :] <= blk[:, None]).astype(I32), axis=1)
    last_e = jnp.sum((pend <= jnp.maximum(pend[-1] - 1, 0)).astype(I32))
    blk_e = jnp.minimum(blk_e, last_e).astype(I32)
    last_blk = jnp.where(padded > 0, pend - MOE_ROWS, -1).astype(I32)
    pstart_b = jnp.broadcast_to(pstart.astype(F32)[:, None], (n_experts, ROW_TILE))
    dests = [_dest(s["meta"], pstart_b) for s in streams]
    xs = None
    for s, dst in zip(streams, dests):
        xs = _dispatch(last_blk, dst, s["tok"], xs, p)
    ys = _experts(blk_e, n_used, xs, w1, w3, w2, layer)
    return [_combine(dst, ys, s["meta"], s["h1"], s["g2"], s["lg"], s["lb"], s["seq"], alpha)
            for s, dst in zip(streams, dests)]


def _log_sigmoid(x):
    return jnp.minimum(x, 0.0) - jnp.log(1.0 + jnp.exp(-jnp.abs(x)))


def _gla_in_kernel(h_ref, sc_ref, sh_ref, w_ref, wl_ref, wg_ref, bg_ref, qk_ref, v_ref, r_ref, g_ref, *, qscale):
    tm, d = h_ref.shape
    a = (h_ref[...] * (1.0 + sc_ref[0]) + sh_ref[0]).astype(BF16)
    qk = _dot(a, w_ref[:, :d])
    lane = lax.broadcasted_iota(I32, (tm, d), 1)
    qk_ref[...] = jnp.where(lane < d // 2, qk * qscale, qk)
    v_ref[...] = _dot(a, w_ref[:, d:2 * d]).astype(BF16)
    r_ref[...] = _dot(a, w_ref[:, 2 * d:])
    lr = _dot(a, wl_ref[...])
    gpre = _dot(lr.astype(BF16), wg_ref[...]) + bg_ref[...]
    g_ref[...] = _log_sigmoid(gpre) * (1.0 / GLA_TAU)


def _gla_weights(w_in, w_gate, b_gate, d):
    dk = d // 2
    n_lr = HY_DIRS * GLA_RANK
    w_main = w_in[:, :2 * dk + 2 * d].astype(BF16)
    w_lr = jnp.zeros((d, LANES), F32).at[:, :n_lr].set(w_in[:, 2 * dk + 2 * d:]).astype(BF16)
    w_g = jnp.zeros((LANES, HY_DIRS * dk), F32)
    for dr in range(HY_DIRS):
        w_g = w_g.at[dr * GLA_RANK:(dr + 1) * GLA_RANK, dr * dk:(dr + 1) * dk].set(w_gate[dr])
    return w_main, w_lr, w_g.astype(BF16), b_gate.reshape(1, HY_DIRS * dk)


def _gla_in(h, sc, sh, w_main, w_lr, w_g, b_g, seq):
    n, d = h.shape
    tm = min(ROW_TILE, seq)
    tps = seq // tm
    row = lambda i: (i, 0)
    per_b = lambda i: (i // tps, 0, 0)
    fixed = lambda i: (0, 0)
    qscale = float(((d // 2) // GLA_HEADS) ** -0.5)
    return pl.pallas_call(
        functools.partial(_gla_in_kernel, qscale=qscale),
        out_shape=(jax.ShapeDtypeStruct((n, d), F32), jax.ShapeDtypeStruct((n, d), BF16),
                   jax.ShapeDtypeStruct((n, d), F32), jax.ShapeDtypeStruct((n, d), F32)),
        grid=(n // tm,),
        in_specs=[pl.BlockSpec((tm, d), row), pl.BlockSpec((1, 1, d), per_b), pl.BlockSpec((1, 1, d), per_b),
                  pl.BlockSpec((d, 3 * d), fixed), pl.BlockSpec((d, LANES), fixed),
                  pl.BlockSpec((LANES, d), fixed), pl.BlockSpec((1, d), fixed)],
        out_specs=tuple(pl.BlockSpec((tm, d), row) for _ in range(4)),
        compiler_params=_params(1), name="gla_in",
    )(h, sc, sh, w_main, w_lr, w_g, b_g)


def _gla_direction(q_ref, k_ref, v_ref, g_ref, st_scr, o_ref, reverse):
    rt, dk = q_ref.shape
    cs = GLA_CHUNK
    nch = rt // cs
    g = g_ref[...]
    pos = lax.broadcasted_iota(I32, (rt, dk), 0) % cs
    b = g
    sh = 1
    while sh < cs:
        if reverse:
            b = b + jnp.where(pos < cs - sh, pltpu.roll(b, rt - sh, 0), 0.0)
        else:
            b = b + jnp.where(pos >= sh, pltpu.roll(b, sh, 0), 0.0)
        sh *= 2
    edge = 0 if reverse else cs - 1
    b3 = b.reshape(nch, cs, dk)
    b_edge = b3[:, edge:edge + 1, :]
    q = q_ref[...]
    k = k_ref[...]
    qe = (q * jnp.exp(b)).astype(BF16)
    ke = (k * jnp.exp(-b)).astype(BF16)
    kd = (k.reshape(nch, cs, dk) * jnp.exp(b_edge - b3)).astype(BF16)
    decay = jnp.exp(b_edge)
    r_i = lax.broadcasted_iota(I32, (rt, rt), 0)
    c_i = lax.broadcasted_iota(I32, (rt, rt), 1)
    same = (r_i // cs) == (c_i // cs)
    tri = (c_i >= r_i) if reverse else (c_i <= r_i)
    att = jnp.where(same & tri, _dot_nt(qe, ke), 0.0).astype(BF16)
    v = v_ref[...]
    intra = _dot(att, v)
    st = st_scr[...]
    order = range(nch - 1, -1, -1) if reverse else range(nch)
    for j in order:
        rows = slice(j * cs, (j + 1) * cs)
        o_ref[rows, :] = intra[rows, :] + _dot_nt(qe[rows, :], st.astype(BF16))
        st = st * decay[j] + _dot_tn(v[rows, :], kd[j])
    st_scr[...] = st


def _gla_scan_kernel(qf, kf, vf, gf, qb, kb, vb, gb, s0f, s0b, of, ob, sfo, sbo, sf_scr, sb_scr):
    c = pl.program_id(2)

    @pl.when(c == 0)
    def _():
        sf_scr[...] = s0f[0, 0]
        sb_scr[...] = s0b[0, 0]

    _gla_direction(qf, kf, vf, gf, sf_scr, of, False)
    _gla_direction(qb, kb, vb, gb, sb_scr, ob, True)
    sfo[0, 0] = sf_scr[...]
    sbo[0, 0] = sb_scr[...]


def _gla_scan(qk, v, r, g, s0f, s0b, nb, seq):
    del r
    n, d = qk.shape
    nh = GLA_HEADS
    dkh, dvh = (d // 2) // nh, d // nh
    rt = min(ROW_TILE, seq)
    npb = seq // rt
    fw = lambda col: (lambda b, hd, c: (b * npb + c, col(hd)))
    bw = lambda col: (lambda b, hd, c: (b * npb + npb - 1 - c, col(hd)))
    qcol, kcol = (lambda hd: hd), (lambda hd: nh + hd)
    st = lambda b, hd, c: (b, hd, 0, 0)
    state = jax.ShapeDtypeStruct((nb, nh, dvh, dkh), F32)
    return pl.pallas_call(
        _gla_scan_kernel,
        out_shape=(jax.ShapeDtypeStruct((n, d), F32), jax.ShapeDtypeStruct((n, d), F32), state, state),
        grid=(nb, nh, npb),
        in_specs=[pl.BlockSpec((rt, dkh), fw(qcol)), pl.BlockSpec((rt, dkh), fw(kcol)),
                  pl.BlockSpec((rt, dvh), fw(qcol)), pl.BlockSpec((rt, dkh), fw(qcol)),
                  pl.BlockSpec((rt, dkh), bw(qcol)), pl.BlockSpec((rt, dkh), bw(kcol)),
                  pl.BlockSpec((rt, dvh), bw(qcol)), pl.BlockSpec((rt, dkh), bw(kcol)),
                  pl.BlockSpec((1, 1, dvh, dkh), st), pl.BlockSpec((1, 1, dvh, dkh), st)],
        out_specs=(pl.BlockSpec((rt, dvh), fw(qcol)), pl.BlockSpec((rt, dvh), bw(qcol)),
                   pl.BlockSpec((1, 1, dvh, dkh), st), pl.BlockSpec((1, 1, dvh, dkh), st)),
        scratch_shapes=[pltpu.VMEM((dvh, dkh), F32), pltpu.VMEM((dvh, dkh), F32)],
        compiler_params=_params(3), name="gla_scan",
    )(qk, qk, v, g, qk, qk, v, g, s0f, s0b)


def _hy_in_kernel(h_ref, hp_ref, hn_ref, sc_ref, sh_ref, w_ref, b_ref, cw_ref, cb_ref, o_ref, a_scr, z_scr,
                  *, tps, halo):
    i = pl.program_id(0)
    tm, d = h_ref.shape
    sc = 1.0 + sc_ref[0]
    sh = sh_ref[0]
    a_scr[0:halo, :] = (hp_ref[...] * sc + sh).astype(BF16)
    a_scr[halo:halo + tm, :] = (h_ref[...] * sc + sh).astype(BF16)
    a_scr[halo + tm:, :] = (hn_ref[...] * sc + sh).astype(BF16)
    row = lax.broadcasted_iota(I32, (tm, 1), 0)
    first = jnp.where(i % tps == 0, 0, -1)
    last = jnp.where(i % tps == tps - 1, tm - 1, -1)
    for c in range(o_ref.shape[0]):
        cols = slice(c * d, (c + 1) * d)
        z_scr[...] = _dot(a_scr[...], w_ref[:, cols]) + b_ref[:, cols]
        zp = jnp.where(row == first, 0.0, z_scr[pl.ds(halo - 1, tm), :])
        zc = z_scr[pl.ds(halo, tm), :]
        zn = jnp.where(row == last, 0.0, z_scr[pl.ds(halo + 1, tm), :])
        o_ref[c] = cw_ref[0:1, cols] * zp + cw_ref[1:2, cols] * zc + cw_ref[2:3, cols] * zn + cb_ref[:, cols]


def _hy_in(h, sc, sh, w, b, cw, cb, seq):
    n, d = h.shape
    nsplit = w.shape[1] // d
    halo = 16
    tm = min(ROW_TILE, seq)
    tps = seq // tm
    hb = tm // halo
    row = lambda i: (i, 0)
    per_b = lambda i: (i // tps, 0, 0)
    fixed = lambda i: (0, 0)
    return pl.pallas_call(
        functools.partial(_hy_in_kernel, tps=tps, halo=halo),
        out_shape=jax.ShapeDtypeStruct((nsplit, n, d), F32),
        grid=(n // tm,),
        in_specs=[pl.BlockSpec((tm, d), row),
                  pl.BlockSpec((halo, d), lambda i: (jnp.maximum(i * hb - 1, 0), 0)),
                  pl.BlockSpec((halo, d), lambda i: (jnp.minimum((i + 1) * hb, n // halo - 1), 0)),
                  pl.BlockSpec((1, 1, d), per_b), pl.BlockSpec((1, 1, d), per_b),
                  pl.BlockSpec((d, nsplit * d), fixed), pl.BlockSpec((1, nsplit * d), fixed),
                  pl.BlockSpec((cw.shape[0], nsplit * d), fixed), pl.BlockSpec((1, nsplit * d), fixed)],
        out_specs=pl.BlockSpec((nsplit, tm, d), lambda i: (0, i, 0)),
        scratch_shapes=[pltpu.VMEM((tm + 2 * halo, d), BF16), pltpu.VMEM((tm + 2 * halo, d), F32)],
        compiler_params=_params(1), name="hyena_in",
    )(h, h, h, sc, sh, w, b, cw, cb)


def _hy_filter_rows(t, seq, w1_ref, b1_ref, w2_ref, b2_ref, w3_ref, b3_ref, n_bands):
    rows = t.shape[0]
    tf = t.astype(F32)
    t_lin = tf * (1.0 / (seq - 1))
    wpos = tf * (2.0 * math.pi / seq)
    lane = lax.broadcasted_iota(I32, (rows, LANES), 1)
    jb = jnp.where(lane > n_bands, lane - n_bands - 1, lane - 1).astype(F32)
    band = 1e-4 + jb * ((n_bands - 1 - 1e-4) / (n_bands - 1))
    ang = band * wpos
    z = jnp.where(lane == 0, t_lin,
                  jnp.where(lane <= n_bands, jnp.cos(ang), jnp.where(lane <= 2 * n_bands, -jnp.sin(ang), 0.0)))
    hf = jnp.sin(_dot(z.astype(BF16), w1_ref[...]) + b1_ref[...])
    hf = jnp.sin(_dot(hf.astype(BF16), w2_ref[...]) + b2_ref[...])
    hf = jnp.sin(_dot(hf.astype(BF16), w3_ref[...]) + b3_ref[...])
    return hf.astype(BF16), t_lin


def _hy_taps_kernel(w1_ref, b1_ref, w2_ref, b2_ref, w3_ref, b3_ref, w4_ref, o_ref, *, seq, n_bands):
    i = pl.program_id(0)
    n_ord, tr, d = o_ref.shape
    mlp = (w1_ref, b1_ref, w2_ref, b2_ref, w3_ref, b3_ref)
    r0 = i * tr
    second = (r0 >= seq).astype(I32)
    n_idx = r0 + lax.broadcasted_iota(I32, (tr, 1), 0)
    t = jnp.where(second == 1, 2 * seq - n_idx, n_idx)
    hf, t_lin = _hy_filter_rows(t, seq, *mlp, n_bands)
    c_idx = lax.broadcasted_iota(I32, (1, d), 1).astype(F32)
    min_decay = math.log(HY_DECAY_TARGET) / HY_DECAY_LONG_PCT
    max_decay = math.log(HY_DECAY_TARGET) / HY_DECAY_SHORT_PCT
    delta = jnp.abs(min_decay + c_idx * ((max_decay - min_decay) / (d - 1)))
    window = jnp.where(n_idx == seq, 0.0, jnp.exp(-t_lin * delta))
    for o in range(n_ord):
        o_ref[o] = _dot(hf, w4_ref[second, o]) * window

    @pl.when(i == 0)
    def _():
        hf0, _ = _hy_filter_rows(jnp.zeros((8, 1), I32), seq, *mlp, n_bands)
        first = lax.broadcasted_iota(I32, (8, 1), 0) == 0
        for o in range(n_ord):
            o_ref[o, 0:8, :] = o_ref[o, 0:8, :] + jnp.where(first, _dot(hf0, w4_ref[1, o]), 0.0)


def _hy_taps(seq, d, w1, b1, w2, b2, w3, b3, w4):
    emb, ff = w1.shape
    n_bands = (emb - 1) // 2
    w1p = jnp.zeros((LANES, ff), F32).at[:emb].set(w1).astype(BF16)
    w4r = w4.reshape(ff, HY_ORDER, HY_DIRS, d).transpose(2, 1, 0, 3).astype(BF16)
    tr = min(ROW_TILE, seq)
    fixed = lambda i: (0, 0)
    return pl.pallas_call(
        functools.partial(_hy_taps_kernel, seq=seq, n_bands=n_bands),
        out_shape=jax.ShapeDtypeStruct((HY_ORDER, 2 * seq, d), F32),
        grid=(2 * seq // tr,),
        in_specs=[pl.BlockSpec((LANES, ff), fixed), pl.BlockSpec((1, ff), fixed),
                  pl.BlockSpec((ff, ff), fixed), pl.BlockSpec((1, ff), fixed),
                  pl.BlockSpec((ff, ff), fixed), pl.BlockSpec((1, ff), fixed),
                  pl.BlockSpec((HY_DIRS, HY_ORDER, ff, d), lambda i: (0, 0, 0, 0))],
        out_specs=pl.BlockSpec((HY_ORDER, tr, d), lambda i: (0, i, 0)),
        compiler_params=_params(1), name="hyena_taps",
    )(w1p, b1.reshape(1, ff), w2.astype(BF16), b2.reshape(1, ff), w3.astype(BF16), b3.reshape(1, ff), w4r)


def _cos_sin(num, den):
    ang = (num % den).astype(F32) * (2.0 * math.pi / den)
    return jnp.cos(ang), jnp.sin(ang)


def _fft_tables(seq, n2):
    n = 2 * seq
    n1 = n // n2
    half = seq // n2
    ar = jnp.arange(n1, dtype=I32)
    ca, sa = _cos_sin(ar[:, None] * ar[None, :], n1)
    fa_half = jnp.concatenate([ca[:, :half], sa[:, :half]], 0).astype(BF16)
    fa_full = jnp.concatenate([ca, sa], 0).astype(BF16)
    ia = (jnp.concatenate([ca[:half], sa[:half]], 0) * (1.0 / n)).astype(BF16)
    k = ar[:, None, None] + n1 * jnp.arange(n2, dtype=I32)[None, :, None]
    cb, sb = _cos_sin(k * jnp.arange(n2, dtype=I32)[None, None, :], n)
    fb = jnp.concatenate([cb, sb], 1).astype(BF16)
    gb = jnp.concatenate([cb.transpose(0, 2, 1), sb.transpose(0, 2, 1)], 1).astype(BF16)
    return fa_half, fa_full, ia, fb, gb


def _pack_c(re, im):
    hi = pltpu.bitcast(re.astype(BF16).astype(F32), U32)
    lo = pltpu.bitcast(im.astype(BF16).astype(F32), U32)
    return hi | (lo >> 16)


def _unpack_c(w):
    re = pltpu.bitcast(w & jnp.uint32(0xFFFF0000), F32)
    im = pltpu.bitcast(w << 16, F32)
    return jnp.concatenate([re, im], axis=1).astype(BF16)


def _cmul_split(p, rows, cols, conj):
    a, b, c, d = p[:rows, :cols], p[:rows, cols:], p[rows:, :cols], p[rows:, cols:]
    return (a - d, b + c) if conj else (a + d, b - c)


def _fft_a_kernel(x_ref, f_ref, o_ref, x_scr, *, cplx):
    n1 = f_ref.shape[0] // 2
    per, rows, g, d = x_ref.shape[1:]
    for b in range(per):
        xb = x_ref[0, b].reshape(rows // SUBLANES, SUBLANES, g, d)
        x_scr[b] = jnp.swapaxes(xb, 1, 2)
    for s in range(g):
        if cplx:
            z = jnp.concatenate([x_scr[b, :, s].reshape(rows, d) for b in range(per)], axis=1).astype(BF16)
            ar, ai = _cmul_split(_dot(f_ref[...], z), n1, d, False)
        else:
            p = _dot(f_ref[...], x_scr[0, :, s].reshape(rows, d).astype(BF16))
            ar, ai = p[:n1], -p[n1:]
        o_ref[0, :, s, :] = _pack_c(ar, ai)


def _fft_a(x5, which, f, cplx):
    _, nb, rows, n2, d = x5.shape
    n1 = f.shape[0] // 2
    per = 2 if cplx else 1
    groups = nb // per
    g = FFT_GROUP
    return pl.pallas_call(
        functools.partial(_fft_a_kernel, cplx=cplx),
        out_shape=jax.ShapeDtypeStruct((groups, n1, n2, d), U32),
        grid=(groups, n2 // g),
        in_specs=[pl.BlockSpec((1, per, rows, g, d), lambda p, j: (which, p, 0, j, 0)),
                  pl.BlockSpec((2 * n1, rows), lambda p, j: (0, 0))],
        out_specs=pl.BlockSpec((1, n1, g, d), lambda p, j: (p, 0, j, 0)),
        scratch_shapes=[pltpu.VMEM((per, rows // SUBLANES, g, SUBLANES, d), F32)],
        compiler_params=_params(2), name="fft_stage_a",
    )(x5, f)


def _fft_b_kernel(*refs, conv):
    if conv:
        a_ref, fb_ref, k_ref, gb_ref, o_ref = refs
    else:
        a_ref, fb_ref, o_ref = refs
    g, n2, d = a_ref.shape[1:]
    for s in range(g):
        xr, xi = _cmul_split(_dot(fb_ref[s], _unpack_c(a_ref[0, s])), n2, d, False)
        if not conv:
            o_ref[0, 0, s] = xr
            o_ref[0, 1, s] = xi
            continue
        kr, ki = k_ref[0, 0, s], k_ref[0, 1, s]
        y = jnp.concatenate([xr * kr - xi * ki, xr * ki + xi * kr], axis=1).astype(BF16)
        yr, yi = _cmul_split(_dot(gb_ref[s], y), n2, d, True)
        o_ref[0, :, s, :] = _pack_c(yr, yi)


def _fft_b(a4, fb, kspec=None, which=0, gb=None):
    groups, n1, n2, d = a4.shape
    conv = kspec is not None
    g = FFT_GROUP
    slab = pl.BlockSpec((1, g, n2, d), lambda k, p: (p, k, 0, 0))
    tab = pl.BlockSpec((g, 2 * n2, n2), lambda k, p: (k, 0, 0))
    if conv:
        in_specs = [slab, tab, pl.BlockSpec((1, 2, g, n2, d), lambda k, p: (which, 0, k, 0, 0)), tab]
        args = [a4, fb, kspec, gb]
        out_shape = jax.ShapeDtypeStruct((groups, n2, n1, d), U32)
        out_spec = pl.BlockSpec((1, n2, g, d), lambda k, p: (p, 0, k, 0))
    else:
        in_specs = [slab, tab]
        args = [a4, fb]
        out_shape = jax.ShapeDtypeStruct((groups, 2, n1, n2, d), F32)
        out_spec = pl.BlockSpec((1, 2, g, n2, d), lambda k, p: (p, 0, k, 0, 0))
    return pl.pallas_call(
        functools.partial(_fft_b_kernel, conv=conv),
        out_shape=out_shape,
        grid=(n1 // g, groups),
        in_specs=in_specs, out_specs=out_spec,
        compiler_params=_params(2), name="fft_stage_b_conv" if conv else "fft_stage_b",
    )(*args)


def _fft_c_kernel(y_ref, ia_ref, u_ref, g_ref, skip_ref, o_ref):
    half = ia_ref.shape[0] // 2
    g, _, d = y_ref.shape[1:]
    for s in range(g):
        cr, ci = _cmul_split(_dot(ia_ref[...], _unpack_c(y_ref[0, s])), half, d, True)
        o_ref[0, :, s, :] = cr
        o_ref[1, :, s, :] = ci
    skip = skip_ref[...][None]
    for b in range(2):
        o_ref[b] = g_ref[0, b] * (o_ref[b] + u_ref[0, b] * skip)


def _fft_c(y4, ia, u5, u_which, g5, g_which, skip):
    groups, n2, n1, d = y4.shape
    half = ia.shape[0] // 2
    g = FFT_GROUP
    return pl.pallas_call(
        _fft_c_kernel,
        out_shape=jax.ShapeDtypeStruct((2 * groups, half, n2, d), F32),
        grid=(groups, n2 // g),
        in_specs=[pl.BlockSpec((1, g, n1, d), lambda p, j: (p, j, 0, 0)),
                  pl.BlockSpec((2 * half, n1), lambda p, j: (0, 0)),
                  pl.BlockSpec((1, 2, half, g, d), lambda p, j: (u_which, p, 0, j, 0)),
                  pl.BlockSpec((1, 2, half, g, d), lambda p, j: (g_which, p, 0, j, 0)),
                  pl.BlockSpec((1, d), lambda p, j: (0, 0))],
        out_specs=pl.BlockSpec((2, half, g, d), lambda p, j: (p, 0, j, 0)),
        compiler_params=_params(2), name="fft_stage_c",
    )(y4, ia, u5, g5, skip.reshape(1, d))


def _hy_long_conv(zs, taps, skip, nb, seq):
    nsplit, n, d = zs.shape
    n2 = FFT_N2
    half = seq // n2
    n1 = 2 * half
    fa_half, fa_full, ia, fb, gb = _fft_tables(seq, n2)
    kspec = _fft_b(_fft_a(taps.reshape(1, HY_ORDER, n1, n2, d), 0, fa_full, False), fb)
    zs5 = zs.reshape(nsplit, nb, half, n2, d)
    u5, u_which = zs5, 0
    for o in range(HY_ORDER):
        a = _fft_a(u5, u_which, fa_half, True)
        y = _fft_b(a, fb, kspec, o, gb)
        out = _fft_c(y, ia, u5, u_which, zs5, o + 1, skip[o])
        u5, u_which = out[None], 0
    return out.reshape(n, d)


def _short_conv_kernel(u_ref, g_ref, t_ref, skip_ref, ff_ref, tf_ref, fi_ref, o_ref):
    seq, ct = u_ref.shape[-2:]
    n = 2 * seq
    z = jnp.concatenate([u_ref[0, 0], u_ref[0, 1]], axis=1).astype(BF16)
    xr, xi = _cmul_split(_dot(ff_ref[...], z), n, ct, False)
    pt = _dot(tf_ref[...], t_ref[0].astype(BF16))
    kr, ki = pt[:n], -pt[n:]
    y = jnp.concatenate([xr * kr - xi * ki, xr * ki + xi * kr], axis=1).astype(BF16)
    cr, ci = _cmul_split(_dot(fi_ref[...], y), seq, ct, True)
    skip = skip_ref[...]
    o_ref[0] = g_ref[0, 0] * (cr + u_ref[0, 0] * skip)
    o_ref[1] = g_ref[0, 1] * (ci + u_ref[0, 1] * skip)


def _hy_short_conv(zs, taps, skip, nb, seq):
    nsplit, n, d = zs.shape
    nn = 2 * seq
    ar = jnp.arange(nn, dtype=I32)
    cm, sm = _cos_sin(ar[:, None] * ar[None, :], nn)
    ff = jnp.concatenate([cm[:, :seq], sm[:, :seq]], 0).astype(BF16)
    tf = jnp.concatenate([cm, sm], 0).astype(BF16)
    fi = (jnp.concatenate([cm[:seq], sm[:seq]], 0) * (1.0 / nn)).astype(BF16)
    ct = d // 2
    zs4 = zs.reshape(nsplit, nb, seq, d)
    u4, u_which = zs4, 0
    for o in range(HY_ORDER):
        out = pl.pallas_call(
            _short_conv_kernel,
            out_shape=jax.ShapeDtypeStruct((nb, seq, d), F32),
            grid=(nb // 2, d // ct),
            in_specs=[pl.BlockSpec((1, 2, seq, ct), lambda p, j, w=u_which: (w, p, 0, j)),
                      pl.BlockSpec((1, 2, seq, ct), lambda p, j, w=o + 1: (w, p, 0, j)),
                      pl.BlockSpec((1, nn, ct), lambda p, j, w=o: (w, 0, j)),
                      pl.BlockSpec((1, ct), lambda p, j: (0, j)),
                      pl.BlockSpec((2 * nn, seq), lambda p, j: (0, 0)),
                      pl.BlockSpec((2 * nn, nn), lambda p, j: (0, 0)),
                      pl.BlockSpec((2 * seq, nn), lambda p, j: (0, 0))],
            out_specs=pl.BlockSpec((2, seq, ct), lambda p, j: (p, 0, j)),
            compiler_params=_params(2), name="hyena_short_conv",
        )(u4, zs4, taps, skip[o].reshape(1, d), ff, tf, fi)
        u4, u_which = out[None], 0
    return out.reshape(n, d)


def kernel(x, c, ctx, c_ctx, w_mod, b_mod, ln_g, ln_b, hy_w_in, hy_b_in, hy_conv_w, hy_conv_b, hy_f_w1, hy_f_b1, hy_f_w2, hy_f_b2, hy_f_w3, hy_f_b3, hy_f_w4, hy_skip, hy_w_out, hy_b_out, gla_w_in, gla_w_gate, gla_b_gate, gla_norm_g, gla_w_out, gm_w_in, gm_b_in, gm_ln_g, gm_ln_b, gm_ws, gm_bs, gm_w_out, gm_b_out, router_w, router_b, moe_w1, moe_w3, moe_w2):
    B, L, D = x.shape
    Lc = ctx.shape[1]
    depth = w_mod.shape[0]
    E = router_w.shape[1]
    alpha = (2.0 * depth) ** 0.25
    gla_layers = list(range(1, depth, N_MIXERS))
    last_ctx = gla_layers[-1] if gla_layers else -1

    cvec = jnp.zeros((8, D), F32).at[:B].set(c).at[B].set(c_ctx)
    mod = _modulation(cvec, w_mod, b_mod)
    h = _add_pos(x.reshape(B * L, D), L)
    hc = ctx.reshape(B * Lc, D)
    rw_t = router_w.T
    rw_hi = rw_t.astype(BF16)
    rhl = jnp.concatenate([rw_hi, (rw_t - rw_hi.astype(F32)).astype(BF16)], axis=0)
    rb_b = jnp.broadcast_to(router_b[:, None], (E, ROW_TILE))
    zero_cnt = jnp.zeros((E, LANES), F32)
    row2 = lambda v: v.reshape(1, -1)

    for i in range(depth):
        kind, j = i % N_MIXERS, i // N_MIXERS
        ctx_full = i < last_ctx
        ctx_any = i <= last_ctx
        lat = [mod[i, :B, k * D:(k + 1) * D].reshape(B, 1, D) for k in range(6)]
        cm = [jnp.broadcast_to(mod[i, B, k * D:(k + 1) * D].reshape(1, 1, D), (B, 1, D)) for k in range(6)]
        streams_in = [(h, lat, L)] + ([(hc, cm, Lc)] if ctx_any else [])
        pre = []
        if kind == 0:
            w_in = hy_w_in[j].astype(BF16)
            for s_h, s_m, s_len in streams_in[:1 + int(ctx_full)]:
                zs = _hy_in(s_h, s_m[1], s_m[0], w_in, row2(hy_b_in[j]), hy_conv_w[j], row2(hy_conv_b[j]), s_len)
                taps = _hy_taps(s_len, D, hy_f_w1[j], hy_f_b1[j], hy_f_w2[j], hy_f_b2[j], hy_f_w3[j], hy_f_b3[j],
                                hy_f_w4[j])
                if s_len == L:
                    y2 = _hy_long_conv(zs, taps, hy_skip[j], B, s_len)
                else:
                    y2 = _hy_short_conv(zs, taps, hy_skip[j], B, s_len)
                pre.append(("hyena", (y2,), lambda tm, d: [pl.BlockSpec((tm, d), lambda t: (t, 0))]))
            w_out, b_out = hy_w_out[j].astype(BF16), row2(hy_b_out[j])
        elif kind == 1:
            w_main, w_lr, w_g, b_g = _gla_weights(gla_w_in[j], gla_w_gate[j], gla_b_gate[j], D)
            proj = [_gla_in(s_h, s_m[1], s_m[0], w_main, w_lr, w_g, b_g, s_len) for s_h, s_m, s_len in streams_in]
            dvh = D // GLA_HEADS
            dkh = (D // 2) // GLA_HEADS
            zero_state = jnp.zeros((B, GLA_HEADS, dvh, dkh), F32)
            if ctx_any:
                ocf, ocb, s_f, s_b = _gla_scan(*proj[1], zero_state, zero_state, B, Lc)
            else:
                s_f = s_b = zero_state
            o_f, o_b, _, _ = _gla_scan(*proj[0], s_f, s_b, B, L)
            gla_specs = lambda tm, d: [pl.BlockSpec((tm, d), lambda t: (t, 0))] * 3 + [pl.BlockSpec((1, d), lambda t: (0, 0))]
            pre.append(("gla", (o_f, o_b, proj[0][2], row2(gla_norm_g[j])), gla_specs))
            if ctx_full:
                pre.append(("gla", (ocf, ocb, proj[1][2], row2(gla_norm_g[j])), gla_specs))
            w_out, b_out = gla_w_out[j].astype(BF16), jnp.zeros((1, D), F32)
        else:
            w_in = gm_w_in[j].astype(BF16)
            ws = gm_ws[j].astype(BF16)
            bs_exp = jnp.repeat(gm_bs[j].T, D // GM_HEADS, axis=1)
            gm_specs = lambda tm, d: [pl.BlockSpec((tm, d), lambda t: (t, 0))] * 2 + [
                pl.BlockSpec((GM_HEADS, GM_CHUNK, GM_CHUNK), lambda t: (0, 0, 0)), pl.BlockSpec((GM_CHUNK, d), lambda t: (0, 0))]
            for s_h, s_m, s_len in streams_in[:1 + int(ctx_full)]:
                u, vn = _gm_in(s_h, s_m[1], s_m[0], w_in, row2(gm_b_in[j]), row2(gm_ln_g[j]), row2(gm_ln_b[j]), s_len)
                pre.append(("gmlp", (u, vn, ws, bs_exp), gm_specs))
            w_out, b_out = gm_w_out[j].astype(BF16), row2(gm_b_out[j])

        moe_streams = []
        cnt = zero_cnt
        for (s_h, s_m, s_len), (pk, pargs, pspecs) in zip(streams_in, pre):
            h1, tok, meta, cnt = _post(pk, pargs, pspecs, s_h, w_out, b_out, s_m[2], row2(ln_g[i, 0]), row2(ln_b[i, 0]),
                                       s_m[4], s_m[3], rhl, rb_b, cnt, s_len, alpha)
            moe_streams.append(dict(h1=h1, tok=tok, meta=meta, g2=s_m[5], lg=row2(ln_g[i, 1]), lb=row2(ln_b[i, 1]),
                                    seq=s_len))
        outs = _moe(moe_streams, cnt, moe_w1, moe_w3, moe_w2, i, alpha)
        h = outs[0]
        if ctx_full:
            hc = outs[1]
    return h.reshape(B, L, D)
```

```python
import functools
import math

import jax
import jax.numpy as jnp
from jax import lax
from jax.experimental import pallas as pl
from jax.experimental.pallas import tpu as pltpu

F32 = jnp.float32
BF16 = jnp.bfloat16
I32 = jnp.int32
U32 = jnp.uint32
HIGHEST = lax.Precision.HIGHEST

GRID_W = 64
N_MIXERS = 3
LN_EPS = 1e-5
HY_ORDER = 2
HY_DIRS = 2
HY_DECAY_TARGET = 1e-2
HY_DECAY_SHORT_PCT = 0.3
HY_DECAY_LONG_PCT = 1.5
GLA_HEADS = 4
GLA_RANK = 16
GLA_TAU = 16.0
GLA_CHUNK = 64
GM_CHUNK = 128
GM_HEADS = 4
N_GROUPS = 4
TOP_K = 2

LANES = 128
SUBLANES = 8
V7X_VMEM_LIMIT_BYTES = 56 * 1024 * 1024
MOE_ROWS = 512
ROW_TILE = 512
DMA_TILE = 256
DMA_UNROLL = 8
FFT_N2 = 128
FFT_GROUP = SUBLANES


def _params(n_grid):
    return pltpu.CompilerParams(dimension_semantics=("arbitrary",) * n_grid,
                                vmem_limit_bytes=V7X_VMEM_LIMIT_BYTES)


def _dot(a, b):
    return jnp.dot(a, b, preferred_element_type=F32)


def _dot_nt(a, b):
    return lax.dot_general(a, b, (((1,), (1,)), ((), ())), preferred_element_type=F32)


def _dot_tn(a, b):
    return lax.dot_general(a, b, (((0,), (0,)), ((), ())), preferred_element_type=F32)


def _rows_to_tiles(x):
    rows, d = x.shape
    nsub = d // LANES
    parts = [x[:, s * LANES:(s + 1) * LANES].reshape(rows // SUBLANES, SUBLANES, LANES) for s in range(nsub)]
    return jnp.swapaxes(jnp.stack(parts, axis=1), 1, 2).reshape(rows, nsub, LANES)


def _tiles_to_rows(x):
    rows, nsub, _ = x.shape
    y = jnp.swapaxes(x.reshape(rows // SUBLANES, SUBLANES, nsub, LANES), 1, 2)
    return jnp.concatenate([y[:, s].reshape(rows, LANES) for s in range(nsub)], axis=1)


def _ln(x, g, b):
    mu = jnp.mean(x, -1, keepdims=True)
    xc = x - mu
    var = jnp.mean(xc * xc, -1, keepdims=True)
    return xc * lax.rsqrt(var + LN_EPS) * g + b


def _silu(x):
    return x * jax.nn.sigmoid(x)


def _gelu_tanh(x):
    return 0.5 * x * (1.0 + jnp.tanh(math.sqrt(2.0 / math.pi) * (x + 0.044715 * (x * x * x))))


def _mod_kernel(c_ref, w_ref, b_ref, o_ref):
    s = _silu(c_ref[...])
    o_ref[0] = _dot(s.astype(BF16), w_ref[0].astype(BF16)) + b_ref[0]


def _modulation(cvec, w_mod, b_mod):
    depth, d, n = w_mod.shape
    tn = n // 4
    return pl.pallas_call(
        _mod_kernel,
        out_shape=jax.ShapeDtypeStruct((depth, 8, n), F32),
        grid=(depth, n // tn),
        in_specs=[pl.BlockSpec((8, d), lambda i, j: (0, 0)),
                  pl.BlockSpec((1, d, tn), lambda i, j: (i, 0, j)),
                  pl.BlockSpec((1, 1, tn), lambda i, j: (i, 0, j))],
        out_specs=pl.BlockSpec((1, 8, tn), lambda i, j: (i, 0, j)),
        compiler_params=_params(2), name="modulation",
    )(cvec, w_mod, b_mod.reshape(depth, 1, n))


def _pos_table_kernel(o_ref, *, q):
    rows, cols = o_ref.shape
    p = lax.broadcasted_iota(I32, (rows, cols), 0).astype(F32)
    lane = lax.broadcasted_iota(I32, (rows, cols), 1)
    j = jnp.where(lane >= q, lane - q, lane).astype(F32)
    omega = jnp.exp(j * (-math.log(10000.0) / q))
    ang = p * omega
    o_ref[...] = jnp.where(lane >= q, jnp.cos(ang), jnp.sin(ang))


def _pos_table(n, d):
    q = d // 4
    return pl.pallas_call(functools.partial(_pos_table_kernel, q=q),
                          out_shape=jax.ShapeDtypeStruct((n, 2 * q), F32), name="pos_table")()


def _add_pos_kernel(x_ref, er_ref, ec_ref, o_ref, *, half):
    tm = x_ref.shape[0]
    reps = tm // GRID_W
    er = er_ref[...]
    er_rows = jnp.broadcast_to(er[:, None, :], (reps, GRID_W, half)).reshape(tm, half)
    ec_rows = jnp.broadcast_to(ec_ref[...][None], (reps, GRID_W, half)).reshape(tm, half)
    o_ref[:, :half] = x_ref[:, :half] + er_rows
    o_ref[:, half:] = x_ref[:, half:] + ec_rows


def _add_pos(x2d, seq):
    n, d = x2d.shape
    half = d // 2
    rows = seq // GRID_W
    er = _pos_table(rows, d)
    ec = _pos_table(GRID_W, d)
    tm = ROW_TILE
    reps = tm // GRID_W
    tps = seq // tm
    return pl.pallas_call(
        functools.partial(_add_pos_kernel, half=half),
        out_shape=jax.ShapeDtypeStruct((n, d), F32),
        grid=(n // tm,),
        in_specs=[pl.BlockSpec((tm, d), lambda i: (i, 0)),
                  pl.BlockSpec((reps, half), lambda i: (i % tps, 0)),
                  pl.BlockSpec((GRID_W, half), lambda i: (0, 0))],
        out_specs=pl.BlockSpec((tm, d), lambda i: (i, 0)),
        compiler_params=_params(1), name="add_pos",
    )(x2d, er, ec)


def _gm_in_kernel(h_ref, sc_ref, sh_ref, w_ref, b_ref, g_ref, bb_ref, u_ref, v_ref):
    d = h_ref.shape[1]
    a = (h_ref[...] * (1.0 + sc_ref[0]) + sh_ref[0]).astype(BF16)
    u_ref[...] = _gelu_tanh(_dot(a, w_ref[:, :d]) + b_ref[:, :d])
    v = _gelu_tanh(_dot(a, w_ref[:, d:]) + b_ref[:, d:])
    v_ref[...] = _ln(v, g_ref[...], bb_ref[...]).astype(BF16)


def _gm_in(h, sc, sh, w, b, g, bb, seq):
    n, d = h.shape
    tm = min(ROW_TILE, seq)
    tps = seq // tm
    row = lambda i: (i, 0)
    per_b = lambda i: (i // tps, 0, 0)
    fixed = lambda i: (0, 0)
    return pl.pallas_call(
        _gm_in_kernel,
        out_shape=(jax.ShapeDtypeStruct((n, d), F32), jax.ShapeDtypeStruct((n, d), BF16)),
        grid=(n // tm,),
        in_specs=[pl.BlockSpec((tm, d), row), pl.BlockSpec((1, 1, d), per_b), pl.BlockSpec((1, 1, d), per_b),
                  pl.BlockSpec((d, 2 * d), fixed), pl.BlockSpec((1, 2 * d), fixed),
                  pl.BlockSpec((1, d), fixed), pl.BlockSpec((1, d), fixed)],
        out_specs=(pl.BlockSpec((tm, d), row), pl.BlockSpec((tm, d), row)),
        compiler_params=_params(1), name="gmlp_in",
    )(h, sc, sh, w, b, g, bb)


def _gm_prologue(u_ref, v_ref, ws_ref, bs_ref):
    tm, d = u_ref.shape
    dh = d // GM_HEADS
    for c in range(tm // GM_CHUNK):
        rows = slice(c * GM_CHUNK, (c + 1) * GM_CHUNK)
        parts = [_dot(ws_ref[g], v_ref[rows, g * dh:(g + 1) * dh]) for g in range(GM_HEADS)]
        vm = jnp.concatenate(parts, axis=1) + bs_ref[...]
        yield rows, (u_ref[rows, :] * vm).astype(BF16)


def _gla_prologue(of_ref, ob_ref, r_ref, ng_ref):
    tm, d = of_ref.shape
    dh = d // GLA_HEADS
    o = of_ref[...] + ob_ref[...]
    parts = []
    for hd in range(GLA_HEADS):
        oh = o[:, hd * dh:(hd + 1) * dh]
        mu = jnp.mean(oh, -1, keepdims=True)
        oc = oh - mu
        var = jnp.mean(oc * oc, -1, keepdims=True)
        parts.append(oc * lax.rsqrt(var + LN_EPS) * ng_ref[:, hd * dh:(hd + 1) * dh])
    y = jnp.concatenate(parts, axis=1) * _silu(r_ref[...])
    yield slice(0, tm), y.astype(BF16)


def _hy_prologue(y_ref):
    yield slice(0, y_ref.shape[0]), y_ref[...].astype(BF16)


_PROLOGUES = {"gmlp": (_gm_prologue, 4), "gla": (_gla_prologue, 4), "hyena": (_hy_prologue, 1)}


def _route_t(scores, biased):
    n_experts, tm = scores.shape
    gsz = n_experts // N_GROUPS
    neg = jnp.float32(-jnp.inf)
    v3 = biased.reshape(N_GROUPS, gsz, tm)
    sub = lax.broadcasted_iota(I32, v3.shape, 1).astype(F32)
    m1 = jnp.max(v3, axis=1, keepdims=True)
    i1 = jnp.min(jnp.where(v3 == m1, sub, float(gsz)), axis=1, keepdims=True)
    v3b = jnp.where(sub == i1, neg, v3)
    m2 = jnp.max(v3b, axis=1, keepdims=True)
    i2 = jnp.min(jnp.where(v3b == m2, sub, float(gsz)), axis=1, keepdims=True)
    gscore = (m1 + m2).reshape(N_GROUPS, tm)
    i1 = i1.reshape(N_GROUPS, tm)
    i2 = i2.reshape(N_GROUPS, tm)
    best, e0, e1 = gscore[0:1], i1[0:1], i2[0:1]
    for g in range(1, N_GROUPS):
        better = gscore[g:g + 1] > best
        best = jnp.where(better, gscore[g:g + 1], best)
        e0 = jnp.where(better, i1[g:g + 1] + float(g * gsz), e0)
        e1 = jnp.where(better, i2[g:g + 1] + float(g * gsz), e1)
    row = lax.broadcasted_iota(I32, scores.shape, 0).astype(F32)
    oh0 = (row == e0).astype(F32)
    oh1 = (row == e1).astype(F32)
    w0 = jnp.sum(oh0 * scores, axis=0, keepdims=True)
    w1 = jnp.sum(oh1 * scores, axis=0, keepdims=True)
    den = w0 + w1
    return e0, e1, oh0, oh1, w0 / den, w1 / den


def _post_kernel(*refs, kind, alpha):
    prologue, n_pro = _PROLOGUES[kind]
    pro = refs[:n_pro]
    (h_ref, w_ref, b_ref, g1_ref, lg_ref, lb_ref, sc_ref, sh_ref, rhl_ref, rb_ref, cin_ref,
     h1_ref, tok_ref, meta_ref, cout_ref, cnt_scr) = refs[n_pro:]
    i = pl.program_id(0)
    tm = h_ref.shape[0]
    n_experts = rb_ref.shape[0]

    @pl.when(i == 0)
    def _():
        cnt_scr[...] = cin_ref[...]

    for rows, y in prologue(*pro):
        out = _dot(y, w_ref[...]) + b_ref[...]
        h1_ref[rows, :] = _ln(alpha * h_ref[rows, :] + g1_ref[0] * out, lg_ref[...], lb_ref[...])
    tok = h1_ref[...] * (1.0 + sc_ref[0]) + sh_ref[0]
    tok_ref[...] = _rows_to_tiles(tok)
    t_hi = tok.astype(BF16)
    t_lo = (tok - t_hi.astype(F32)).astype(BF16)
    p_hi = _dot_nt(rhl_ref[...], t_hi)
    logits = p_hi[:n_experts] + p_hi[n_experts:] + _dot_nt(rhl_ref[0:n_experts, :], t_lo)
    scores = jax.nn.sigmoid(logits)
    e0, e1, oh0, oh1, w0, w1 = _route_t(scores, scores + rb_ref[...])
    r_i = lax.broadcasted_iota(I32, (tm, tm), 0)
    c_i = lax.broadcasted_iota(I32, (tm, tm), 1)
    upper = (r_i < c_i).astype(BF16)
    oh = oh0 + oh1
    carry = cnt_scr[...]
    before = _dot(oh.astype(BF16), upper) + jnp.concatenate([carry] * (tm // LANES), axis=1)
    rank0 = jnp.sum(oh0 * before, axis=0, keepdims=True)
    rank1 = jnp.sum(oh1 * before, axis=0, keepdims=True)
    cnt_scr[...] = carry + jnp.sum(oh, axis=1, keepdims=True)
    zero = jnp.zeros_like(w0)
    meta_ref[0] = jnp.concatenate([e0, e1, rank0, rank1, w0, w1, zero, zero], axis=0)
    cout_ref[...] = cnt_scr[...]


def _post(kind, pro_args, pro_specs, h, w, b, g1, lg, lb, sc, sh, rhl, rb, cnt_in, seq, alpha):
    n, d = h.shape
    tm = min(ROW_TILE, seq)
    tps = seq // tm
    n_experts = rb.shape[0]
    row = lambda i: (i, 0)
    per_b = lambda i: (i // tps, 0, 0)
    fixed = lambda i: (0, 0)
    in_specs = list(pro_specs(tm, d)) + [
        pl.BlockSpec((tm, d), row), pl.BlockSpec((d, d), fixed), pl.BlockSpec((1, d), fixed),
        pl.BlockSpec((1, 1, d), per_b), pl.BlockSpec((1, d), fixed), pl.BlockSpec((1, d), fixed),
        pl.BlockSpec((1, 1, d), per_b), pl.BlockSpec((1, 1, d), per_b),
        pl.BlockSpec((2 * n_experts, d), fixed), pl.BlockSpec((n_experts, tm), fixed),
        pl.BlockSpec((n_experts, LANES), fixed)]
    return pl.pallas_call(
        functools.partial(_post_kernel, kind=kind, alpha=alpha),
        out_shape=(jax.ShapeDtypeStruct((n, d), F32), jax.ShapeDtypeStruct((n, d // LANES, LANES), F32),
                   jax.ShapeDtypeStruct((n // tm, SUBLANES, tm), F32),
                   jax.ShapeDtypeStruct((n_experts, LANES), F32)),
        grid=(n // tm,),
        in_specs=in_specs,
        out_specs=(pl.BlockSpec((tm, d), row), pl.BlockSpec((tm, d // LANES, LANES), lambda i: (i, 0, 0)),
                   pl.BlockSpec((1, SUBLANES, tm), lambda i: (i, 0, 0)), pl.BlockSpec((n_experts, LANES), fixed)),
        scratch_shapes=[pltpu.VMEM((n_experts, LANES), F32)],
        compiler_params=_params(1), name="post_" + kind,
    )(*pro_args, h, w, b, g1, lg, lb, sc, sh, rhl, rb[:, :tm], cnt_in)


def _dest_kernel(meta_ref, pstart_ref, o_ref):
    group, _, tmeta = meta_ref.shape
    n_experts = pstart_ref.shape[0]
    tm = o_ref.shape[2]
    per = tmeta // tm
    row = lax.broadcasted_iota(I32, (n_experts, tmeta), 0).astype(F32)
    for g in range(group):
        meta = meta_ref[g]
        rows = []
        for k in range(TOP_K):
            off = jnp.sum(jnp.where(row == meta[k:k + 1], pstart_ref[...], 0.0), axis=0, keepdims=True)
            rows.append(off + meta[TOP_K + k:TOP_K + k + 1])
        rows += [jnp.zeros_like(rows[0])] * (SUBLANES - TOP_K)
        dest = jnp.concatenate(rows, axis=0).astype(I32)
        for j in range(per):
            o_ref[g * per + j] = dest[:, j * tm:(j + 1) * tm]


def _dest(meta, pstart_b):
    ntile, _, tmeta = meta.shape
    tm = DMA_TILE
    per = tmeta // tm
    n_experts = pstart_b.shape[0]
    group = math.gcd(ntile, 8)
    return pl.pallas_call(
        _dest_kernel,
        out_shape=jax.ShapeDtypeStruct((ntile * per, SUBLANES, tm), I32),
        grid=(ntile // group,),
        in_specs=[pl.BlockSpec((group, SUBLANES, tmeta), lambda i: (i, 0, 0)),
                  pl.BlockSpec((n_experts, tmeta), lambda i: (0, 0))],
        out_specs=pl.BlockSpec((group * per, SUBLANES, tm), lambda i: (i, 0, 0)),
        compiler_params=_params(1), name="moe_dest",
    )(meta, pstart_b[:, :tmeta])


def _row_loop(tm, body):
    def step(it, c):
        for u in range(DMA_UNROLL):
            body(it * DMA_UNROLL + u, u)
        return c
    lax.fori_loop(0, tm // DMA_UNROLL, step, 0)


def _dispatch_kernel(lb_ref, dest_ref, tok_ref, *rest, fill):
    if fill:
        xs_ref, idx_scr, tok_scr, zero_scr, isem, lsem, rsem, zsem = rest
    else:
        _, xs_ref, idx_scr, tok_scr, zero_scr, isem, lsem, rsem, zsem = rest
    i = pl.program_id(0)
    nstep = pl.num_programs(0)
    tm = idx_scr.shape[2]
    slot = i % 2

    def load(step, to):
        return pltpu.make_async_copy(tok_ref.at[pl.ds(step * tm, tm)], tok_scr.at[to], lsem.at[to])

    @pl.when(i == 0)
    def _():
        load(0, 0).start()

    if fill:
        @pl.when(i == 0)
        def _():
            zero_scr[...] = jnp.zeros_like(zero_scr)

            def fill_copy(e):
                start_row = pl.multiple_of(jnp.maximum(lb_ref[e], 0), MOE_ROWS)
                return pltpu.make_async_copy(zero_scr, xs_ref.at[pl.ds(start_row, MOE_ROWS)], zsem)

            def start(e, c):
                @pl.when(lb_ref[e] >= 0)
                def _():
                    fill_copy(e).start()
                return c

            def wait(e, c):
                @pl.when(lb_ref[e] >= 0)
                def _():
                    fill_copy(e).wait()
                return c

            lax.fori_loop(0, lb_ref.shape[0], start, 0)
            lax.fori_loop(0, lb_ref.shape[0], wait, 0)

    for k in range(TOP_K):
        pltpu.make_async_copy(dest_ref.at[i, k], idx_scr.at[slot, k], isem).start()
    for k in range(TOP_K):
        pltpu.make_async_copy(dest_ref.at[i, k], idx_scr.at[slot, k], isem).wait()

    def row_copy(s, r, k):
        return pltpu.make_async_copy(tok_scr.at[s, r], xs_ref.at[idx_scr[s, k, r]], rsem.at[s])

    def start_row(r, u):
        for k in range(TOP_K):
            row_copy(slot, r, k).start(priority=(u + k) % 2)

    def wait_rows(s):
        def wait_row(r, u):
            for k in range(TOP_K):
                row_copy(s, r, k).wait()
        _row_loop(tm, wait_row)

    load(i, slot).wait()
    _row_loop(tm, start_row)

    @pl.when(i > 0)
    def _():
        wait_rows(1 - slot)

    @pl.when(i + 1 < nstep)
    def _():
        load(i + 1, 1 - slot).start()

    @pl.when(i == nstep - 1)
    def _():
        wait_rows(slot)


def _dispatch(last_blk, dest, tok, xs, p):
    n, ts, _ = tok.shape
    tm = DMA_TILE
    fill = xs is None
    any_spec = pl.BlockSpec(memory_space=pl.ANY)
    args = [last_blk, dest, tok] + ([] if fill else [xs])
    return pl.pallas_call(
        functools.partial(_dispatch_kernel, fill=fill),
        out_shape=jax.ShapeDtypeStruct((p, ts, LANES), F32),
        grid_spec=pltpu.PrefetchScalarGridSpec(
            num_scalar_prefetch=1, grid=(n // tm,), in_specs=[any_spec] * (len(args) - 1), out_specs=any_spec,
            scratch_shapes=[pltpu.SMEM((2, TOP_K, tm), I32), pltpu.VMEM((2, tm, ts, LANES), F32),
                            pltpu.VMEM((MOE_ROWS, ts, LANES), F32), pltpu.SemaphoreType.DMA,
                            pltpu.SemaphoreType.DMA((2,)), pltpu.SemaphoreType.DMA((2,)), pltpu.SemaphoreType.DMA]),
        input_output_aliases={} if fill else {3: 0},
        compiler_params=_params(1), name="moe_dispatch",
    )(*args)


def _expert_kernel(be_ref, nu_ref, x_ref, w1_ref, w3_ref, w2_ref, y_ref, w1_scr, w3_scr, w2_scr):
    j = pl.program_id(0)
    e = be_ref[j]
    e_prev = be_ref[jnp.maximum(j - 1, 0)]

    @pl.when(j < nu_ref[0])
    def _():
        @pl.when((j == 0) | (e != e_prev))
        def _():
            w1_scr[...] = w1_ref[0, 0].astype(BF16)
            w3_scr[...] = w3_ref[0, 0].astype(BF16)
            w2_scr[...] = w2_ref[0, 0].astype(BF16)

        x = _tiles_to_rows(x_ref[...]).astype(BF16)
        hid = _silu(_dot(x, w1_scr[...])) * _dot(x, w3_scr[...])
        y_ref[...] = _rows_to_tiles(_dot(hid.astype(BF16), w2_scr[...]))

    @pl.when(j >= nu_ref[0])
    def _():
        y_ref[...] = jnp.zeros_like(y_ref)


def _experts(blk_e, n_used, xs, w1, w3, w2, layer):
    p, ts, _ = xs.shape
    d = ts * LANES
    de = w1.shape[-1]
    nblk = p // MOE_ROWS
    wmap = lambda j, be, nu: (layer, be[j], 0, 0)
    xmap = lambda j, be, nu: (jnp.minimum(j, nu[0] - 1), 0, 0)
    return pl.pallas_call(
        _expert_kernel,
        out_shape=jax.ShapeDtypeStruct((p, ts, LANES), F32),
        grid_spec=pltpu.PrefetchScalarGridSpec(
            num_scalar_prefetch=2, grid=(nblk,),
            in_specs=[pl.BlockSpec((MOE_ROWS, ts, LANES), xmap),
                      pl.BlockSpec((1, 1, d, de), wmap), pl.BlockSpec((1, 1, d, de), wmap),
                      pl.BlockSpec((1, 1, de, d), wmap)],
            out_specs=pl.BlockSpec((MOE_ROWS, ts, LANES), lambda j, be, nu: (j, 0, 0)),
            scratch_shapes=[pltpu.VMEM((d, de), BF16), pltpu.VMEM((d, de), BF16), pltpu.VMEM((de, d), BF16)]),
        compiler_params=_params(1), name="moe_experts",
    )(blk_e, n_used, xs, w1, w3, w2)


def _combine_kernel(dest_ref, ys_ref, meta_ref, h_ref, g2_ref, lg_ref, lb_ref, o_ref,
                    idx_scr, buf_scr, isem, rsem, *, alpha):
    i = pl.program_id(0)
    nstep = pl.num_programs(0)
    tm = h_ref.shape[0]

    def gather(step, slot):
        for k in range(TOP_K):
            pltpu.make_async_copy(dest_ref.at[step, k], idx_scr.at[slot, k], isem).start()
        for k in range(TOP_K):
            pltpu.make_async_copy(dest_ref.at[step, k], idx_scr.at[slot, k], isem).wait()

        def start_row(r, u):
            for k in range(TOP_K):
                pltpu.make_async_copy(ys_ref.at[idx_scr[slot, k, r]], buf_scr.at[slot, k, r],
                                      rsem.at[slot]).start(priority=(u + k) % 2)

        _row_loop(tm, start_row)

    slot = i % 2

    @pl.when(i == 0)
    def _():
        gather(0, 0)

    @pl.when(i + 1 < nstep)
    def _():
        gather(i + 1, 1 - slot)

    def wait_row(r, u):
        for k in range(TOP_K):
            pltpu.make_async_copy(ys_ref.at[idx_scr[slot, k, r]], buf_scr.at[slot, k, r], rsem.at[slot]).wait()

    _row_loop(tm, wait_row)
    meta = meta_ref[0]
    wt = jnp.concatenate([meta, jnp.zeros((LANES - SUBLANES, tm), F32)], axis=0).T
    f = wt[:, 2 * TOP_K:2 * TOP_K + 1] * _tiles_to_rows(buf_scr[slot, 0])
    for k in range(1, TOP_K):
        f = f + wt[:, 2 * TOP_K + k:2 * TOP_K + k + 1] * _tiles_to_rows(buf_scr[slot, k])
    o_ref[...] = _ln(alpha * h_ref[...] + g2_ref[0] * f, lg_ref[...], lb_ref[...])


def _combine(dest, ys, meta, h, g2, lg, lb, seq, alpha):
    n, d = h.shape
    tm = DMA_TILE
    tps = seq // tm
    per = meta.shape[2] // tm
    row = lambda i: (i, 0)
    fixed = lambda i: (0, 0)
    return pl.pallas_call(
        functools.partial(_combine_kernel, alpha=alpha),
        out_shape=jax.ShapeDtypeStruct((n, d), F32),
        grid=(n // tm,),
        in_specs=[pl.BlockSpec(memory_space=pl.ANY), pl.BlockSpec(memory_space=pl.ANY),
                  pl.BlockSpec((1, SUBLANES, tm), lambda i: (i // per, 0, i % per)), pl.BlockSpec((tm, d), row),
                  pl.BlockSpec((1, 1, d), lambda i: (i // tps, 0, 0)),
                  pl.BlockSpec((1, d), fixed), pl.BlockSpec((1, d), fixed)],
        out_specs=pl.BlockSpec((tm, d), row),
        scratch_shapes=[pltpu.SMEM((2, TOP_K, tm), I32), pltpu.VMEM((2, TOP_K, tm, d // LANES, LANES), F32),
                        pltpu.SemaphoreType.DMA, pltpu.SemaphoreType.DMA((2,))],
        compiler_params=_params(1), name="moe_combine",
    )(dest, ys, meta, h, g2, lg, lb)


def _moe(streams, counts, w1, w3, w2, layer, alpha):
    n_experts = w1.shape[1]
    n_assign = TOP_K * sum(s["tok"].shape[0] for s in streams)
    p = n_assign + n_experts * MOE_ROWS
    nblk = p // MOE_ROWS
    cnt = counts[:, 0].astype(I32)
    padded = (cnt + MOE_ROWS - 1) // MOE_ROWS * MOE_ROWS
    pend = jnp.cumsum(padded)
    pstart = pend - padded
    n_used = (pend[-1] // MOE_ROWS).astype(I32).reshape(1)
    blk = jnp.arange(nblk, dtype=I32) * MOE_ROWS
    blk_e = jnp.sum((pend[None, :] <= blk[:, None]).astype(I32), axis=1)
    last_e = jnp.sum((pend <= jnp.maximum(pend[-1] - 1, 0)).astype(I32))
    blk_e = jnp.minimum(blk_e, last_e).astype(I32)
    last_blk = jnp.where(padded > 0, pend - MOE_ROWS, -1).astype(I32)
    pstart_b = jnp.broadcast_to(pstart.astype(F32)[:, None], (n_experts, ROW_TILE))
    dests = [_dest(s["meta"], pstart_b) for s in streams]
    xs = None
    for s, dst in zip(streams, dests):
        xs = _dispatch(last_blk, dst, s["tok"], xs, p)
    ys = _experts(blk_e, n_used, xs, w1, w3, w2, layer)
    return [_combine(dst, ys, s["meta"], s["h1"], s["g2"], s["lg"], s["lb"], s["seq"], alpha)
            for s, dst in zip(streams, dests)]


def _log_sigmoid(x):
    return jnp.minimum(x, 0.0) - jnp.log(1.0 + jnp.exp(-jnp.abs(x)))


def _gla_in_kernel(h_ref, sc_ref, sh_ref, w_ref, wl_ref, wg_ref, bg_ref, qk_ref, v_ref, r_ref, g_ref, *, qscale):
    tm, d = h_ref.shape
    a = (h_ref[...] * (1.0 + sc_ref[0]) + sh_ref[0]).astype(BF16)
    qk = _dot(a, w_ref[:, :d])
    lane = lax.broadcasted_iota(I32, (tm, d), 1)
    qk_ref[...] = jnp.where(lane < d // 2, qk * qscale, qk)
    v_ref[...] = _dot(a, w_ref[:, d:2 * d]).astype(BF16)
    r_ref[...] = _dot(a, w_ref[:, 2 * d:])
    lr = _dot(a, wl_ref[...])
    gpre = _dot(lr.astype(BF16), wg_ref[...]) + bg_ref[...]
    g_ref[...] = _log_sigmoid(gpre) * (1.0 / GLA_TAU)


def _gla_weights(w_in, w_gate, b_gate, d):
    dk = d // 2
    n_lr = HY_DIRS * GLA_RANK
    w_main = w_in[:, :2 * dk + 2 * d].astype(BF16)
    w_lr = jnp.zeros((d, LANES), F32).at[:, :n_lr].set(w_in[:, 2 * dk + 2 * d:]).astype(BF16)
    w_g = jnp.zeros((LANES, HY_DIRS * dk), F32)
    for dr in range(HY_DIRS):
        w_g = w_g.at[dr * GLA_RANK:(dr + 1) * GLA_RANK, dr * dk:(dr + 1) * dk].set(w_gate[dr])
    return w_main, w_lr, w_g.astype(BF16), b_gate.reshape(1, HY_DIRS * dk)


def _gla_in(h, sc, sh, w_main, w_lr, w_g, b_g, seq):
    n, d = h.shape
    tm = min(ROW_TILE, seq)
    tps = seq // tm
    row = lambda i: (i, 0)
    per_b = lambda i: (i // tps, 0, 0)
    fixed = lambda i: (0, 0)
    qscale = float(((d // 2) // GLA_HEADS) ** -0.5)
    return pl.pallas_call(
        functools.partial(_gla_in_kernel, qscale=qscale),
        out_shape=(jax.ShapeDtypeStruct((n, d), F32), jax.ShapeDtypeStruct((n, d), BF16),
                   jax.ShapeDtypeStruct((n, d), F32), jax.ShapeDtypeStruct((n, d), F32)),
        grid=(n // tm,),
        in_specs=[pl.BlockSpec((tm, d), row), pl.BlockSpec((1, 1, d), per_b), pl.BlockSpec((1, 1, d), per_b),
                  pl.BlockSpec((d, 3 * d), fixed), pl.BlockSpec((d, LANES), fixed),
                  pl.BlockSpec((LANES, d), fixed), pl.BlockSpec((1, d), fixed)],
        out_specs=tuple(pl.BlockSpec((tm, d), row) for _ in range(4)),
        compiler_params=_params(1), name="gla_in",
    )(h, sc, sh, w_main, w_lr, w_g, b_g)


def _gla_direction(q_ref, k_ref, v_ref, g_ref, st_scr, o_ref, reverse):
    rt, dk = q_ref.shape
    cs = GLA_CHUNK
    nch = rt // cs
    g = g_ref[...]
    pos = lax.broadcasted_iota(I32, (rt, dk), 0) % cs
    b = g
    sh = 1
    while sh < cs:
        if reverse:
            b = b + jnp.where(pos < cs - sh, pltpu.roll(b, rt - sh, 0), 0.0)
        else:
            b = b + jnp.where(pos >= sh, pltpu.roll(b, sh, 0), 0.0)
        sh *= 2
    edge = 0 if reverse else cs - 1
    b3 = b.reshape(nch, cs, dk)
    b_edge = b3[:, edge:edge + 1, :]
    q = q_ref[...]
    k = k_ref[...]
    qe = (q * jnp.exp(b)).astype(BF16)
    ke = (k * jnp.exp(-b)).astype(BF16)
    kd = (k.reshape(nch, cs, dk) * jnp.exp(b_edge - b3)).astype(BF16)
    decay = jnp.exp(b_edge)
    r_i = lax.broadcasted_iota(I32, (rt, rt), 0)
    c_i = lax.broadcasted_iota(I32, (rt, rt), 1)
    same = (r_i // cs) == (c_i // cs)
    tri = (c_i >= r_i) if reverse else (c_i <= r_i)
    att = jnp.where(same & tri, _dot_nt(qe, ke), 0.0).astype(BF16)
    v = v_ref[...]
    intra = _dot(att, v)
    st = st_scr[...]
    order = range(nch - 1, -1, -1) if reverse else range(nch)
    for j in order:
        rows = slice(j * cs, (j + 1) * cs)
        o_ref[rows, :] = intra[rows, :] + _dot_nt(qe[rows, :], st.astype(BF16))
        st = st * decay[j] + _dot_tn(v[rows, :], kd[j])
    st_scr[...] = st


def _gla_scan_kernel(qkf, vf, gf, qkb, vb, gb, s0f, s0b, of, ob, sfo, sbo, sf_scr, sb_scr):
    c = pl.program_id(1)
    nh = GLA_HEADS
    d = qkf.shape[1]
    dkh, dvh = (d // 2) // nh, d // nh

    @pl.when(c == 0)
    def _():
        sf_scr[...] = s0f[0]
        sb_scr[...] = s0b[0]

    for hd in range(nh):
        qs, ks, vs = pl.ds(hd * dkh, dkh), pl.ds(d // 2 + hd * dkh, dkh), pl.ds(hd * dvh, dvh)
        _gla_direction(qkf.at[:, qs], qkf.at[:, ks], vf.at[:, vs], gf.at[:, qs], sf_scr.at[hd], of.at[:, vs], False)
        _gla_direction(qkb.at[:, qs], qkb.at[:, ks], vb.at[:, vs], gb.at[:, ks], sb_scr.at[hd], ob.at[:, vs], True)
    sfo[0] = sf_scr[...]
    sbo[0] = sb_scr[...]


def _gla_scan(qk, v, r, g, s0f, s0b, nb, seq):
    del r
    n, d = qk.shape
    nh = GLA_HEADS
    dkh, dvh = (d // 2) // nh, d // nh
    rt = min(ROW_TILE, seq)
    npb = seq // rt
    fw = pl.BlockSpec((rt, d), lambda b, c: (b * npb + c, 0))
    bw = pl.BlockSpec((rt, d), lambda b, c: (b * npb + npb - 1 - c, 0))
    st = pl.BlockSpec((1, nh, dvh, dkh), lambda b, c: (b, 0, 0, 0))
    state = jax.ShapeDtypeStruct((nb, nh, dvh, dkh), F32)
    return pl.pallas_call(
        _gla_scan_kernel,
        out_shape=(jax.ShapeDtypeStruct((n, d), F32), jax.ShapeDtypeStruct((n, d), F32), state, state),
        grid=(nb, npb),
        in_specs=[fw, fw, fw, bw, bw, bw, st, st],
        out_specs=(fw, bw, st, st),
        scratch_shapes=[pltpu.VMEM((nh, dvh, dkh), F32), pltpu.VMEM((nh, dvh, dkh), F32)],
        compiler_params=_params(2), name="gla_scan",
    )(qk, v, g, qk, v, g, s0f, s0b)


def _hy_in_kernel(h_ref, hp_ref, hn_ref, sc_ref, sh_ref, w_ref, b_ref, cw_ref, cb_ref, o_ref, a_scr, z_scr,
                  *, tps, halo):
    i = pl.program_id(0)
    tm, d = h_ref.shape
    sc = 1.0 + sc_ref[0]
    sh = sh_ref[0]
    a_scr[0:halo, :] = (hp_ref[...] * sc + sh).astype(BF16)
    a_scr[halo:halo + tm, :] = (h_ref[...] * sc + sh).astype(BF16)
    a_scr[halo + tm:, :] = (hn_ref[...] * sc + sh).astype(BF16)
    row = lax.broadcasted_iota(I32, (tm, 1), 0)
    first = jnp.where(i % tps == 0, 0, -1)
    last = jnp.where(i % tps == tps - 1, tm - 1, -1)
    for c in range(o_ref.shape[0]):
        cols = slice(c * d, (c + 1) * d)
        z_scr[...] = _dot(a_scr[...], w_ref[:, cols]) + b_ref[:, cols]
        zp = jnp.where(row == first, 0.0, z_scr[pl.ds(halo - 1, tm), :])
        zc = z_scr[pl.ds(halo, tm), :]
        zn = jnp.where(row == last, 0.0, z_scr[pl.ds(halo + 1, tm), :])
        o_ref[c] = cw_ref[0:1, cols] * zp + cw_ref[1:2, cols] * zc + cw_ref[2:3, cols] * zn + cb_ref[:, cols]


def _hy_in(h, sc, sh, w, b, cw, cb, seq):
    n, d = h.shape
    nsplit = w.shape[1] // d
    halo = 16
    tm = min(ROW_TILE, seq)
    tps = seq // tm
    hb = tm // halo
    row = lambda i: (i, 0)
    per_b = lambda i: (i // tps, 0, 0)
    fixed = lambda i: (0, 0)
    return pl.pallas_call(
        functools.partial(_hy_in_kernel, tps=tps, halo=halo),
        out_shape=jax.ShapeDtypeStruct((nsplit, n, d), F32),
        grid=(n // tm,),
        in_specs=[pl.BlockSpec((tm, d), row),
                  pl.BlockSpec((halo, d), lambda i: (jnp.maximum(i * hb - 1, 0), 0)),
                  pl.BlockSpec((halo, d), lambda i: (jnp.minimum((i + 1) * hb, n // halo - 1), 0)),
                  pl.BlockSpec((1, 1, d), per_b), pl.BlockSpec((1, 1, d), per_b),
                  pl.BlockSpec((d, nsplit * d), fixed), pl.BlockSpec((1, nsplit * d), fixed),
                  pl.BlockSpec((cw.shape[0], nsplit * d), fixed), pl.BlockSpec((1, nsplit * d), fixed)],
        out_specs=pl.BlockSpec((nsplit, tm, d), lambda i: (0, i, 0)),
        scratch_shapes=[pltpu.VMEM((tm + 2 * halo, d), BF16), pltpu.VMEM((tm + 2 * halo, d), F32)],
        compiler_params=_params(1), name="hyena_in",
    )(h, h, h, sc, sh, w, b, cw, cb)


def _hy_filter_rows(t, seq, w1_ref, b1_ref, w2_ref, b2_ref, w3_ref, b3_ref, n_bands):
    rows = t.shape[0]
    tf = t.astype(F32)
    t_lin = tf * (1.0 / (seq - 1))
    wpos = tf * (2.0 * math.pi / seq)
    lane = lax.broadcasted_iota(I32, (rows, LANES), 1)
    jb = jnp.where(lane > n_bands, lane - n_bands - 1, lane - 1).astype(F32)
    band = 1e-4 + jb * ((n_bands - 1 - 1e-4) / (n_bands - 1))
    ang = band * wpos
    z = jnp.where(lane == 0, t_lin,
                  jnp.where(lane <= n_bands, jnp.cos(ang), jnp.where(lane <= 2 * n_bands, -jnp.sin(ang), 0.0)))
    hf = jnp.sin(_dot(z.astype(BF16), w1_ref[...]) + b1_ref[...])
    hf = jnp.sin(_dot(hf.astype(BF16), w2_ref[...]) + b2_ref[...])
    hf = jnp.sin(_dot(hf.astype(BF16), w3_ref[...]) + b3_ref[...])
    return hf.astype(BF16), t_lin


def _hy_taps_kernel(w1_ref, b1_ref, w2_ref, b2_ref, w3_ref, b3_ref, w4_ref, o_ref, *, seq, n_bands):
    i = pl.program_id(0)
    n_ord, tr, d = o_ref.shape
    mlp = (w1_ref, b1_ref, w2_ref, b2_ref, w3_ref, b3_ref)
    r0 = i * tr
    second = (r0 >= seq).astype(I32)
    n_idx = r0 + lax.broadcasted_iota(I32, (tr, 1), 0)
    t = jnp.where(second == 1, 2 * seq - n_idx, n_idx)
    hf, t_lin = _hy_filter_rows(t, seq, *mlp, n_bands)
    c_idx = lax.broadcasted_iota(I32, (1, d), 1).astype(F32)
    min_decay = math.log(HY_DECAY_TARGET) / HY_DECAY_LONG_PCT
    max_decay = math.log(HY_DECAY_TARGET) / HY_DECAY_SHORT_PCT
    delta = jnp.abs(min_decay + c_idx * ((max_decay - min_decay) / (d - 1)))
    window = jnp.where(n_idx == seq, 0.0, jnp.exp(-t_lin * delta))
    for o in range(n_ord):
        o_ref[o] = _dot(hf, w4_ref[second, o]) * window

    @pl.when(i == 0)
    def _():
        hf0, _ = _hy_filter_rows(jnp.zeros((8, 1), I32), seq, *mlp, n_bands)
        first = lax.broadcasted_iota(I32, (8, 1), 0) == 0
        for o in range(n_ord):
            o_ref[o, 0:8, :] = o_ref[o, 0:8, :] + jnp.where(first, _dot(hf0, w4_ref[1, o]), 0.0)


def _hy_taps(seq, d, w1, b1, w2, b2, w3, b3, w4):
    emb, ff = w1.shape
    n_bands = (emb - 1) // 2
    w1p = jnp.zeros((LANES, ff), F32).at[:emb].set(w1).astype(BF16)
    w4r = w4.reshape(ff, HY_ORDER, HY_DIRS, d).transpose(2, 1, 0, 3).astype(BF16)
    tr = min(ROW_TILE, seq)
    fixed = lambda i: (0, 0)
    return pl.pallas_call(
        functools.partial(_hy_taps_kernel, seq=seq, n_bands=n_bands),
        out_shape=jax.ShapeDtypeStruct((HY_ORDER, 2 * seq, d), F32),
        grid=(2 * seq // tr,),
        in_specs=[pl.BlockSpec((LANES, ff), fixed), pl.BlockSpec((1, ff), fixed),
                  pl.BlockSpec((ff, ff), fixed), pl.BlockSpec((1, ff), fixed),
                  pl.BlockSpec((ff, ff), fixed), pl.BlockSpec((1, ff), fixed),
                  pl.BlockSpec((HY_DIRS, HY_ORDER, ff, d), lambda i: (0, 0, 0, 0))],
        out_specs=pl.BlockSpec((HY_ORDER, tr, d), lambda i: (0, i, 0)),
        compiler_params=_params(1), name="hyena_taps",
    )(w1p, b1.reshape(1, ff), w2.astype(BF16), b2.reshape(1, ff), w3.astype(BF16), b3.reshape(1, ff), w4r)


def _cos_sin(num, den):
    ang = (num % den).astype(F32) * (2.0 * math.pi / den)
    return jnp.cos(ang), jnp.sin(ang)


def _fft_tables(seq, n2):
    n = 2 * seq
    n1 = n // n2
    half = seq // n2
    ar = jnp.arange(n1, dtype=I32)
    ca, sa = _cos_sin(ar[:, None] * ar[None, :], n1)
    fa_half = jnp.concatenate([ca[:, :half], sa[:, :half]], 0).astype(BF16)
    fa_full = jnp.concatenate([ca, sa], 0).astype(BF16)
    ia = (jnp.concatenate([ca[:half], sa[:half]], 0) * (1.0 / n)).astype(BF16)
    k = ar[:, None, None] + n1 * jnp.arange(n2, dtype=I32)[None, :, None]
    cb, sb = _cos_sin(k * jnp.arange(n2, dtype=I32)[None, None, :], n)
    fb = jnp.concatenate([cb, sb], 1).astype(BF16)
    gb = jnp.concatenate([cb.transpose(0, 2, 1), sb.transpose(0, 2, 1)], 1).astype(BF16)
    return fa_half, fa_full, ia, fb, gb


def _pack_c(re, im):
    hi = pltpu.bitcast(re.astype(BF16).astype(F32), U32)
    lo = pltpu.bitcast(im.astype(BF16).astype(F32), U32)
    return hi | (lo >> 16)


def _unpack_c(w):
    re = pltpu.bitcast(w & jnp.uint32(0xFFFF0000), F32)
    im = pltpu.bitcast(w << 16, F32)
    return jnp.concatenate([re, im], axis=1).astype(BF16)


def _cmul_split(p, rows, cols, conj):
    a, b, c, d = p[:rows, :cols], p[:rows, cols:], p[rows:, :cols], p[rows:, cols:]
    return (a - d, b + c) if conj else (a + d, b - c)


def _fft_a_kernel(x_ref, f_ref, o_ref, x_scr, *, cplx):
    n1 = f_ref.shape[0] // 2
    per, rows, g, d = x_ref.shape[1:]
    for b in range(per):
        xb = x_ref[0, b].reshape(rows // SUBLANES, SUBLANES, g, d)
        x_scr[b] = jnp.swapaxes(xb, 1, 2)
    for s in range(g):
        if cplx:
            z = jnp.concatenate([x_scr[b, :, s].reshape(rows, d) for b in range(per)], axis=1).astype(BF16)
            ar, ai = _cmul_split(_dot(f_ref[...], z), n1, d, False)
        else:
            p = _dot(f_ref[...], x_scr[0, :, s].reshape(rows, d).astype(BF16))
            ar, ai = p[:n1], -p[n1:]
        o_ref[0, :, s, :] = _pack_c(ar, ai)


def _fft_a(x5, which, f, cplx):
    _, nb, rows, n2, d = x5.shape
    n1 = f.shape[0] // 2
    per = 2 if cplx else 1
    groups = nb // per
    g = FFT_GROUP
    return pl.pallas_call(
        functools.partial(_fft_a_kernel, cplx=cplx),
        out_shape=jax.ShapeDtypeStruct((groups, n1, n2, d), U32),
        grid=(groups, n2 // g),
        in_specs=[pl.BlockSpec((1, per, rows, g, d), lambda p, j: (which, p, 0, j, 0)),
                  pl.BlockSpec((2 * n1, rows), lambda p, j: (0, 0))],
        out_specs=pl.BlockSpec((1, n1, g, d), lambda p, j: (p, 0, j, 0)),
        scratch_shapes=[pltpu.VMEM((per, rows // SUBLANES, g, SUBLANES, d), F32)],
        compiler_params=_params(2), name="fft_stage_a",
    )(x5, f)


def _fft_b_kernel(*refs, conv):
    if conv:
        a_ref, fb_ref, k_ref, gb_ref, o_ref = refs
    else:
        a_ref, fb_ref, o_ref = refs
    g, n2, d = a_ref.shape[1:]
    for s in range(g):
        xr, xi = _cmul_split(_dot(fb_ref[s], _unpack_c(a_ref[0, s])), n2, d, False)
        if not conv:
            o_ref[0, 0, s] = xr
            o_ref[0, 1, s] = xi
            continue
        kr, ki = k_ref[0, 0, s], k_ref[0, 1, s]
        y = jnp.concatenate([xr * kr - xi * ki, xr * ki + xi * kr], axis=1).astype(BF16)
        yr, yi = _cmul_split(_dot(gb_ref[s], y), n2, d, True)
        o_ref[0, :, s, :] = _pack_c(yr, yi)


def _fft_b(a4, fb, kspec=None, which=0, gb=None):
    groups, n1, n2, d = a4.shape
    conv = kspec is not None
    g = FFT_GROUP
    slab = pl.BlockSpec((1, g, n2, d), lambda k, p: (p, k, 0, 0))
    tab = pl.BlockSpec((g, 2 * n2, n2), lambda k, p: (k, 0, 0))
    if conv:
        in_specs = [slab, tab, pl.BlockSpec((1, 2, g, n2, d), lambda k, p: (which, 0, k, 0, 0)), tab]
        args = [a4, fb, kspec, gb]
        out_shape = jax.ShapeDtypeStruct((groups, n2, n1, d), U32)
        out_spec = pl.BlockSpec((1, n2, g, d), lambda k, p: (p, 0, k, 0))
    else:
        in_specs = [slab, tab]
        args = [a4, fb]
        out_shape = jax.ShapeDtypeStruct((groups, 2, n1, n2, d), F32)
        out_spec = pl.BlockSpec((1, 2, g, n2, d), lambda k, p: (p, 0, k, 0, 0))
    return pl.pallas_call(
        functools.partial(_fft_b_kernel, conv=conv),
        out_shape=out_shape,
        grid=(n1 // g, groups),
        in_specs=in_specs, out_specs=out_spec,
        compiler_params=_params(2), name="fft_stage_b_conv" if conv else "fft_stage_b",
    )(*args)


def _fft_c_kernel(y_ref, ia_ref, u_ref, g_ref, skip_ref, o_ref):
    half = ia_ref.shape[0] // 2
    g, _, d = y_ref.shape[1:]
    for s in range(g):
        cr, ci = _cmul_split(_dot(ia_ref[...], _unpack_c(y_ref[0, s])), half, d, True)
        o_ref[0, :, s, :] = cr
        o_ref[1, :, s, :] = ci
    skip = skip_ref[...][None]
    for b in range(2):
        o_ref[b] = g_ref[0, b] * (o_ref[b] + u_ref[0, b] * skip)


def _fft_c(y4, ia, u5, u_which, g5, g_which, skip):
    groups, n2, n1, d = y4.shape
    half = ia.shape[0] // 2
    g = FFT_GROUP
    return pl.pallas_call(
        _fft_c_kernel,
        out_shape=jax.ShapeDtypeStruct((2 * groups, half, n2, d), F32),
        grid=(groups, n2 // g),
        in_specs=[pl.BlockSpec((1, g, n1, d), lambda p, j: (p, j, 0, 0)),
                  pl.BlockSpec((2 * half, n1), lambda p, j: (0, 0)),
                  pl.BlockSpec((1, 2, half, g, d), lambda p, j: (u_which, p, 0, j, 0)),
                  pl.BlockSpec((1, 2, half, g, d), lambda p, j: (g_which, p, 0, j, 0)),
                  pl.BlockSpec((1, d), lambda p, j: (0, 0))],
        out_specs=pl.BlockSpec((2, half, g, d), lambda p, j: (p, 0, j, 0)),
        compiler_params=_params(2), name="fft_stage_c",
    )(y4, ia, u5, g5, skip.reshape(1, d))


def _hy_long_conv(zs, taps, skip, nb, seq):
    nsplit, n, d = zs.shape
    n2 = FFT_N2
    half = seq // n2
    n1 = 2 * half
    fa_half, fa_full, ia, fb, gb = _fft_tables(seq, n2)
    kspec = _fft_b(_fft_a(taps.reshape(1, HY_ORDER, n1, n2, d), 0, fa_full, False), fb)
    zs5 = zs.reshape(nsplit, nb, half, n2, d)
    u5, u_which = zs5, 0
    for o in range(HY_ORDER):
        a = _fft_a(u5, u_which, fa_half, True)
        y = _fft_b(a, fb, kspec, o, gb)
        out = _fft_c(y, ia, u5, u_which, zs5, o + 1, skip[o])
        u5, u_which = out[None], 0
    return out.reshape(n, d)


def _short_conv_kernel(u_ref, g_ref, t_ref, skip_ref, ff_ref, tf_ref, fi_ref, o_ref):
    seq, ct = u_ref.shape[-2:]
    n = 2 * seq
    z = jnp.concatenate([u_ref[0, 0], u_ref[0, 1]], axis=1).astype(BF16)
    xr, xi = _cmul_split(_dot(ff_ref[...], z), n, ct, False)
    pt = _dot(tf_ref[...], t_ref[0].astype(BF16))
    kr, ki = pt[:n], -pt[n:]
    y = jnp.concatenate([xr * kr - xi * ki, xr * ki + xi * kr], axis=1).astype(BF16)
    cr, ci = _cmul_split(_dot(fi_ref[...], y), seq, ct, True)
    skip = skip_ref[...]
    o_ref[0] = g_ref[0, 0] * (cr + u_ref[0, 0] * skip)
    o_ref[1] = g_ref[0, 1] * (ci + u_ref[0, 1] * skip)


def _hy_short_conv(zs, taps, skip, nb, seq):
    nsplit, n, d = zs.shape
    nn = 2 * seq
    ar = jnp.arange(nn, dtype=I32)
    cm, sm = _cos_sin(ar[:, None] * ar[None, :], nn)
    ff = jnp.concatenate([cm[:, :seq], sm[:, :seq]], 0).astype(BF16)
    tf = jnp.concatenate([cm, sm], 0).astype(BF16)
    fi = (jnp.concatenate([cm[:seq], sm[:seq]], 0) * (1.0 / nn)).astype(BF16)
    ct = d // 2
    zs4 = zs.reshape(nsplit, nb, seq, d)
    u4, u_which = zs4, 0
    for o in range(HY_ORDER):
        out = pl.pallas_call(
            _short_conv_kernel,
            out_shape=jax.ShapeDtypeStruct((nb, seq, d), F32),
            grid=(nb // 2, d // ct),
            in_specs=[pl.BlockSpec((1, 2, seq, ct), lambda p, j, w=u_which: (w, p, 0, j)),
                      pl.BlockSpec((1, 2, seq, ct), lambda p, j, w=o + 1: (w, p, 0, j)),
                      pl.BlockSpec((1, nn, ct), lambda p, j, w=o: (w, 0, j)),
                      pl.BlockSpec((1, ct), lambda p, j: (0, j)),
                      pl.BlockSpec((2 * nn, seq), lambda p, j: (0, 0)),
                      pl.BlockSpec((2 * nn, nn), lambda p, j: (0, 0)),
                      pl.BlockSpec((2 * seq, nn), lambda p, j: (0, 0))],
            out_specs=pl.BlockSpec((2, seq, ct), lambda p, j: (p, 0, j)),
            compiler_params=_params(2), name="hyena_short_conv",
        )(u4, zs4, taps, skip[o].reshape(1, d), ff, tf, fi)
        u4, u_which = out[None], 0
    return out.reshape(n, d)


def kernel(x, c, ctx, c_ctx, w_mod, b_mod, ln_g, ln_b, hy_w_in, hy_b_in, hy_conv_w, hy_conv_b, hy_f_w1, hy_f_b1, hy_f_w2, hy_f_b2, hy_f_w3, hy_f_b3, hy_f_w4, hy_skip, hy_w_out, hy_b_out, gla_w_in, gla_w_gate, gla_b_gate, gla_norm_g, gla_w_out, gm_w_in, gm_b_in, gm_ln_g, gm_ln_b, gm_ws, gm_bs, gm_w_out, gm_b_out, router_w, router_b, moe_w1, moe_w3, moe_w2):
    B, L, D = x.shape
    Lc = ctx.shape[1]
    depth = w_mod.shape[0]
    E = router_w.shape[1]
    alpha = (2.0 * depth) ** 0.25
    gla_layers = list(range(1, depth, N_MIXERS))
    last_ctx = gla_layers[-1] if gla_layers else -1

    cvec = jnp.zeros((8, D), F32).at[:B].set(c).at[B].set(c_ctx)
    mod = _modulation(cvec, w_mod, b_mod)
    h = _add_pos(x.reshape(B * L, D), L)
    hc = ctx.reshape(B * Lc, D)
    rw_t = router_w.T
    rw_hi = rw_t.astype(BF16)
    rhl = jnp.concatenate([rw_hi, (rw_t - rw_hi.astype(F32)).astype(BF16)], axis=0)
    rb_b = jnp.broadcast_to(router_b[:, None], (E, ROW_TILE))
    zero_cnt = jnp.zeros((E, LANES), F32)
    row2 = lambda v: v.reshape(1, -1)

    for i in range(depth):
        kind, j = i % N_MIXERS, i // N_MIXERS
        ctx_full = i < last_ctx
        ctx_any = i <= last_ctx
        lat = [mod[i, :B, k * D:(k + 1) * D].reshape(B, 1, D) for k in range(6)]
        cm = [jnp.broadcast_to(mod[i, B, k * D:(k + 1) * D].reshape(1, 1, D), (B, 1, D)) for k in range(6)]
        streams_in = [(h, lat, L)] + ([(hc, cm, Lc)] if ctx_any else [])
        pre = []
        if kind == 0:
            w_in = hy_w_in[j].astype(BF16)
            for s_h, s_m, s_len in streams_in[:1 + int(ctx_full)]:
                zs = _hy_in(s_h, s_m[1], s_m[0], w_in, row2(hy_b_in[j]), hy_conv_w[j], row2(hy_conv_b[j]), s_len)
                taps = _hy_taps(s_len, D, hy_f_w1[j], hy_f_b1[j], hy_f_w2[j], hy_f_b2[j], hy_f_w3[j], hy_f_b3[j],
                                hy_f_w4[j])
                if s_len == L:
                    y2 = _hy_long_conv(zs, taps, hy_skip[j], B, s_len)
                else:
                    y2 = _hy_short_conv(zs, taps, hy_skip[j], B, s_len)
                pre.append(("hyena", (y2,), lambda tm, d: [pl.BlockSpec((tm, d), lambda t: (t, 0))]))
            w_out, b_out = hy_w_out[j].astype(BF16), row2(hy_b_out[j])
        elif kind == 1:
            w_main, w_lr, w_g, b_g = _gla_weights(gla_w_in[j], gla_w_gate[j], gla_b_gate[j], D)
            proj = [_gla_in(s_h, s_m[1], s_m[0], w_main, w_lr, w_g, b_g, s_len) for s_h, s_m, s_len in streams_in]
            dvh = D // GLA_HEADS
            dkh = (D // 2) // GLA_HEADS
            zero_state = jnp.zeros((B, GLA_HEADS, dvh, dkh), F32)
            if ctx_any:
                ocf, ocb, s_f, s_b = _gla_scan(*proj[1], zero_state, zero_state, B, Lc)
            else:
                s_f = s_b = zero_state
            o_f, o_b, _, _ = _gla_scan(*proj[0], s_f, s_b, B, L)
            gla_specs = lambda tm, d: [pl.BlockSpec((tm, d), lambda t: (t, 0))] * 3 + [pl.BlockSpec((1, d), lambda t: (0, 0))]
            pre.append(("gla", (o_f, o_b, proj[0][2], row2(gla_norm_g[j])), gla_specs))
            if ctx_full:
                pre.append(("gla", (ocf, ocb, proj[1][2], row2(gla_norm_g[j])), gla_specs))
            w_out, b_out = gla_w_out[j].astype(BF16), jnp.zeros((1, D), F32)
        else:
            w_in = gm_w_in[j].astype(BF16)
            ws = gm_ws[j].astype(BF16)
            bs_exp = jnp.repeat(gm_bs[j].T, D // GM_HEADS, axis=1)
            gm_specs = lambda tm, d: [pl.BlockSpec((tm, d), lambda t: (t, 0))] * 2 + [
                pl.BlockSpec((GM_HEADS, GM_CHUNK, GM_CHUNK), lambda t: (0, 0, 0)), pl.BlockSpec((GM_CHUNK, d), lambda t: (0, 0))]
            for s_h, s_m, s_len in streams_in[:1 + int(ctx_full)]:
                u, vn = _gm_in(s_h, s_m[1], s_m[0], w_in, row2(gm_b_in[j]), row2(gm_ln_g[j]), row2(gm_ln_b[j]), s_len)
                pre.append(("gmlp", (u, vn, ws, bs_exp), gm_specs))
            w_out, b_out = gm_w_out[j].astype(BF16), row2(gm_b_out[j])

        moe_streams = []
        cnt = zero_cnt
        for (s_h, s_m, s_len), (pk, pargs, pspecs) in zip(streams_in, pre):
            h1, tok, meta, cnt = _post(pk, pargs, pspecs, s_h, w_out, b_out, s_m[2], row2(ln_g[i, 0]), row2(ln_b[i, 0]),
                                       s_m[4], s_m[3], rhl, rb_b, cnt, s_len, alpha)
            moe_streams.append(dict(h1=h1, tok=tok, meta=meta, g2=s_m[5], lg=row2(ln_g[i, 1]), lb=row2(ln_b[i, 1]),
                                    seq=s_len))
        outs = _moe(moe_streams, cnt, moe_w1, moe_w3, moe_w2, i, alpha)
        h = outs[0]
        if ctx_full:
            hc = outs[1]
    return h.reshape(B, L, D)
```

```python
import functools
import math

import jax
import jax.numpy as jnp
from jax import lax
from jax.experimental import pallas as pl
from jax.experimental.pallas import tpu as pltpu

F32 = jnp.float32
BF16 = jnp.bfloat16
I32 = jnp.int32
U32 = jnp.uint32
HIGHEST = lax.Precision.HIGHEST

GRID_W = 64
N_MIXERS = 3
LN_EPS = 1e-5
HY_ORDER = 2
HY_DIRS = 2
HY_DECAY_TARGET = 1e-2
HY_DECAY_SHORT_PCT = 0.3
HY_DECAY_LONG_PCT = 1.5
GLA_HEADS = 4
GLA_RANK = 16
GLA_TAU = 16.0
GLA_CHUNK = 64
GM_CHUNK = 128
GM_HEADS = 4
N_GROUPS = 4
TOP_K = 2

LANES = 128
SUBLANES = 8
V7X_VMEM_LIMIT_BYTES = 56 * 1024 * 1024
MOE_ROWS = 512
ROW_TILE = 512
DMA_TILE = 256
DMA_UNROLL = 8
COMBINE_SLOTS = 3
FFT_N2 = 128
FFT_GROUP = SUBLANES


def _params(n_grid):
    return pltpu.CompilerParams(dimension_semantics=("arbitrary",) * n_grid,
                                vmem_limit_bytes=V7X_VMEM_LIMIT_BYTES)


def _dot(a, b):
    return jnp.dot(a, b, preferred_element_type=F32)


def _dot_nt(a, b):
    return lax.dot_general(a, b, (((1,), (1,)), ((), ())), preferred_element_type=F32)


def _dot_tn(a, b):
    return lax.dot_general(a, b, (((0,), (0,)), ((), ())), preferred_element_type=F32)


def _rows_to_tiles(x):
    rows, d = x.shape
    nsub = d // LANES
    parts = [x[:, s * LANES:(s + 1) * LANES].reshape(rows // SUBLANES, SUBLANES, LANES) for s in range(nsub)]
    return jnp.swapaxes(jnp.stack(parts, axis=1), 1, 2).reshape(rows, nsub, LANES)


def _tiles_to_rows(x):
    rows, nsub, _ = x.shape
    y = jnp.swapaxes(x.reshape(rows // SUBLANES, SUBLANES, nsub, LANES), 1, 2)
    return jnp.concatenate([y[:, s].reshape(rows, LANES) for s in range(nsub)], axis=1)


def _ln(x, g, b):
    mu = jnp.mean(x, -1, keepdims=True)
    xc = x - mu
    var = jnp.mean(xc * xc, -1, keepdims=True)
    return xc * lax.rsqrt(var + LN_EPS) * g + b


def _silu(x):
    return x * jax.nn.sigmoid(x)


def _gelu_tanh(x):
    return 0.5 * x * (1.0 + jnp.tanh(math.sqrt(2.0 / math.pi) * (x + 0.044715 * (x * x * x))))


def _mod_kernel(c_ref, w_ref, b_ref, o_ref):
    s = _silu(c_ref[...])
    o_ref[0] = _dot(s.astype(BF16), w_ref[0].astype(BF16)) + b_ref[0]


def _modulation(cvec, w_mod, b_mod):
    depth, d, n = w_mod.shape
    tn = n // 4
    return pl.pallas_call(
        _mod_kernel,
        out_shape=jax.ShapeDtypeStruct((depth, 8, n), F32),
        grid=(depth, n // tn),
        in_specs=[pl.BlockSpec((8, d), lambda i, j: (0, 0)),
                  pl.BlockSpec((1, d, tn), lambda i, j: (i, 0, j)),
                  pl.BlockSpec((1, 1, tn), lambda i, j: (i, 0, j))],
        out_specs=pl.BlockSpec((1, 8, tn), lambda i, j: (i, 0, j)),
        compiler_params=_params(2), name="modulation",
    )(cvec, w_mod, b_mod.reshape(depth, 1, n))


def _pos_table_kernel(o_ref, *, q):
    rows, cols = o_ref.shape
    p = lax.broadcasted_iota(I32, (rows, cols), 0).astype(F32)
    lane = lax.broadcasted_iota(I32, (rows, cols), 1)
    j = jnp.where(lane >= q, lane - q, lane).astype(F32)
    omega = jnp.exp(j * (-math.log(10000.0) / q))
    ang = p * omega
    o_ref[...] = jnp.where(lane >= q, jnp.cos(ang), jnp.sin(ang))


def _pos_table(n, d):
    q = d // 4
    return pl.pallas_call(functools.partial(_pos_table_kernel, q=q),
                          out_shape=jax.ShapeDtypeStruct((n, 2 * q), F32), name="pos_table")()


def _add_pos_kernel(x_ref, er_ref, ec_ref, o_ref, *, half):
    tm = x_ref.shape[0]
    reps = tm // GRID_W
    er = er_ref[...]
    er_rows = jnp.broadcast_to(er[:, None, :], (reps, GRID_W, half)).reshape(tm, half)
    ec_rows = jnp.broadcast_to(ec_ref[...][None], (reps, GRID_W, half)).reshape(tm, half)
    o_ref[:, :half] = x_ref[:, :half] + er_rows
    o_ref[:, half:] = x_ref[:, half:] + ec_rows


def _add_pos(x2d, seq):
    n, d = x2d.shape
    half = d // 2
    rows = seq // GRID_W
    er = _pos_table(rows, d)
    ec = _pos_table(GRID_W, d)
    tm = ROW_TILE
    reps = tm // GRID_W
    tps = seq // tm
    return pl.pallas_call(
        functools.partial(_add_pos_kernel, half=half),
        out_shape=jax.ShapeDtypeStruct((n, d), F32),
        grid=(n // tm,),
        in_specs=[pl.BlockSpec((tm, d), lambda i: (i, 0)),
                  pl.BlockSpec((reps, half), lambda i: (i % tps, 0)),
                  pl.BlockSpec((GRID_W, half), lambda i: (0, 0))],
        out_specs=pl.BlockSpec((tm, d), lambda i: (i, 0)),
        compiler_params=_params(1), name="add_pos",
    )(x2d, er, ec)


def _gm_in_kernel(h_ref, sc_ref, sh_ref, w_ref, b_ref, g_ref, bb_ref, u_ref, v_ref):
    d = h_ref.shape[1]
    a = (h_ref[...] * (1.0 + sc_ref[0]) + sh_ref[0]).astype(BF16)
    u_ref[...] = _gelu_tanh(_dot(a, w_ref[:, :d]) + b_ref[:, :d])
    v = _gelu_tanh(_dot(a, w_ref[:, d:]) + b_ref[:, d:])
    v_ref[...] = _ln(v, g_ref[...], bb_ref[...]).astype(BF16)


def _gm_in(h, sc, sh, w, b, g, bb, seq):
    n, d = h.shape
    tm = min(ROW_TILE, seq)
    tps = seq // tm
    row = lambda i: (i, 0)
    per_b = lambda i: (i // tps, 0, 0)
    fixed = lambda i: (0, 0)
    return pl.pallas_call(
        _gm_in_kernel,
        out_shape=(jax.ShapeDtypeStruct((n, d), F32), jax.ShapeDtypeStruct((n, d), BF16)),
        grid=(n // tm,),
        in_specs=[pl.BlockSpec((tm, d), row), pl.BlockSpec((1, 1, d), per_b), pl.BlockSpec((1, 1, d), per_b),
                  pl.BlockSpec((d, 2 * d), fixed), pl.BlockSpec((1, 2 * d), fixed),
                  pl.BlockSpec((1, d), fixed), pl.BlockSpec((1, d), fixed)],
        out_specs=(pl.BlockSpec((tm, d), row), pl.BlockSpec((tm, d), row)),
        compiler_params=_params(1), name="gmlp_in",
    )(h, sc, sh, w, b, g, bb)


def _gm_prologue(u_ref, v_ref, ws_ref, bs_ref):
    tm, d = u_ref.shape
    dh = d // GM_HEADS
    for c in range(tm // GM_CHUNK):
        rows = slice(c * GM_CHUNK, (c + 1) * GM_CHUNK)
        parts = [_dot(ws_ref[g], v_ref[rows, g * dh:(g + 1) * dh]) for g in range(GM_HEADS)]
        vm = jnp.concatenate(parts, axis=1) + bs_ref[...]
        yield rows, (u_ref[rows, :] * vm).astype(BF16)


def _gla_prologue(of_ref, ob_ref, r_ref, ng_ref):
    tm, d = of_ref.shape
    dh = d // GLA_HEADS
    o = of_ref[...] + ob_ref[...]
    parts = []
    for hd in range(GLA_HEADS):
        oh = o[:, hd * dh:(hd + 1) * dh]
        mu = jnp.mean(oh, -1, keepdims=True)
        oc = oh - mu
        var = jnp.mean(oc * oc, -1, keepdims=True)
        parts.append(oc * lax.rsqrt(var + LN_EPS) * ng_ref[:, hd * dh:(hd + 1) * dh])
    y = jnp.concatenate(parts, axis=1) * _silu(r_ref[...])
    yield slice(0, tm), y.astype(BF16)


def _hy_prologue(y_ref):
    yield slice(0, y_ref.shape[0]), y_ref[...].astype(BF16)


_PROLOGUES = {"gmlp": (_gm_prologue, 4), "gla": (_gla_prologue, 4), "hyena": (_hy_prologue, 1)}


def _route_t(scores, biased):
    n_experts, tm = scores.shape
    gsz = n_experts // N_GROUPS
    neg = jnp.float32(-jnp.inf)
    v3 = biased.reshape(N_GROUPS, gsz, tm)
    sub = lax.broadcasted_iota(I32, v3.shape, 1).astype(F32)
    m1 = jnp.max(v3, axis=1, keepdims=True)
    i1 = jnp.min(jnp.where(v3 == m1, sub, float(gsz)), axis=1, keepdims=True)
    v3b = jnp.where(sub == i1, neg, v3)
    m2 = jnp.max(v3b, axis=1, keepdims=True)
    i2 = jnp.min(jnp.where(v3b == m2, sub, float(gsz)), axis=1, keepdims=True)
    gscore = (m1 + m2).reshape(N_GROUPS, tm)
    i1 = i1.reshape(N_GROUPS, tm)
    i2 = i2.reshape(N_GROUPS, tm)
    best, e0, e1 = gscore[0:1], i1[0:1], i2[0:1]
    for g in range(1, N_GROUPS):
        better = gscore[g:g + 1] > best
        best = jnp.where(better, gscore[g:g + 1], best)
        e0 = jnp.where(better, i1[g:g + 1] + float(g * gsz), e0)
        e1 = jnp.where(better, i2[g:g + 1] + float(g * gsz), e1)
    row = lax.broadcasted_iota(I32, scores.shape, 0).astype(F32)
    oh0 = (row == e0).astype(F32)
    oh1 = (row == e1).astype(F32)
    w0 = jnp.sum(oh0 * scores, axis=0, keepdims=True)
    w1 = jnp.sum(oh1 * scores, axis=0, keepdims=True)
    den = w0 + w1
    return e0, e1, oh0, oh1, w0 / den, w1 / den


def _post_kernel(*refs, kind, alpha):
    prologue, n_pro = _PROLOGUES[kind]
    pro = refs[:n_pro]
    (h_ref, w_ref, b_ref, g1_ref, lg_ref, lb_ref, sc_ref, sh_ref, rhl_ref, rb_ref, cin_ref,
     h1_ref, tok_ref, meta_ref, cout_ref, cnt_scr) = refs[n_pro:]
    i = pl.program_id(0)
    tm = h_ref.shape[0]
    n_experts = rb_ref.shape[0]

    @pl.when(i == 0)
    def _():
        cnt_scr[...] = cin_ref[...]

    for rows, y in prologue(*pro):
        out = _dot(y, w_ref[...]) + b_ref[...]
        h1_ref[rows, :] = _ln(alpha * h_ref[rows, :] + g1_ref[0] * out, lg_ref[...], lb_ref[...])
    tok = h1_ref[...] * (1.0 + sc_ref[0]) + sh_ref[0]
    tok_ref[...] = _rows_to_tiles(tok)
    t_hi = tok.astype(BF16)
    t_lo = (tok - t_hi.astype(F32)).astype(BF16)
    p_hi = _dot_nt(rhl_ref[...], t_hi)
    logits = p_hi[:n_experts] + p_hi[n_experts:] + _dot_nt(rhl_ref[0:n_experts, :], t_lo)
    scores = jax.nn.sigmoid(logits)
    e0, e1, oh0, oh1, w0, w1 = _route_t(scores, scores + rb_ref[...])
    r_i = lax.broadcasted_iota(I32, (tm, tm), 0)
    c_i = lax.broadcasted_iota(I32, (tm, tm), 1)
    upper = (r_i < c_i).astype(BF16)
    oh = oh0 + oh1
    carry = cnt_scr[...]
    before = _dot(oh.astype(BF16), upper) + jnp.concatenate([carry] * (tm // LANES), axis=1)
    rank0 = jnp.sum(oh0 * before, axis=0, keepdims=True)
    rank1 = jnp.sum(oh1 * before, axis=0, keepdims=True)
    cnt_scr[...] = carry + jnp.sum(oh, axis=1, keepdims=True)
    zero = jnp.zeros_like(w0)
    meta_ref[0] = jnp.concatenate([e0, e1, rank0, rank1, w0, w1, zero, zero], axis=0)
    cout_ref[...] = cnt_scr[...]


def _post(kind, pro_args, pro_specs, h, w, b, g1, lg, lb, sc, sh, rhl, rb, cnt_in, seq, alpha):
    n, d = h.shape
    tm = min(ROW_TILE, seq)
    tps = seq // tm
    n_experts = rb.shape[0]
    row = lambda i: (i, 0)
    per_b = lambda i: (i // tps, 0, 0)
    fixed = lambda i: (0, 0)
    in_specs = list(pro_specs(tm, d)) + [
        pl.BlockSpec((tm, d), row), pl.BlockSpec((d, d), fixed), pl.BlockSpec((1, d), fixed),
        pl.BlockSpec((1, 1, d), per_b), pl.BlockSpec((1, d), fixed), pl.BlockSpec((1, d), fixed),
        pl.BlockSpec((1, 1, d), per_b), pl.BlockSpec((1, 1, d), per_b),
        pl.BlockSpec((2 * n_experts, d), fixed), pl.BlockSpec((n_experts, tm), fixed),
        pl.BlockSpec((n_experts, LANES), fixed)]
    return pl.pallas_call(
        functools.partial(_post_kernel, kind=kind, alpha=alpha),
        out_shape=(jax.ShapeDtypeStruct((n, d), F32), jax.ShapeDtypeStruct((n, d // LANES, LANES), F32),
                   jax.ShapeDtypeStruct((n // tm, SUBLANES, tm), F32),
                   jax.ShapeDtypeStruct((n_experts, LANES), F32)),
        grid=(n // tm,),
        in_specs=in_specs,
        out_specs=(pl.BlockSpec((tm, d), row), pl.BlockSpec((tm, d // LANES, LANES), lambda i: (i, 0, 0)),
                   pl.BlockSpec((1, SUBLANES, tm), lambda i: (i, 0, 0)), pl.BlockSpec((n_experts, LANES), fixed)),
        scratch_shapes=[pltpu.VMEM((n_experts, LANES), F32)],
        compiler_params=_params(1), name="post_" + kind,
    )(*pro_args, h, w, b, g1, lg, lb, sc, sh, rhl, rb[:, :tm], cnt_in)


def _dest_kernel(meta_ref, pstart_ref, o_ref):
    group, _, tmeta = meta_ref.shape
    n_experts = pstart_ref.shape[0]
    tm = o_ref.shape[2]
    per = tmeta // tm
    row = lax.broadcasted_iota(I32, (n_experts, tmeta), 0).astype(F32)
    for g in range(group):
        meta = meta_ref[g]
        rows = []
        for k in range(TOP_K):
            off = jnp.sum(jnp.where(row == meta[k:k + 1], pstart_ref[...], 0.0), axis=0, keepdims=True)
            rows.append(off + meta[TOP_K + k:TOP_K + k + 1])
        rows += [jnp.zeros_like(rows[0])] * (SUBLANES - TOP_K)
        dest = jnp.concatenate(rows, axis=0).astype(I32)
        for j in range(per):
            o_ref[g * per + j] = dest[:, j * tm:(j + 1) * tm]


def _dest(meta, pstart_b):
    ntile, _, tmeta = meta.shape
    tm = DMA_TILE
    per = tmeta // tm
    n_experts = pstart_b.shape[0]
    group = math.gcd(ntile, 8)
    return pl.pallas_call(
        _dest_kernel,
        out_shape=jax.ShapeDtypeStruct((ntile * per, SUBLANES, tm), I32),
        grid=(ntile // group,),
        in_specs=[pl.BlockSpec((group, SUBLANES, tmeta), lambda i: (i, 0, 0)),
                  pl.BlockSpec((n_experts, tmeta), lambda i: (0, 0))],
        out_specs=pl.BlockSpec((group * per, SUBLANES, tm), lambda i: (i, 0, 0)),
        compiler_params=_params(1), name="moe_dest",
    )(meta, pstart_b[:, :tmeta])


def _row_loop(tm, body):
    def step(it, c):
        for u in range(DMA_UNROLL):
            body(it * DMA_UNROLL + u, u)
        return c
    lax.fori_loop(0, tm // DMA_UNROLL, step, 0)


def _dispatch_kernel(lb_ref, dest_ref, tok_ref, *rest, fill):
    if fill:
        xs_ref, idx_scr, tok_scr, zero_scr, isem, lsem, rsem, zsem = rest
    else:
        _, xs_ref, idx_scr, tok_scr, zero_scr, isem, lsem, rsem, zsem = rest
    i = pl.program_id(0)
    nstep = pl.num_programs(0)
    tm = idx_scr.shape[2]
    slot = i % 2

    def load(step, to):
        return pltpu.make_async_copy(tok_ref.at[pl.ds(step * tm, tm)], tok_scr.at[to], lsem.at[to])

    @pl.when(i == 0)
    def _():
        load(0, 0).start()

    if fill:
        @pl.when(i == 0)
        def _():
            zero_scr[...] = jnp.zeros_like(zero_scr)

            def fill_copy(e):
                start_row = pl.multiple_of(jnp.maximum(lb_ref[e], 0), MOE_ROWS)
                return pltpu.make_async_copy(zero_scr, xs_ref.at[pl.ds(start_row, MOE_ROWS)], zsem)

            def start(e, c):
                @pl.when(lb_ref[e] >= 0)
                def _():
                    fill_copy(e).start()
                return c

            def wait(e, c):
                @pl.when(lb_ref[e] >= 0)
                def _():
                    fill_copy(e).wait()
                return c

            lax.fori_loop(0, lb_ref.shape[0], start, 0)
            lax.fori_loop(0, lb_ref.shape[0], wait, 0)

    for k in range(TOP_K):
        pltpu.make_async_copy(dest_ref.at[i, k], idx_scr.at[slot, k], isem).start()
    for k in range(TOP_K):
        pltpu.make_async_copy(dest_ref.at[i, k], idx_scr.at[slot, k], isem).wait()

    def row_copy(s, r, k):
        return pltpu.make_async_copy(tok_scr.at[s, r], xs_ref.at[idx_scr[s, k, r]], rsem.at[s])

    def start_row(r, u):
        for k in range(TOP_K):
            row_copy(slot, r, k).start(priority=(u + k) % 2)

    def wait_rows(s):
        def wait_row(r, u):
            for k in range(TOP_K):
                row_copy(s, r, k).wait()
        _row_loop(tm, wait_row)

    load(i, slot).wait()
    _row_loop(tm, start_row)

    @pl.when(i > 0)
    def _():
        wait_rows(1 - slot)

    @pl.when(i + 1 < nstep)
    def _():
        load(i + 1, 1 - slot).start()

    @pl.when(i == nstep - 1)
    def _():
        wait_rows(slot)


def _dispatch(last_blk, dest, tok, xs, p):
    n, ts, _ = tok.shape
    tm = DMA_TILE
    fill = xs is None
    any_spec = pl.BlockSpec(memory_space=pl.ANY)
    args = [last_blk, dest, tok] + ([] if fill else [xs])
    return pl.pallas_call(
        functools.partial(_dispatch_kernel, fill=fill),
        out_shape=jax.ShapeDtypeStruct((p, ts, LANES), F32),
        grid_spec=pltpu.PrefetchScalarGridSpec(
            num_scalar_prefetch=1, grid=(n // tm,), in_specs=[any_spec] * (len(args) - 1), out_specs=any_spec,
            scratch_shapes=[pltpu.SMEM((2, TOP_K, tm), I32), pltpu.VMEM((2, tm, ts, LANES), F32),
                            pltpu.VMEM((MOE_ROWS, ts, LANES), F32), pltpu.SemaphoreType.DMA,
                            pltpu.SemaphoreType.DMA((2,)), pltpu.SemaphoreType.DMA((2,)), pltpu.SemaphoreType.DMA]),
        input_output_aliases={} if fill else {3: 0},
        compiler_params=_params(1), name="moe_dispatch",
    )(*args)


def _expert_kernel(be_ref, nu_ref, x_ref, w1_ref, w3_ref, w2_ref, y_ref, w1_scr, w3_scr, w2_scr):
    j = pl.program_id(0)
    e = be_ref[j]
    e_prev = be_ref[jnp.maximum(j - 1, 0)]

    @pl.when(j < nu_ref[0])
    def _():
        @pl.when((j == 0) | (e != e_prev))
        def _():
            w1_scr[...] = w1_ref[0, 0].astype(BF16)
            w3_scr[...] = w3_ref[0, 0].astype(BF16)
            w2_scr[...] = w2_ref[0, 0].astype(BF16)

        x = _tiles_to_rows(x_ref[...]).astype(BF16)
        hid = _silu(_dot(x, w1_scr[...])) * _dot(x, w3_scr[...])
        y_ref[...] = _rows_to_tiles(_dot(hid.astype(BF16), w2_scr[...]))

    @pl.when(j >= nu_ref[0])
    def _():
        y_ref[...] = jnp.zeros_like(y_ref)


def _experts(blk_e, n_used, xs, w1, w3, w2, layer):
    p, ts, _ = xs.shape
    d = ts * LANES
    de = w1.shape[-1]
    nblk = p // MOE_ROWS
    wmap = lambda j, be, nu: (layer, be[j], 0, 0)
    xmap = lambda j, be, nu: (jnp.minimum(j, nu[0] - 1), 0, 0)
    return pl.pallas_call(
        _expert_kernel,
        out_shape=jax.ShapeDtypeStruct((p, ts, LANES), F32),
        grid_spec=pltpu.PrefetchScalarGridSpec(
            num_scalar_prefetch=2, grid=(nblk,),
            in_specs=[pl.BlockSpec((MOE_ROWS, ts, LANES), xmap),
                      pl.BlockSpec((1, 1, d, de), wmap), pl.BlockSpec((1, 1, d, de), wmap),
                      pl.BlockSpec((1, 1, de, d), wmap)],
            out_specs=pl.BlockSpec((MOE_ROWS, ts, LANES), lambda j, be, nu: (j, 0, 0)),
            scratch_shapes=[pltpu.VMEM((d, de), BF16), pltpu.VMEM((d, de), BF16), pltpu.VMEM((de, d), BF16)]),
        compiler_params=_params(1), name="moe_experts",
    )(blk_e, n_used, xs, w1, w3, w2)


def _combine_kernel(dest_ref, ys_ref, meta_ref, h_ref, g2_ref, lg_ref, lb_ref, o_ref,
                    idx_scr, buf_scr, isem, rsem, *, alpha):
    i = pl.program_id(0)
    nstep = pl.num_programs(0)
    tm = h_ref.shape[0]

    def gather(step, slot):
        for k in range(TOP_K):
            pltpu.make_async_copy(dest_ref.at[step, k], idx_scr.at[slot, k], isem).start()
        for k in range(TOP_K):
            pltpu.make_async_copy(dest_ref.at[step, k], idx_scr.at[slot, k], isem).wait()

        def start_row(r, u):
            for k in range(TOP_K):
                pltpu.make_async_copy(ys_ref.at[idx_scr[slot, k, r]], buf_scr.at[slot, k, r],
                                      rsem.at[slot]).start(priority=(u + k) % 2)

        _row_loop(tm, start_row)

    depth = COMBINE_SLOTS - 1
    slot = i % COMBINE_SLOTS

    @pl.when(i == 0)
    def _():
        for a in range(depth):
            @pl.when(a < nstep)
            def _():
                gather(a, a)

    @pl.when(i + depth < nstep)
    def _():
        gather(i + depth, (i + depth) % COMBINE_SLOTS)

    def wait_row(r, u):
        for k in range(TOP_K):
            pltpu.make_async_copy(ys_ref.at[idx_scr[slot, k, r]], buf_scr.at[slot, k, r], rsem.at[slot]).wait()

    _row_loop(tm, wait_row)
    meta = meta_ref[0]
    wt = jnp.concatenate([meta, jnp.zeros((LANES - SUBLANES, tm), F32)], axis=0).T
    f = wt[:, 2 * TOP_K:2 * TOP_K + 1] * _tiles_to_rows(buf_scr[slot, 0])
    for k in range(1, TOP_K):
        f = f + wt[:, 2 * TOP_K + k:2 * TOP_K + k + 1] * _tiles_to_rows(buf_scr[slot, k])
    o_ref[...] = _ln(alpha * h_ref[...] + g2_ref[0] * f, lg_ref[...], lb_ref[...])


def _combine(dest, ys, meta, h, g2, lg, lb, seq, alpha):
    n, d = h.shape
    tm = DMA_TILE
    tps = seq // tm
    per = meta.shape[2] // tm
    row = lambda i: (i, 0)
    fixed = lambda i: (0, 0)
    return pl.pallas_call(
        functools.partial(_combine_kernel, alpha=alpha),
        out_shape=jax.ShapeDtypeStruct((n, d), F32),
        grid=(n // tm,),
        in_specs=[pl.BlockSpec(memory_space=pl.ANY), pl.BlockSpec(memory_space=pl.ANY),
                  pl.BlockSpec((1, SUBLANES, tm), lambda i: (i // per, 0, i % per)), pl.BlockSpec((tm, d), row),
                  pl.BlockSpec((1, 1, d), lambda i: (i // tps, 0, 0)),
                  pl.BlockSpec((1, d), fixed), pl.BlockSpec((1, d), fixed)],
        out_specs=pl.BlockSpec((tm, d), row),
        scratch_shapes=[pltpu.SMEM((COMBINE_SLOTS, TOP_K, tm), I32),
                        pltpu.VMEM((COMBINE_SLOTS, TOP_K, tm, d // LANES, LANES), F32),
                        pltpu.SemaphoreType.DMA, pltpu.SemaphoreType.DMA((COMBINE_SLOTS,))],
        compiler_params=_params(1), name="moe_combine",
    )(dest, ys, meta, h, g2, lg, lb)


def _moe(streams, counts, w1, w3, w2, layer, alpha):
    n_experts = w1.shape[1]
    n_assign = TOP_K * sum(s["tok"].shape[0] for s in streams)
    p = n_assign + n_experts * MOE_ROWS
    nblk = p // MOE_ROWS
    cnt = counts[:, 0].astype(I32)
    padded = (cnt + MOE_ROWS - 1) // MOE_ROWS * MOE_ROWS
    pend = jnp.cumsum(padded)
    pstart = pend - padded
    n_used = (pend[-1] // MOE_ROWS).astype(I32).reshape(1)
    blk = jnp.arange(nblk, dtype=I32) * MOE_ROWS
    blk_e = jnp.sum((pend[None, :] <= blk[:, None]).astype(I32), axis=1)
    last_e = jnp.sum((pend <= jnp.maximum(pend[-1] - 1, 0)).astype(I32))
    blk_e = jnp.minimum(blk_e, last_e).astype(I32)
    last_blk = jnp.where(padded > 0, pend - MOE_ROWS, -1).astype(I32)
    pstart_b = jnp.broadcast_to(pstart.astype(F32)[:, None], (n_experts, ROW_TILE))
    dests = [_dest(s["meta"], pstart_b) for s in streams]
    xs = None
    for s, dst in zip(streams, dests):
        xs = _dispatch(last_blk, dst, s["tok"], xs, p)
    ys = _experts(blk_e, n_used, xs, w1, w3, w2, layer)
    return [_combine(dst, ys, s["meta"], s["h1"], s["g2"], s["lg"], s["lb"], s["seq"], alpha)
            for s, dst in zip(streams, dests)]


def _log_sigmoid(x):
    return jnp.minimum(x, 0.0) - jnp.log(1.0 + jnp.exp(-jnp.abs(x)))


def _gla_in_kernel(h_ref, sc_ref, sh_ref, w_ref, wl_ref, wg_ref, bg_ref, qk_ref, v_ref, r_ref, g_ref, *, qscale):
    tm, d = h_ref.shape
    a = (h_ref[...] * (1.0 + sc_ref[0]) + sh_ref[0]).astype(BF16)
    qk = _dot(a, w_ref[:, :d])
    lane = lax.broadcasted_iota(I32, (tm, d), 1)
    qk_ref[...] = jnp.where(lane < d // 2, qk * qscale, qk)
    v_ref[...] = _dot(a, w_ref[:, d:2 * d]).astype(BF16)
    r_ref[...] = _dot(a, w_ref[:, 2 * d:])
    lr = _dot(a, wl_ref[...])
    gpre = _dot(lr.astype(BF16), wg_ref[...]) + bg_ref[...]
    g_ref[...] = _log_sigmoid(gpre) * (1.0 / GLA_TAU)


def _gla_weights(w_in, w_gate, b_gate, d):
    dk = d // 2
    n_lr = HY_DIRS * GLA_RANK
    w_main = w_in[:, :2 * dk + 2 * d].astype(BF16)
    w_lr = jnp.zeros((d, LANES), F32).at[:, :n_lr].set(w_in[:, 2 * dk + 2 * d:]).astype(BF16)
    w_g = jnp.zeros((LANES, HY_DIRS * dk), F32)
    for dr in range(HY_DIRS):
        w_g = w_g.at[dr * GLA_RANK:(dr + 1) * GLA_RANK, dr * dk:(dr + 1) * dk].set(w_gate[dr])
    return w_main, w_lr, w_g.astype(BF16), b_gate.reshape(1, HY_DIRS * dk)


def _gla_in(h, sc, sh, w_main, w_lr, w_g, b_g, seq):
    n, d = h.shape
    tm = min(ROW_TILE, seq)
    tps = seq // tm
    row = lambda i: (i, 0)
    per_b = lambda i: (i // tps, 0, 0)
    fixed = lambda i: (0, 0)
    qscale = float(((d // 2) // GLA_HEADS) ** -0.5)
    return pl.pallas_call(
        functools.partial(_gla_in_kernel, qscale=qscale),
        out_shape=(jax.ShapeDtypeStruct((n, d), F32), jax.ShapeDtypeStruct((n, d), BF16),
                   jax.ShapeDtypeStruct((n, d), F32), jax.ShapeDtypeStruct((n, d), F32)),
        grid=(n // tm,),
        in_specs=[pl.BlockSpec((tm, d), row), pl.BlockSpec((1, 1, d), per_b), pl.BlockSpec((1, 1, d), per_b),
                  pl.BlockSpec((d, 3 * d), fixed), pl.BlockSpec((d, LANES), fixed),
                  pl.BlockSpec((LANES, d), fixed), pl.BlockSpec((1, d), fixed)],
        out_specs=tuple(pl.BlockSpec((tm, d), row) for _ in range(4)),
        compiler_params=_params(1), name="gla_in",
    )(h, sc, sh, w_main, w_lr, w_g, b_g)


def _gla_direction(q_ref, k_ref, v_ref, g_ref, st_scr, o_ref, reverse):
    rt, dk = q_ref.shape
    cs = GLA_CHUNK
    nch = rt // cs
    g = g_ref[...]
    pos = lax.broadcasted_iota(I32, (rt, dk), 0) % cs
    b = g
    sh = 1
    while sh < cs:
        if reverse:
            b = b + jnp.where(pos < cs - sh, pltpu.roll(b, rt - sh, 0), 0.0)
        else:
            b = b + jnp.where(pos >= sh, pltpu.roll(b, sh, 0), 0.0)
        sh *= 2
    edge = 0 if reverse else cs - 1
    b3 = b.reshape(nch, cs, dk)
    b_edge = b3[:, edge:edge + 1, :]
    q = q_ref[...]
    k = k_ref[...]
    qe = (q * jnp.exp(b)).astype(BF16)
    ke = (k * jnp.exp(-b)).astype(BF16)
    kd = (k.reshape(nch, cs, dk) * jnp.exp(b_edge - b3)).astype(BF16)
    decay = jnp.exp(b_edge)
    r_i = lax.broadcasted_iota(I32, (rt, rt), 0)
    c_i = lax.broadcasted_iota(I32, (rt, rt), 1)
    same = (r_i // cs) == (c_i // cs)
    tri = (c_i >= r_i) if reverse else (c_i <= r_i)
    att = jnp.where(same & tri, _dot_nt(qe, ke), 0.0).astype(BF16)
    v = v_ref[...]
    intra = _dot(att, v)
    st = st_scr[...]
    order = range(nch - 1, -1, -1) if reverse else range(nch)
    for j in order:
        rows = slice(j * cs, (j + 1) * cs)
        o_ref[rows, :] = intra[rows, :] + _dot_nt(qe[rows, :], st.astype(BF16))
        st = st * decay[j] + _dot_tn(v[rows, :], kd[j])
    st_scr[...] = st


def _gla_scan_kernel(qkf, vf, gf, qkb, vb, gb, s0f, s0b, of, ob, sfo, sbo, sf_scr, sb_scr):
    c = pl.program_id(1)
    nh = GLA_HEADS
    d = qkf.shape[1]
    dkh, dvh = (d // 2) // nh, d // nh

    @pl.when(c == 0)
    def _():
        sf_scr[...] = s0f[0]
        sb_scr[...] = s0b[0]

    for hd in range(nh):
        qs, ks, vs = pl.ds(hd * dkh, dkh), pl.ds(d // 2 + hd * dkh, dkh), pl.ds(hd * dvh, dvh)
        _gla_direction(qkf.at[:, qs], qkf.at[:, ks], vf.at[:, vs], gf.at[:, qs], sf_scr.at[hd], of.at[:, vs], False)
        _gla_direction(qkb.at[:, qs], qkb.at[:, ks], vb.at[:, vs], gb.at[:, ks], sb_scr.at[hd], ob.at[:, vs], True)
    sfo[0] = sf_scr[...]
    sbo[0] = sb_scr[...]


def _gla_scan(qk, v, r, g, s0f, s0b, nb, seq):
    del r
    n, d = qk.shape
    nh = GLA_HEADS
    dkh, dvh = (d // 2) // nh, d // nh
    rt = min(ROW_TILE, seq)
    npb = seq // rt
    fw = pl.BlockSpec((rt, d), lambda b, c: (b * npb + c, 0))
    bw = pl.BlockSpec((rt, d), lambda b, c: (b * npb + npb - 1 - c, 0))
    st = pl.BlockSpec((1, nh, dvh, dkh), lambda b, c: (b, 0, 0, 0))
    state = jax.ShapeDtypeStruct((nb, nh, dvh, dkh), F32)
    return pl.pallas_call(
        _gla_scan_kernel,
        out_shape=(jax.ShapeDtypeStruct((n, d), F32), jax.ShapeDtypeStruct((n, d), F32), state, state),
        grid=(nb, npb),
        in_specs=[fw, fw, fw, bw, bw, bw, st, st],
        out_specs=(fw, bw, st, st),
        scratch_shapes=[pltpu.VMEM((nh, dvh, dkh), F32), pltpu.VMEM((nh, dvh, dkh), F32)],
        compiler_params=_params(2), name="gla_scan",
    )(qk, v, g, qk, v, g, s0f, s0b)


def _hy_in_kernel(h_ref, hp_ref, hn_ref, sc_ref, sh_ref, w_ref, b_ref, cw_ref, cb_ref, o_ref, a_scr, z_scr,
                  *, tps, halo):
    i = pl.program_id(0)
    tm, d = h_ref.shape
    sc = 1.0 + sc_ref[0]
    sh = sh_ref[0]
    a_scr[0:halo, :] = (hp_ref[...] * sc + sh).astype(BF16)
    a_scr[halo:halo + tm, :] = (h_ref[...] * sc + sh).astype(BF16)
    a_scr[halo + tm:, :] = (hn_ref[...] * sc + sh).astype(BF16)
    row = lax.broadcasted_iota(I32, (tm, 1), 0)
    first = jnp.where(i % tps == 0, 0, -1)
    last = jnp.where(i % tps == tps - 1, tm - 1, -1)
    for c in range(o_ref.shape[0]):
        cols = slice(c * d, (c + 1) * d)
        z_scr[...] = _dot(a_scr[...], w_ref[:, cols]) + b_ref[:, cols]
        zp = jnp.where(row == first, 0.0, z_scr[pl.ds(halo - 1, tm), :])
        zc = z_scr[pl.ds(halo, tm), :]
        zn = jnp.where(row == last, 0.0, z_scr[pl.ds(halo + 1, tm), :])
        o_ref[c] = cw_ref[0:1, cols] * zp + cw_ref[1:2, cols] * zc + cw_ref[2:3, cols] * zn + cb_ref[:, cols]


def _hy_in(h, sc, sh, w, b, cw, cb, seq):
    n, d = h.shape
    nsplit = w.shape[1] // d
    halo = 16
    tm = min(ROW_TILE, seq)
    tps = seq // tm
    hb = tm // halo
    row = lambda i: (i, 0)
    per_b = lambda i: (i // tps, 0, 0)
    fixed = lambda i: (0, 0)
    return pl.pallas_call(
        functools.partial(_hy_in_kernel, tps=tps, halo=halo),
        out_shape=jax.ShapeDtypeStruct((nsplit, n, d), F32),
        grid=(n // tm,),
        in_specs=[pl.BlockSpec((tm, d), row),
                  pl.BlockSpec((halo, d), lambda i: (jnp.maximum(i * hb - 1, 0), 0)),
                  pl.BlockSpec((halo, d), lambda i: (jnp.minimum((i + 1) * hb, n // halo - 1), 0)),
                  pl.BlockSpec((1, 1, d), per_b), pl.BlockSpec((1, 1, d), per_b),
                  pl.BlockSpec((d, nsplit * d), fixed), pl.BlockSpec((1, nsplit * d), fixed),
                  pl.BlockSpec((cw.shape[0], nsplit * d), fixed), pl.BlockSpec((1, nsplit * d), fixed)],
        out_specs=pl.BlockSpec((nsplit, tm, d), lambda i: (0, i, 0)),
        scratch_shapes=[pltpu.VMEM((tm + 2 * halo, d), BF16), pltpu.VMEM((tm + 2 * halo, d), F32)],
        compiler_params=_params(1), name="hyena_in",
    )(h, h, h, sc, sh, w, b, cw, cb)


def _hy_filter_rows(t, seq, w1_ref, b1_ref, w2_ref, b2_ref, w3_ref, b3_ref, n_bands):
    rows = t.shape[0]
    tf = t.astype(F32)
    t_lin = tf * (1.0 / (seq - 1))
    wpos = tf * (2.0 * math.pi / seq)
    lane = lax.broadcasted_iota(I32, (rows, LANES), 1)
    jb = jnp.where(lane > n_bands, lane - n_bands - 1, lane - 1).astype(F32)
    band = 1e-4 + jb * ((n_bands - 1 - 1e-4) / (n_bands - 1))
    ang = band * wpos
    z = jnp.where(lane == 0, t_lin,
                  jnp.where(lane <= n_bands, jnp.cos(ang), jnp.where(lane <= 2 * n_bands, -jnp.sin(ang), 0.0)))
    hf = jnp.sin(_dot(z.astype(BF16), w1_ref[...]) + b1_ref[...])
    hf = jnp.sin(_dot(hf.astype(BF16), w2_ref[...]) + b2_ref[...])
    hf = jnp.sin(_dot(hf.astype(BF16), w3_ref[...]) + b3_ref[...])
    return hf.astype(BF16), t_lin


def _hy_taps_kernel(w1_ref, b1_ref, w2_ref, b2_ref, w3_ref, b3_ref, w4_ref, o_ref, *, seq, n_bands):
    i = pl.program_id(0)
    n_ord, tr, d = o_ref.shape
    mlp = (w1_ref, b1_ref, w2_ref, b2_ref, w3_ref, b3_ref)
    r0 = i * tr
    second = (r0 >= seq).astype(I32)
    n_idx = r0 + lax.broadcasted_iota(I32, (tr, 1), 0)
    t = jnp.where(second == 1, 2 * seq - n_idx, n_idx)
    hf, t_lin = _hy_filter_rows(t, seq, *mlp, n_bands)
    c_idx = lax.broadcasted_iota(I32, (1, d), 1).astype(F32)
    min_decay = math.log(HY_DECAY_TARGET) / HY_DECAY_LONG_PCT
    max_decay = math.log(HY_DECAY_TARGET) / HY_DECAY_SHORT_PCT
    delta = jnp.abs(min_decay + c_idx * ((max_decay - min_decay) / (d - 1)))
    window = jnp.where(n_idx == seq, 0.0, jnp.exp(-t_lin * delta))
    for o in range(n_ord):
        o_ref[o] = _dot(hf, w4_ref[second, o]) * window

    @pl.when(i == 0)
    def _():
        hf0, _ = _hy_filter_rows(jnp.zeros((8, 1), I32), seq, *mlp, n_bands)
        first = lax.broadcasted_iota(I32, (8, 1), 0) == 0
        for o in range(n_ord):
            o_ref[o, 0:8, :] = o_ref[o, 0:8, :] + jnp.where(first, _dot(hf0, w4_ref[1, o]), 0.0)


def _hy_taps(seq, d, w1, b1, w2, b2, w3, b3, w4):
    emb, ff = w1.shape
    n_bands = (emb - 1) // 2
    w1p = jnp.zeros((LANES, ff), F32).at[:emb].set(w1).astype(BF16)
    w4r = w4.reshape(ff, HY_ORDER, HY_DIRS, d).transpose(2, 1, 0, 3).astype(BF16)
    tr = min(ROW_TILE, seq)
    fixed = lambda i: (0, 0)
    return pl.pallas_call(
        functools.partial(_hy_taps_kernel, seq=seq, n_bands=n_bands),
        out_shape=jax.ShapeDtypeStruct((HY_ORDER, 2 * seq, d), F32),
        grid=(2 * seq // tr,),
        in_specs=[pl.BlockSpec((LANES, ff), fixed), pl.BlockSpec((1, ff), fixed),
                  pl.BlockSpec((ff, ff), fixed), pl.BlockSpec((1, ff), fixed),
                  pl.BlockSpec((ff, ff), fixed), pl.BlockSpec((1, ff), fixed),
                  pl.BlockSpec((HY_DIRS, HY_ORDER, ff, d), lambda i: (0, 0, 0, 0))],
        out_specs=pl.BlockSpec((HY_ORDER, tr, d), lambda i: (0, i, 0)),
        compiler_params=_params(1), name="hyena_taps",
    )(w1p, b1.reshape(1, ff), w2.astype(BF16), b2.reshape(1, ff), w3.astype(BF16), b3.reshape(1, ff), w4r)


def _cos_sin(num, den):
    ang = (num % den).astype(F32) * (2.0 * math.pi / den)
    return jnp.cos(ang), jnp.sin(ang)


def _fft_tables(seq, n2):
    n = 2 * seq
    n1 = n // n2
    half = seq // n2
    ar = jnp.arange(n1, dtype=I32)
    ca, sa = _cos_sin(ar[:, None] * ar[None, :], n1)
    fa_half = jnp.concatenate([ca[:, :half], sa[:, :half]], 0).astype(BF16)
    fa_full = jnp.concatenate([ca, sa], 0).astype(BF16)
    ia = (jnp.concatenate([ca[:half], sa[:half]], 0) * (1.0 / n)).astype(BF16)
    k = ar[:, None, None] + n1 * jnp.arange(n2, dtype=I32)[None, :, None]
    cb, sb = _cos_sin(k * jnp.arange(n2, dtype=I32)[None, None, :], n)
    fb = jnp.concatenate([cb, sb], 1).astype(BF16)
    gb = jnp.concatenate([cb.transpose(0, 2, 1), sb.transpose(0, 2, 1)], 1).astype(BF16)
    return fa_half, fa_full, ia, fb, gb


def _pack_c(re, im):
    hi = pltpu.bitcast(re.astype(BF16).astype(F32), U32)
    lo = pltpu.bitcast(im.astype(BF16).astype(F32), U32)
    return hi | (lo >> 16)


def _unpack_c(w):
    re = pltpu.bitcast(w & jnp.uint32(0xFFFF0000), F32)
    im = pltpu.bitcast(w << 16, F32)
    return jnp.concatenate([re, im], axis=1).astype(BF16)


def _cmul_split(p, rows, cols, conj):
    a, b, c, d = p[:rows, :cols], p[:rows, cols:], p[rows:, :cols], p[rows:, cols:]
    return (a - d, b + c) if conj else (a + d, b - c)


def _fft_a_kernel(x_ref, f_ref, o_ref, x_scr, *, cplx):
    n1 = f_ref.shape[0] // 2
    per, rows, g, d = x_ref.shape[1:]
    for b in range(per):
        xb = x_ref[0, b].reshape(rows // SUBLANES, SUBLANES, g, d)
        x_scr[b] = jnp.swapaxes(xb, 1, 2)
    for s in range(g):
        if cplx:
            z = jnp.concatenate([x_scr[b, :, s].reshape(rows, d) for b in range(per)], axis=1).astype(BF16)
            ar, ai = _cmul_split(_dot(f_ref[...], z), n1, d, False)
        else:
            p = _dot(f_ref[...], x_scr[0, :, s].reshape(rows, d).astype(BF16))
            ar, ai = p[:n1], -p[n1:]
        o_ref[0, :, s, :] = _pack_c(ar, ai)


def _fft_a(x5, which, f, cplx):
    _, nb, rows, n2, d = x5.shape
    n1 = f.shape[0] // 2
    per = 2 if cplx else 1
    groups = nb // per
    g = FFT_GROUP
    return pl.pallas_call(
        functools.partial(_fft_a_kernel, cplx=cplx),
        out_shape=jax.ShapeDtypeStruct((groups, n1, n2, d), U32),
        grid=(groups, n2 // g),
        in_specs=[pl.BlockSpec((1, per, rows, g, d), lambda p, j: (which, p, 0, j, 0)),
                  pl.BlockSpec((2 * n1, rows), lambda p, j: (0, 0))],
        out_specs=pl.BlockSpec((1, n1, g, d), lambda p, j: (p, 0, j, 0)),
        scratch_shapes=[pltpu.VMEM((per, rows // SUBLANES, g, SUBLANES, d), F32)],
        compiler_params=_params(2), name="fft_stage_a",
    )(x5, f)


def _fft_b_kernel(*refs, conv):
    if conv:
        a_ref, fb_ref, k_ref, gb_ref, o_ref = refs
    else:
        a_ref, fb_ref, o_ref = refs
    g, n2, d = a_ref.shape[1:]
    for s in range(g):
        xr, xi = _cmul_split(_dot(fb_ref[s], _unpack_c(a_ref[0, s])), n2, d, False)
        if not conv:
            o_ref[0, 0, s] = xr
            o_ref[0, 1, s] = xi
            continue
        kr, ki = k_ref[0, 0, s], k_ref[0, 1, s]
        y = jnp.concatenate([xr * kr - xi * ki, xr * ki + xi * kr], axis=1).astype(BF16)
        yr, yi = _cmul_split(_dot(gb_ref[s], y), n2, d, True)
        o_ref[0, :, s, :] = _pack_c(yr, yi)


def _fft_b(a4, fb, kspec=None, which=0, gb=None):
    groups, n1, n2, d = a4.shape
    conv = kspec is not None
    g = FFT_GROUP
    slab = pl.BlockSpec((1, g, n2, d), lambda k, p: (p, k, 0, 0))
    tab = pl.BlockSpec((g, 2 * n2, n2), lambda k, p: (k, 0, 0))
    if conv:
        in_specs = [slab, tab, pl.BlockSpec((1, 2, g, n2, d), lambda k, p: (which, 0, k, 0, 0)), tab]
        args = [a4, fb, kspec, gb]
        out_shape = jax.ShapeDtypeStruct((groups, n2, n1, d), U32)
        out_spec = pl.BlockSpec((1, n2, g, d), lambda k, p: (p, 0, k, 0))
    else:
        in_specs = [slab, tab]
        args = [a4, fb]
        out_shape = jax.ShapeDtypeStruct((groups, 2, n1, n2, d), F32)
        out_spec = pl.BlockSpec((1, 2, g, n2, d), lambda k, p: (p, 0, k, 0, 0))
    return pl.pallas_call(
        functools.partial(_fft_b_kernel, conv=conv),
        out_shape=out_shape,
        grid=(n1 // g, groups),
        in_specs=in_specs, out_specs=out_spec,
        compiler_params=_params(2), name="fft_stage_b_conv" if conv else "fft_stage_b",
    )(*args)


def _fft_c_kernel(y_ref, ia_ref, u_ref, g_ref, skip_ref, o_ref):
    half = ia_ref.shape[0] // 2
    g, _, d = y_ref.shape[1:]
    for s in range(g):
        cr, ci = _cmul_split(_dot(ia_ref[...], _unpack_c(y_ref[0, s])), half, d, True)
        o_ref[0, :, s, :] = cr
        o_ref[1, :, s, :] = ci
    skip = skip_ref[...][None]
    for b in range(2):
        o_ref[b] = g_ref[0, b] * (o_ref[b] + u_ref[0, b] * skip)


def _fft_c(y4, ia, u5, u_which, g5, g_which, skip):
    groups, n2, n1, d = y4.shape
    half = ia.shape[0] // 2
    g = FFT_GROUP
    return pl.pallas_call(
        _fft_c_kernel,
        out_shape=jax.ShapeDtypeStruct((2 * groups, half, n2, d), F32),
        grid=(groups, n2 // g),
        in_specs=[pl.BlockSpec((1, g, n1, d), lambda p, j: (p, j, 0, 0)),
                  pl.BlockSpec((2 * half, n1), lambda p, j: (0, 0)),
                  pl.BlockSpec((1, 2, half, g, d), lambda p, j: (u_which, p, 0, j, 0)),
                  pl.BlockSpec((1, 2, half, g, d), lambda p, j: (g_which, p, 0, j, 0)),
                  pl.BlockSpec((1, d), lambda p, j: (0, 0))],
        out_specs=pl.BlockSpec((2, half, g, d), lambda p, j: (p, 0, j, 0)),
        compiler_params=_params(2), name="fft_stage_c",
    )(y4, ia, u5, g5, skip.reshape(1, d))


def _hy_long_conv(zs, taps, skip, nb, seq):
    nsplit, n, d = zs.shape
    n2 = FFT_N2
    half = seq // n2
    n1 = 2 * half
    fa_half, fa_full, ia, fb, gb = _fft_tables(seq, n2)
    kspec = _fft_b(_fft_a(taps.reshape(1, HY_ORDER, n1, n2, d), 0, fa_full, False), fb)
    zs5 = zs.reshape(nsplit, nb, half, n2, d)
    u5, u_which = zs5, 0
    for o in range(HY_ORDER):
        a = _fft_a(u5, u_which, fa_half, True)
        y = _fft_b(a, fb, kspec, o, gb)
        out = _fft_c(y, ia, u5, u_which, zs5, o + 1, skip[o])
        u5, u_which = out[None], 0
    return out.reshape(n, d)


def _short_conv_kernel(u_ref, g_ref, t_ref, skip_ref, ff_ref, tf_ref, fi_ref, o_ref):
    seq, ct = u_ref.shape[-2:]
    n = 2 * seq
    z = jnp.concatenate([u_ref[0, 0], u_ref[0, 1]], axis=1).astype(BF16)
    xr, xi = _cmul_split(_dot(ff_ref[...], z), n, ct, False)
    pt = _dot(tf_ref[...], t_ref[0].astype(BF16))
    kr, ki = pt[:n], -pt[n:]
    y = jnp.concatenate([xr * kr - xi * ki, xr * ki + xi * kr], axis=1).astype(BF16)
    cr, ci = _cmul_split(_dot(fi_ref[...], y), seq, ct, True)
    skip = skip_ref[...]
    o_ref[0] = g_ref[0, 0] * (cr + u_ref[0, 0] * skip)
    o_ref[1] = g_ref[0, 1] * (ci + u_ref[0, 1] * skip)


def _hy_short_conv(zs, taps, skip, nb, seq):
    nsplit, n, d = zs.shape
    nn = 2 * seq
    ar = jnp.arange(nn, dtype=I32)
    cm, sm = _cos_sin(ar[:, None] * ar[None, :], nn)
    ff = jnp.concatenate([cm[:, :seq], sm[:, :seq]], 0).astype(BF16)
    tf = jnp.concatenate([cm, sm], 0).astype(BF16)
    fi = (jnp.concatenate([cm[:seq], sm[:seq]], 0) * (1.0 / nn)).astype(BF16)
    ct = d // 2
    zs4 = zs.reshape(nsplit, nb, seq, d)
    u4, u_which = zs4, 0
    for o in range(HY_ORDER):
        out = pl.pallas_call(
            _short_conv_kernel,
            out_shape=jax.ShapeDtypeStruct((nb, seq, d), F32),
            grid=(nb // 2, d // ct),
            in_specs=[pl.BlockSpec((1, 2, seq, ct), lambda p, j, w=u_which: (w, p, 0, j)),
                      pl.BlockSpec((1, 2, seq, ct), lambda p, j, w=o + 1: (w, p, 0, j)),
                      pl.BlockSpec((1, nn, ct), lambda p, j, w=o: (w, 0, j)),
                      pl.BlockSpec((1, ct), lambda p, j: (0, j)),
                      pl.BlockSpec((2 * nn, seq), lambda p, j: (0, 0)),
                      pl.BlockSpec((2 * nn, nn), lambda p, j: (0, 0)),
                      pl.BlockSpec((2 * seq, nn), lambda p, j: (0, 0))],
            out_specs=pl.BlockSpec((2, seq, ct), lambda p, j: (p, 0, j)),
            compiler_params=_params(2), name="hyena_short_conv",
        )(u4, zs4, taps, skip[o].reshape(1, d), ff, tf, fi)
        u4, u_which = out[None], 0
    return out.reshape(n, d)


def kernel(x, c, ctx, c_ctx, w_mod, b_mod, ln_g, ln_b, hy_w_in, hy_b_in, hy_conv_w, hy_conv_b, hy_f_w1, hy_f_b1, hy_f_w2, hy_f_b2, hy_f_w3, hy_f_b3, hy_f_w4, hy_skip, hy_w_out, hy_b_out, gla_w_in, gla_w_gate, gla_b_gate, gla_norm_g, gla_w_out, gm_w_in, gm_b_in, gm_ln_g, gm_ln_b, gm_ws, gm_bs, gm_w_out, gm_b_out, router_w, router_b, moe_w1, moe_w3, moe_w2):
    B, L, D = x.shape
    Lc = ctx.shape[1]
    depth = w_mod.shape[0]
    E = router_w.shape[1]
    alpha = (2.0 * depth) ** 0.25
    gla_layers = list(range(1, depth, N_MIXERS))
    last_ctx = gla_layers[-1] if gla_layers else -1

    cvec = jnp.zeros((8, D), F32).at[:B].set(c).at[B].set(c_ctx)
    mod = _modulation(cvec, w_mod, b_mod)
    h = _add_pos(x.reshape(B * L, D), L)
    hc = ctx.reshape(B * Lc, D)
    rw_t = router_w.T
    rw_hi = rw_t.astype(BF16)
    rhl = jnp.concatenate([rw_hi, (rw_t - rw_hi.astype(F32)).astype(BF16)], axis=0)
    rb_b = jnp.broadcast_to(router_b[:, None], (E, ROW_TILE))
    zero_cnt = jnp.zeros((E, LANES), F32)
    row2 = lambda v: v.reshape(1, -1)

    for i in range(depth):
        kind, j = i % N_MIXERS, i // N_MIXERS
        ctx_full = i < last_ctx
        ctx_any = i <= last_ctx
        lat = [mod[i, :B, k * D:(k + 1) * D].reshape(B, 1, D) for k in range(6)]
        cm = [jnp.broadcast_to(mod[i, B, k * D:(k + 1) * D].reshape(1, 1, D), (B, 1, D)) for k in range(6)]
        streams_in = [(h, lat, L)] + ([(hc, cm, Lc)] if ctx_any else [])
        pre = []
        if kind == 0:
            w_in = hy_w_in[j].astype(BF16)
            for s_h, s_m, s_len in streams_in[:1 + int(ctx_full)]:
                zs = _hy_in(s_h, s_m[1], s_m[0], w_in, row2(hy_b_in[j]), hy_conv_w[j], row2(hy_conv_b[j]), s_len)
                taps = _hy_taps(s_len, D, hy_f_w1[j], hy_f_b1[j], hy_f_w2[j], hy_f_b2[j], hy_f_w3[j], hy_f_b3[j],
                                hy_f_w4[j])
                if s_len == L:
                    y2 = _hy_long_conv(zs, taps, hy_skip[j], B, s_len)
                else:
                    y2 = _hy_short_conv(zs, taps, hy_skip[j], B, s_len)
                pre.append(("hyena", (y2,), lambda tm, d: [pl.BlockSpec((tm, d), lambda t: (t, 0))]))
            w_out, b_out = hy_w_out[j].astype(BF16), row2(hy_b_out[j])
        elif kind == 1:
            w_main, w_lr, w_g, b_g = _gla_weights(gla_w_in[j], gla_w_gate[j], gla_b_gate[j], D)
            proj = [_gla_in(s_h, s_m[1], s_m[0], w_main, w_lr, w_g, b_g, s_len) for s_h, s_m, s_len in streams_in]
            dvh = D // GLA_HEADS
            dkh = (D // 2) // GLA_HEADS
            zero_state = jnp.zeros((B, GLA_HEADS, dvh, dkh), F32)
            if ctx_any:
                ocf, ocb, s_f, s_b = _gla_scan(*proj[1], zero_state, zero_state, B, Lc)
            else:
                s_f = s_b = zero_state
            o_f, o_b, _, _ = _gla_scan(*proj[0], s_f, s_b, B, L)
            gla_specs = lambda tm, d: [pl.BlockSpec((tm, d), lambda t: (t, 0))] * 3 + [pl.BlockSpec((1, d), lambda t: (0, 0))]
            pre.append(("gla", (o_f, o_b, proj[0][2], row2(gla_norm_g[j])), gla_specs))
            if ctx_full:
                pre.append(("gla", (ocf, ocb, proj[1][2], row2(gla_norm_g[j])), gla_specs))
            w_out, b_out = gla_w_out[j].astype(BF16), jnp.zeros((1, D), F32)
        else:
            w_in = gm_w_in[j].astype(BF16)
            ws = gm_ws[j].astype(BF16)
            bs_exp = jnp.repeat(gm_bs[j].T, D // GM_HEADS, axis=1)
            gm_specs = lambda tm, d: [pl.BlockSpec((tm, d), lambda t: (t, 0))] * 2 + [
                pl.BlockSpec((GM_HEADS, GM_CHUNK, GM_CHUNK), lambda t: (0, 0, 0)), pl.BlockSpec((GM_CHUNK, d), lambda t: (0, 0))]
            for s_h, s_m, s_len in streams_in[:1 + int(ctx_full)]:
                u, vn = _gm_in(s_h, s_m[1], s_m[0], w_in, row2(gm_b_in[j]), row2(gm_ln_g[j]), row2(gm_ln_b[j]), s_len)
                pre.append(("gmlp", (u, vn, ws, bs_exp), gm_specs))
            w_out, b_out = gm_w_out[j].astype(BF16), row2(gm_b_out[j])

        moe_streams = []
        cnt = zero_cnt
        for (s_h, s_m, s_len), (pk, pargs, pspecs) in zip(streams_in, pre):
            h1, tok, meta, cnt = _post(pk, pargs, pspecs, s_h, w_out, b_out, s_m[2], row2(ln_g[i, 0]), row2(ln_b[i, 0]),
                                       s_m[4], s_m[3], rhl, rb_b, cnt, s_len, alpha)
            moe_streams.append(dict(h1=h1, tok=tok, meta=meta, g2=s_m[5], lg=row2(ln_g[i, 1]), lb=row2(ln_b[i, 1]),
                                    seq=s_len))
        outs = _moe(moe_streams, cnt, moe_w1, moe_w3, moe_w2, i, alpha)
        h = outs[0]
        if ctx_full:
            hc = outs[1]
    return h.reshape(B, L, D)
```

```python
import functools
import math

import jax
import jax.numpy as jnp
from jax import lax
from jax.experimental import pallas as pl
from jax.experimental.pallas import tpu as pltpu

F32 = jnp.float32
BF16 = jnp.bfloat16
I32 = jnp.int32
U32 = jnp.uint32
HIGHEST = lax.Precision.HIGHEST

GRID_W = 64
N_MIXERS = 3
LN_EPS = 1e-5
HY_ORDER = 2
HY_DIRS = 2
HY_DECAY_TARGET = 1e-2
HY_DECAY_SHORT_PCT = 0.3
HY_DECAY_LONG_PCT = 1.5
GLA_HEADS = 4
GLA_RANK = 16
GLA_TAU = 16.0
GLA_CHUNK = 64
GM_CHUNK = 128
GM_HEADS = 4
N_GROUPS = 4
TOP_K = 2

LANES = 128
SUBLANES = 8
V7X_VMEM_LIMIT_BYTES = 56 * 1024 * 1024
MOE_ROWS = 512
ROW_TILE = 512
DMA_TILE = 256
DMA_UNROLL = 8
COMBINE_SLOTS = 2
POST_SLOTS = 3
FFT_N2 = 128
FFT_GROUP = SUBLANES


def _params(n_grid):
    return pltpu.CompilerParams(dimension_semantics=("arbitrary",) * n_grid,
                                vmem_limit_bytes=V7X_VMEM_LIMIT_BYTES)


def _dot(a, b):
    return jnp.dot(a, b, preferred_element_type=F32)


def _dot_nt(a, b):
    return lax.dot_general(a, b, (((1,), (1,)), ((), ())), preferred_element_type=F32)


def _dot_tn(a, b):
    return lax.dot_general(a, b, (((0,), (0,)), ((), ())), preferred_element_type=F32)


def _rows_to_tiles(x):
    rows, d = x.shape
    nsub = d // LANES
    parts = [x[:, s * LANES:(s + 1) * LANES].reshape(rows // SUBLANES, SUBLANES, LANES) for s in range(nsub)]
    return jnp.swapaxes(jnp.stack(parts, axis=1), 1, 2).reshape(rows, nsub, LANES)


def _tiles_to_rows(x):
    rows, nsub, _ = x.shape
    y = jnp.swapaxes(x.reshape(rows // SUBLANES, SUBLANES, nsub, LANES), 1, 2)
    return jnp.concatenate([y[:, s].reshape(rows, LANES) for s in range(nsub)], axis=1)


def _ln(x, g, b):
    mu = jnp.mean(x, -1, keepdims=True)
    xc = x - mu
    var = jnp.mean(xc * xc, -1, keepdims=True)
    return xc * lax.rsqrt(var + LN_EPS) * g + b


def _silu(x):
    return x * jax.nn.sigmoid(x)


def _gelu_tanh(x):
    return 0.5 * x * (1.0 + jnp.tanh(math.sqrt(2.0 / math.pi) * (x + 0.044715 * (x * x * x))))


def _mod_kernel(c_ref, w_ref, b_ref, o_ref):
    s = _silu(c_ref[...])
    o_ref[0] = _dot(s.astype(BF16), w_ref[0].astype(BF16)) + b_ref[0]


def _modulation(cvec, w_mod, b_mod):
    depth, d, n = w_mod.shape
    tn = n // 4
    return pl.pallas_call(
        _mod_kernel,
        out_shape=jax.ShapeDtypeStruct((depth, 8, n), F32),
        grid=(depth, n // tn),
        in_specs=[pl.BlockSpec((8, d), lambda i, j: (0, 0)),
                  pl.BlockSpec((1, d, tn), lambda i, j: (i, 0, j)),
                  pl.BlockSpec((1, 1, tn), lambda i, j: (i, 0, j))],
        out_specs=pl.BlockSpec((1, 8, tn), lambda i, j: (i, 0, j)),
        compiler_params=_params(2), name="modulation",
    )(cvec, w_mod, b_mod.reshape(depth, 1, n))


def _pos_table_kernel(o_ref, *, q):
    rows, cols = o_ref.shape
    p = lax.broadcasted_iota(I32, (rows, cols), 0).astype(F32)
    lane = lax.broadcasted_iota(I32, (rows, cols), 1)
    j = jnp.where(lane >= q, lane - q, lane).astype(F32)
    omega = jnp.exp(j * (-math.log(10000.0) / q))
    ang = p * omega
    o_ref[...] = jnp.where(lane >= q, jnp.cos(ang), jnp.sin(ang))


def _pos_table(n, d):
    q = d // 4
    return pl.pallas_call(functools.partial(_pos_table_kernel, q=q),
                          out_shape=jax.ShapeDtypeStruct((n, 2 * q), F32), name="pos_table")()


def _add_pos_kernel(x_ref, er_ref, ec_ref, o_ref, *, half):
    tm = x_ref.shape[0]
    reps = tm // GRID_W
    er = er_ref[...]
    er_rows = jnp.broadcast_to(er[:, None, :], (reps, GRID_W, half)).reshape(tm, half)
    ec_rows = jnp.broadcast_to(ec_ref[...][None], (reps, GRID_W, half)).reshape(tm, half)
    o_ref[:, :half] = x_ref[:, :half] + er_rows
    o_ref[:, half:] = x_ref[:, half:] + ec_rows


def _add_pos(x2d, seq):
    n, d = x2d.shape
    half = d // 2
    rows = seq // GRID_W
    er = _pos_table(rows, d)
    ec = _pos_table(GRID_W, d)
    tm = ROW_TILE
    reps = tm // GRID_W
    tps = seq // tm
    return pl.pallas_call(
        functools.partial(_add_pos_kernel, half=half),
        out_shape=jax.ShapeDtypeStruct((n, d), F32),
        grid=(n // tm,),
        in_specs=[pl.BlockSpec((tm, d), lambda i: (i, 0)),
                  pl.BlockSpec((reps, half), lambda i: (i % tps, 0)),
                  pl.BlockSpec((GRID_W, half), lambda i: (0, 0))],
        out_specs=pl.BlockSpec((tm, d), lambda i: (i, 0)),
        compiler_params=_params(1), name="add_pos",
    )(x2d, er, ec)


def _gm_in_kernel(h_ref, sc_ref, sh_ref, w_ref, b_ref, g_ref, bb_ref, u_ref, v_ref):
    d = h_ref.shape[1]
    a = (h_ref[...] * (1.0 + sc_ref[0]) + sh_ref[0]).astype(BF16)
    u_ref[...] = _gelu_tanh(_dot(a, w_ref[:, :d]) + b_ref[:, :d])
    v = _gelu_tanh(_dot(a, w_ref[:, d:]) + b_ref[:, d:])
    v_ref[...] = _ln(v, g_ref[...], bb_ref[...]).astype(BF16)


def _gm_in(h, sc, sh, w, b, g, bb, seq):
    n, d = h.shape
    tm = min(ROW_TILE, seq)
    tps = seq // tm
    row = lambda i: (i, 0)
    per_b = lambda i: (i // tps, 0, 0)
    fixed = lambda i: (0, 0)
    return pl.pallas_call(
        _gm_in_kernel,
        out_shape=(jax.ShapeDtypeStruct((n, d), F32), jax.ShapeDtypeStruct((n, d), BF16)),
        grid=(n // tm,),
        in_specs=[pl.BlockSpec((tm, d), row), pl.BlockSpec((1, 1, d), per_b), pl.BlockSpec((1, 1, d), per_b),
                  pl.BlockSpec((d, 2 * d), fixed), pl.BlockSpec((1, 2 * d), fixed),
                  pl.BlockSpec((1, d), fixed), pl.BlockSpec((1, d), fixed)],
        out_specs=(pl.BlockSpec((tm, d), row), pl.BlockSpec((tm, d), row)),
        compiler_params=_params(1), name="gmlp_in",
    )(h, sc, sh, w, b, g, bb)


def _gm_prologue(u_ref, v_ref, ws_ref, bs_ref):
    tm, d = u_ref.shape
    dh = d // GM_HEADS
    for c in range(tm // GM_CHUNK):
        rows = slice(c * GM_CHUNK, (c + 1) * GM_CHUNK)
        parts = [_dot(ws_ref[g], v_ref[rows, g * dh:(g + 1) * dh]) for g in range(GM_HEADS)]
        vm = jnp.concatenate(parts, axis=1) + bs_ref[...]
        yield rows, (u_ref[rows, :] * vm).astype(BF16)


def _gla_prologue(of_ref, ob_ref, r_ref, ng_ref):
    tm, d = of_ref.shape
    dh = d // GLA_HEADS
    o = of_ref[...] + ob_ref[...]
    parts = []
    for hd in range(GLA_HEADS):
        oh = o[:, hd * dh:(hd + 1) * dh]
        mu = jnp.mean(oh, -1, keepdims=True)
        oc = oh - mu
        var = jnp.mean(oc * oc, -1, keepdims=True)
        parts.append(oc * lax.rsqrt(var + LN_EPS) * ng_ref[:, hd * dh:(hd + 1) * dh])
    y = jnp.concatenate(parts, axis=1) * _silu(r_ref[...])
    yield slice(0, tm), y.astype(BF16)


def _hy_prologue(y_ref):
    yield slice(0, y_ref.shape[0]), y_ref[...].astype(BF16)


_PROLOGUES = {"gmlp": (_gm_prologue, 4), "gla": (_gla_prologue, 4), "hyena": (_hy_prologue, 1)}


def _route_t(scores, biased):
    n_experts, tm = scores.shape
    gsz = n_experts // N_GROUPS
    neg = jnp.float32(-jnp.inf)
    v3 = biased.reshape(N_GROUPS, gsz, tm)
    sub = lax.broadcasted_iota(I32, v3.shape, 1).astype(F32)
    m1 = jnp.max(v3, axis=1, keepdims=True)
    i1 = jnp.min(jnp.where(v3 == m1, sub, float(gsz)), axis=1, keepdims=True)
    v3b = jnp.where(sub == i1, neg, v3)
    m2 = jnp.max(v3b, axis=1, keepdims=True)
    i2 = jnp.min(jnp.where(v3b == m2, sub, float(gsz)), axis=1, keepdims=True)
    gscore = (m1 + m2).reshape(N_GROUPS, tm)
    i1 = i1.reshape(N_GROUPS, tm)
    i2 = i2.reshape(N_GROUPS, tm)
    best, e0, e1 = gscore[0:1], i1[0:1], i2[0:1]
    for g in range(1, N_GROUPS):
        better = gscore[g:g + 1] > best
        best = jnp.where(better, gscore[g:g + 1], best)
        e0 = jnp.where(better, i1[g:g + 1] + float(g * gsz), e0)
        e1 = jnp.where(better, i2[g:g + 1] + float(g * gsz), e1)
    row = lax.broadcasted_iota(I32, scores.shape, 0).astype(F32)
    oh0 = (row == e0).astype(F32)
    oh1 = (row == e1).astype(F32)
    w0 = jnp.sum(oh0 * scores, axis=0, keepdims=True)
    w1 = jnp.sum(oh1 * scores, axis=0, keepdims=True)
    den = w0 + w1
    return e0, e1, oh0, oh1, w0 / den, w1 / den


def _post_kernel(*refs, kind, alpha, cap, aliased, finalize):
    prologue, n_pro = _PROLOGUES[kind]
    pro = refs[:n_pro]
    rest = refs[n_pro:]
    (h_ref, w_ref, b_ref, g1_ref, lg_ref, lb_ref, sc_ref, sh_ref, rhl_ref, rb_ref, cin_ref) = rest[:11]
    rest = rest[11 + int(aliased):]
    (h1_ref, meta_ref, cout_ref, xs_ref, cnt_scr, tok_scr, dvm_scr, idx_scr, zero_scr, cvm_scr, csm_scr,
     isem, rsem, zsem) = rest
    i = pl.program_id(0)
    nstep = pl.num_programs(0)
    tm = h_ref.shape[0]
    n_experts = rb_ref.shape[0]
    slot = i % POST_SLOTS

    def idx_copy(s, k):
        return pltpu.make_async_copy(dvm_scr.at[s, k], idx_scr.at[s, k], isem.at[s])

    def row_copy(s, r, k):
        return pltpu.make_async_copy(tok_scr.at[s, r], xs_ref.at[idx_scr[s, k, r]], rsem.at[s])

    def start_rows(s):
        for k in range(TOP_K):
            idx_copy(s, k).wait()

        def start_row(r, u):
            for k in range(TOP_K):
                row_copy(s, r, k).start(priority=(u + k) % 2)
        _row_loop(tm, start_row)

    def wait_rows(s):
        def wait_row(r, u):
            for k in range(TOP_K):
                row_copy(s, r, k).wait()
        _row_loop(tm, wait_row)

    @pl.when(i == 0)
    def _():
        cnt_scr[...] = cin_ref[...]

    @pl.when(i >= POST_SLOTS)
    def _():
        wait_rows(slot)

    for rows, y in prologue(*pro):
        out = _dot(y, w_ref[...]) + b_ref[...]
        h1_ref[rows, :] = _ln(alpha * h_ref[rows, :] + g1_ref[0] * out, lg_ref[...], lb_ref[...])
    tok = h1_ref[...] * (1.0 + sc_ref[0]) + sh_ref[0]
    tok_scr[slot] = _rows_to_tiles(tok)
    t_hi = tok.astype(BF16)
    t_lo = (tok - t_hi.astype(F32)).astype(BF16)
    p_hi = _dot_nt(rhl_ref[...], t_hi)
    logits = p_hi[:n_experts] + p_hi[n_experts:] + _dot_nt(rhl_ref[0:n_experts, :], t_lo)
    scores = jax.nn.sigmoid(logits)
    e0, e1, oh0, oh1, w0, w1 = _route_t(scores, scores + rb_ref[...])
    r_i = lax.broadcasted_iota(I32, (tm, tm), 0)
    c_i = lax.broadcasted_iota(I32, (tm, tm), 1)
    upper = (r_i < c_i).astype(BF16)
    oh = oh0 + oh1
    carry = cnt_scr[...]
    before = _dot(oh.astype(BF16), upper) + jnp.concatenate([carry] * (tm // LANES), axis=1)
    rank0 = jnp.sum(oh0 * before, axis=0, keepdims=True)
    rank1 = jnp.sum(oh1 * before, axis=0, keepdims=True)
    cnt_scr[...] = carry + jnp.sum(oh, axis=1, keepdims=True)
    zero = jnp.zeros_like(w0)
    meta_ref[0] = jnp.concatenate([e0, e1, rank0, rank1, w0, w1, zero, zero], axis=0)
    cout_ref[...] = cnt_scr[...]

    dvm_scr[slot] = jnp.concatenate([e0 * float(cap) + rank0, e1 * float(cap) + rank1] + [zero] * (SUBLANES - TOP_K),
                                    axis=0).astype(I32)
    for k in range(TOP_K):
        idx_copy(slot, k).start()

    @pl.when(i >= 1)
    def _():
        start_rows((i + POST_SLOTS - 1) % POST_SLOTS)

    @pl.when(i == nstep - 1)
    def _():
        start_rows(slot)
        for back in range(POST_SLOTS - 1, -1, -1):
            @pl.when(i >= back)
            def _():
                wait_rows((i + POST_SLOTS - back) % POST_SLOTS)

        if finalize:
            zero_scr[...] = jnp.zeros_like(zero_scr)
            cvm_scr[...] = cnt_scr[...].astype(I32)
            pltpu.make_async_copy(cvm_scr, csm_scr, zsem).start()
            pltpu.make_async_copy(cvm_scr, csm_scr, zsem).wait()

            def fill_copy(e):
                return pltpu.make_async_copy(zero_scr, xs_ref.at[pl.ds(e * cap + csm_scr[e, 0], MOE_ROWS)], zsem)

            def fill_start(e, c):
                fill_copy(e).start()
                return c

            def fill_wait(e, c):
                fill_copy(e).wait()
                return c

            lax.fori_loop(0, n_experts, fill_start, 0)
            lax.fori_loop(0, n_experts, fill_wait, 0)


def _post(kind, pro_args, pro_specs, h, w, b, g1, lg, lb, sc, sh, rhl, rb, cnt_in, xs, cap, finalize, seq, alpha):
    n, d = h.shape
    ts = d // LANES
    tm = min(ROW_TILE, seq)
    tps = seq // tm
    n_experts = rb.shape[0]
    aliased = xs is not None
    row = lambda i: (i, 0)
    per_b = lambda i: (i // tps, 0, 0)
    fixed = lambda i: (0, 0)
    any_spec = pl.BlockSpec(memory_space=pl.ANY)
    pro_in = list(pro_specs(tm, d))
    in_specs = pro_in + [
        pl.BlockSpec((tm, d), row), pl.BlockSpec((d, d), fixed), pl.BlockSpec((1, d), fixed),
        pl.BlockSpec((1, 1, d), per_b), pl.BlockSpec((1, d), fixed), pl.BlockSpec((1, d), fixed),
        pl.BlockSpec((1, 1, d), per_b), pl.BlockSpec((1, 1, d), per_b),
        pl.BlockSpec((2 * n_experts, d), fixed), pl.BlockSpec((n_experts, tm), fixed),
        pl.BlockSpec((n_experts, LANES), fixed)] + ([any_spec] if aliased else [])
    args = list(pro_args) + [h, w, b, g1, lg, lb, sc, sh, rhl, rb[:, :tm], cnt_in] + ([xs] if aliased else [])
    return pl.pallas_call(
        functools.partial(_post_kernel, kind=kind, alpha=alpha, cap=cap, aliased=aliased, finalize=finalize),
        out_shape=(jax.ShapeDtypeStruct((n, d), F32), jax.ShapeDtypeStruct((n // tm, SUBLANES, tm), F32),
                   jax.ShapeDtypeStruct((n_experts, LANES), F32),
                   jax.ShapeDtypeStruct((n_experts * cap, ts, LANES), F32)),
        grid=(n // tm,),
        in_specs=in_specs,
        out_specs=(pl.BlockSpec((tm, d), row), pl.BlockSpec((1, SUBLANES, tm), lambda i: (i, 0, 0)),
                   pl.BlockSpec((n_experts, LANES), fixed), any_spec),
        scratch_shapes=[pltpu.VMEM((n_experts, LANES), F32), pltpu.VMEM((POST_SLOTS, tm, ts, LANES), F32),
                        pltpu.VMEM((POST_SLOTS, SUBLANES, tm), I32), pltpu.SMEM((POST_SLOTS, TOP_K, tm), I32),
                        pltpu.VMEM((MOE_ROWS, ts, LANES), F32), pltpu.VMEM((n_experts, LANES), I32),
                        pltpu.SMEM((n_experts, LANES), I32), pltpu.SemaphoreType.DMA((POST_SLOTS,)),
                        pltpu.SemaphoreType.DMA((POST_SLOTS,)), pltpu.SemaphoreType.DMA],
        input_output_aliases={len(args) - 1: 3} if aliased else {},
        compiler_params=_params(1), name="post_" + kind,
    )(*args)


def _dest_kernel(meta_ref, pstart_ref, o_ref):
    group, _, tmeta = meta_ref.shape
    n_experts = pstart_ref.shape[0]
    tm = o_ref.shape[2]
    per = tmeta // tm
    row = lax.broadcasted_iota(I32, (n_experts, tmeta), 0).astype(F32)
    for g in range(group):
        meta = meta_ref[g]
        rows = []
        for k in range(TOP_K):
            off = jnp.sum(jnp.where(row == meta[k:k + 1], pstart_ref[...], 0.0), axis=0, keepdims=True)
            rows.append(off + meta[TOP_K + k:TOP_K + k + 1])
        rows += [jnp.zeros_like(rows[0])] * (SUBLANES - TOP_K)
        dest = jnp.concatenate(rows, axis=0).astype(I32)
        for j in range(per):
            o_ref[g * per + j] = dest[:, j * tm:(j + 1) * tm]


def _dest(meta, pstart_b):
    ntile, _, tmeta = meta.shape
    tm = DMA_TILE
    per = tmeta // tm
    n_experts = pstart_b.shape[0]
    group = math.gcd(ntile, 8)
    return pl.pallas_call(
        _dest_kernel,
        out_shape=jax.ShapeDtypeStruct((ntile * per, SUBLANES, tm), I32),
        grid=(ntile // group,),
        in_specs=[pl.BlockSpec((group, SUBLANES, tmeta), lambda i: (i, 0, 0)),
                  pl.BlockSpec((n_experts, tmeta), lambda i: (0, 0))],
        out_specs=pl.BlockSpec((group * per, SUBLANES, tm), lambda i: (i, 0, 0)),
        compiler_params=_params(1), name="moe_dest",
    )(meta, pstart_b[:, :tmeta])


def _row_loop(tm, body):
    def step(it, c):
        for u in range(DMA_UNROLL):
            body(it * DMA_UNROLL + u, u)
        return c
    lax.fori_loop(0, tm // DMA_UNROLL, step, 0)


def _dispatch_kernel(lb_ref, dest_ref, tok_ref, *rest, fill):
    if fill:
        xs_ref, idx_scr, tok_scr, zero_scr, isem, lsem, rsem, zsem = rest
    else:
        _, xs_ref, idx_scr, tok_scr, zero_scr, isem, lsem, rsem, zsem = rest
    i = pl.program_id(0)
    nstep = pl.num_programs(0)
    tm = idx_scr.shape[2]
    slot = i % 2

    def load(step, to):
        return pltpu.make_async_copy(tok_ref.at[pl.ds(step * tm, tm)], tok_scr.at[to], lsem.at[to])

    @pl.when(i == 0)
    def _():
        load(0, 0).start()

    if fill:
        @pl.when(i == 0)
        def _():
            zero_scr[...] = jnp.zeros_like(zero_scr)

            def fill_copy(e):
                start_row = pl.multiple_of(jnp.maximum(lb_ref[e], 0), MOE_ROWS)
                return pltpu.make_async_copy(zero_scr, xs_ref.at[pl.ds(start_row, MOE_ROWS)], zsem)

            def start(e, c):
                @pl.when(lb_ref[e] >= 0)
                def _():
                    fill_copy(e).start()
                return c

            def wait(e, c):
                @pl.when(lb_ref[e] >= 0)
                def _():
                    fill_copy(e).wait()
                return c

            lax.fori_loop(0, lb_ref.shape[0], start, 0)
            lax.fori_loop(0, lb_ref.shape[0], wait, 0)

    for k in range(TOP_K):
        pltpu.make_async_copy(dest_ref.at[i, k], idx_scr.at[slot, k], isem).start()
    for k in range(TOP_K):
        pltpu.make_async_copy(dest_ref.at[i, k], idx_scr.at[slot, k], isem).wait()

    def row_copy(s, r, k):
        return pltpu.make_async_copy(tok_scr.at[s, r], xs_ref.at[idx_scr[s, k, r]], rsem.at[s])

    def start_row(r, u):
        for k in range(TOP_K):
            row_copy(slot, r, k).start(priority=(u + k) % 2)

    def wait_rows(s):
        def wait_row(r, u):
            for k in range(TOP_K):
                row_copy(s, r, k).wait()
        _row_loop(tm, wait_row)

    load(i, slot).wait()
    _row_loop(tm, start_row)

    @pl.when(i > 0)
    def _():
        wait_rows(1 - slot)

    @pl.when(i + 1 < nstep)
    def _():
        load(i + 1, 1 - slot).start()

    @pl.when(i == nstep - 1)
    def _():
        wait_rows(slot)


def _dispatch(last_blk, dest, tok, xs, p):
    n, ts, _ = tok.shape
    tm = DMA_TILE
    fill = xs is None
    any_spec = pl.BlockSpec(memory_space=pl.ANY)
    args = [last_blk, dest, tok] + ([] if fill else [xs])
    return pl.pallas_call(
        functools.partial(_dispatch_kernel, fill=fill),
        out_shape=jax.ShapeDtypeStruct((p, ts, LANES), F32),
        grid_spec=pltpu.PrefetchScalarGridSpec(
            num_scalar_prefetch=1, grid=(n // tm,), in_specs=[any_spec] * (len(args) - 1), out_specs=any_spec,
            scratch_shapes=[pltpu.SMEM((2, TOP_K, tm), I32), pltpu.VMEM((2, tm, ts, LANES), F32),
                            pltpu.VMEM((MOE_ROWS, ts, LANES), F32), pltpu.SemaphoreType.DMA,
                            pltpu.SemaphoreType.DMA((2,)), pltpu.SemaphoreType.DMA((2,)), pltpu.SemaphoreType.DMA]),
        input_output_aliases={} if fill else {3: 0},
        compiler_params=_params(1), name="moe_dispatch",
    )(*args)


def _expert_kernel(be_ref, nu_ref, br_ref, x_ref, w1_ref, w3_ref, w2_ref, y_ref, w1_scr, w3_scr, w2_scr):
    del br_ref
    j = pl.program_id(0)
    e = be_ref[j]
    e_prev = be_ref[jnp.maximum(j - 1, 0)]

    @pl.when(j < nu_ref[0])
    def _():
        @pl.when((j == 0) | (e != e_prev))
        def _():
            w1_scr[...] = w1_ref[0, 0].astype(BF16)
            w3_scr[...] = w3_ref[0, 0].astype(BF16)
            w2_scr[...] = w2_ref[0, 0].astype(BF16)

        x = _tiles_to_rows(x_ref[...]).astype(BF16)
        hid = _silu(_dot(x, w1_scr[...])) * _dot(x, w3_scr[...])
        y_ref[...] = _rows_to_tiles(_dot(hid.astype(BF16), w2_scr[...]))

    @pl.when(j >= nu_ref[0])
    def _():
        y_ref[...] = jnp.zeros_like(y_ref)


def _experts(blk_e, n_used, blk_row, xs, nblk, w1, w3, w2, layer):
    _, ts, _ = xs.shape
    d = ts * LANES
    de = w1.shape[-1]
    wmap = lambda j, be, nu, br: (layer, be[j], 0, 0)
    xmap = lambda j, be, nu, br: (br[jnp.minimum(j, nu[0] - 1)], 0, 0)
    return pl.pallas_call(
        _expert_kernel,
        out_shape=jax.ShapeDtypeStruct((nblk * MOE_ROWS, ts, LANES), F32),
        grid_spec=pltpu.PrefetchScalarGridSpec(
            num_scalar_prefetch=3, grid=(nblk,),
            in_specs=[pl.BlockSpec((MOE_ROWS, ts, LANES), xmap),
                      pl.BlockSpec((1, 1, d, de), wmap), pl.BlockSpec((1, 1, d, de), wmap),
                      pl.BlockSpec((1, 1, de, d), wmap)],
            out_specs=pl.BlockSpec((MOE_ROWS, ts, LANES), lambda j, be, nu, br: (j, 0, 0)),
            scratch_shapes=[pltpu.VMEM((d, de), BF16), pltpu.VMEM((d, de), BF16), pltpu.VMEM((de, d), BF16)]),
        compiler_params=_params(1), name="moe_experts",
    )(blk_e, n_used, blk_row, xs, w1, w3, w2)


def _combine_kernel(dest_ref, ys_ref, meta_ref, h_ref, g2_ref, lg_ref, lb_ref, o_ref,
                    idx_scr, buf_scr, isem, rsem, *, alpha):
    i = pl.program_id(0)
    nstep = pl.num_programs(0)
    tm = h_ref.shape[0]

    def gather(step, slot):
        for k in range(TOP_K):
            pltpu.make_async_copy(dest_ref.at[step, k], idx_scr.at[slot, k], isem).start()
        for k in range(TOP_K):
            pltpu.make_async_copy(dest_ref.at[step, k], idx_scr.at[slot, k], isem).wait()

        def start_row(r, u):
            for k in range(TOP_K):
                pltpu.make_async_copy(ys_ref.at[idx_scr[slot, k, r]], buf_scr.at[slot, k, r],
                                      rsem.at[slot]).start(priority=(u + k) % 2)

        _row_loop(tm, start_row)

    depth = COMBINE_SLOTS - 1
    slot = i % COMBINE_SLOTS

    @pl.when(i == 0)
    def _():
        for a in range(depth):
            @pl.when(a < nstep)
            def _():
                gather(a, a)

    @pl.when(i + depth < nstep)
    def _():
        gather(i + depth, (i + depth) % COMBINE_SLOTS)

    def wait_row(r, u):
        for k in range(TOP_K):
            pltpu.make_async_copy(ys_ref.at[idx_scr[slot, k, r]], buf_scr.at[slot, k, r], rsem.at[slot]).wait()

    _row_loop(tm, wait_row)
    meta = meta_ref[0]
    wt = jnp.concatenate([meta, jnp.zeros((LANES - SUBLANES, tm), F32)], axis=0).T
    f = wt[:, 2 * TOP_K:2 * TOP_K + 1] * _tiles_to_rows(buf_scr[slot, 0])
    for k in range(1, TOP_K):
        f = f + wt[:, 2 * TOP_K + k:2 * TOP_K + k + 1] * _tiles_to_rows(buf_scr[slot, k])
    o_ref[...] = _ln(alpha * h_ref[...] + g2_ref[0] * f, lg_ref[...], lb_ref[...])


def _combine(dest, ys, meta, h, g2, lg, lb, seq, alpha):
    n, d = h.shape
    tm = DMA_TILE
    tps = seq // tm
    per = meta.shape[2] // tm
    row = lambda i: (i, 0)
    fixed = lambda i: (0, 0)
    return pl.pallas_call(
        functools.partial(_combine_kernel, alpha=alpha),
        out_shape=jax.ShapeDtypeStruct((n, d), F32),
        grid=(n // tm,),
        in_specs=[pl.BlockSpec(memory_space=pl.ANY), pl.BlockSpec(memory_space=pl.ANY),
                  pl.BlockSpec((1, SUBLANES, tm), lambda i: (i // per, 0, i % per)), pl.BlockSpec((tm, d), row),
                  pl.BlockSpec((1, 1, d), lambda i: (i // tps, 0, 0)),
                  pl.BlockSpec((1, d), fixed), pl.BlockSpec((1, d), fixed)],
        out_specs=pl.BlockSpec((tm, d), row),
        scratch_shapes=[pltpu.SMEM((COMBINE_SLOTS, TOP_K, tm), I32),
                        pltpu.VMEM((COMBINE_SLOTS, TOP_K, tm, d // LANES, LANES), F32),
                        pltpu.SemaphoreType.DMA, pltpu.SemaphoreType.DMA((COMBINE_SLOTS,))],
        compiler_params=_params(1), name="moe_combine",
    )(dest, ys, meta, h, g2, lg, lb)


def _moe(streams, counts, xs, cap, w1, w3, w2, layer, alpha):
    n_experts = w1.shape[1]
    n_assign = TOP_K * sum(s["h1"].shape[0] for s in streams)
    nblk = n_assign // MOE_ROWS + n_experts
    cnt = counts[:, 0].astype(I32)
    padded = (cnt + MOE_ROWS - 1) // MOE_ROWS * MOE_ROWS
    pend = jnp.cumsum(padded)
    pstart = pend - padded
    n_used = (pend[-1] // MOE_ROWS).astype(I32).reshape(1)
    blk = jnp.arange(nblk, dtype=I32) * MOE_ROWS
    blk_e = jnp.sum((pend[None, :] <= blk[:, None]).astype(I32), axis=1)
    last_e = jnp.sum((pend <= jnp.maximum(pend[-1] - 1, 0)).astype(I32))
    blk_e = jnp.minimum(blk_e, last_e).astype(I32)
    blk_row = ((blk_e * cap + blk - pstart[blk_e]) // MOE_ROWS).astype(I32)
    pstart_b = jnp.broadcast_to(pstart.astype(F32)[:, None], (n_experts, ROW_TILE))
    dests = [_dest(s["meta"], pstart_b) for s in streams]
    ys = _experts(blk_e, n_used, blk_row, xs, nblk, w1, w3, w2, layer)
    return [_combine(dst, ys, s["meta"], s["h1"], s["g2"], s["lg"], s["lb"], s["seq"], alpha)
            for s, dst in zip(streams, dests)]


def _log_sigmoid(x):
    return jnp.minimum(x, 0.0) - jnp.log(1.0 + jnp.exp(-jnp.abs(x)))


def _gla_in_kernel(h_ref, sc_ref, sh_ref, w_ref, wl_ref, wg_ref, bg_ref, qk_ref, v_ref, r_ref, g_ref, *, qscale):
    tm, d = h_ref.shape
    a = (h_ref[...] * (1.0 + sc_ref[0]) + sh_ref[0]).astype(BF16)
    qk = _dot(a, w_ref[:, :d])
    lane = lax.broadcasted_iota(I32, (tm, d), 1)
    qk_ref[...] = jnp.where(lane < d // 2, qk * qscale, qk)
    v_ref[...] = _dot(a, w_ref[:, d:2 * d]).astype(BF16)
    r_ref[...] = _dot(a, w_ref[:, 2 * d:])
    lr = _dot(a, wl_ref[...])
    gpre = _dot(lr.astype(BF16), wg_ref[...]) + bg_ref[...]
    g_ref[...] = _log_sigmoid(gpre) * (1.0 / GLA_TAU)


def _gla_weights(w_in, w_gate, b_gate, d):
    dk = d // 2
    n_lr = HY_DIRS * GLA_RANK
    w_main = w_in[:, :2 * dk + 2 * d].astype(BF16)
    w_lr = jnp.zeros((d, LANES), F32).at[:, :n_lr].set(w_in[:, 2 * dk + 2 * d:]).astype(BF16)
    w_g = jnp.zeros((LANES, HY_DIRS * dk), F32)
    for dr in range(HY_DIRS):
        w_g = w_g.at[dr * GLA_RANK:(dr + 1) * GLA_RANK, dr * dk:(dr + 1) * dk].set(w_gate[dr])
    return w_main, w_lr, w_g.astype(BF16), b_gate.reshape(1, HY_DIRS * dk)


def _gla_in(h, sc, sh, w_main, w_lr, w_g, b_g, seq):
    n, d = h.shape
    tm = min(ROW_TILE, seq)
    tps = seq // tm
    row = lambda i: (i, 0)
    per_b = lambda i: (i // tps, 0, 0)
    fixed = lambda i: (0, 0)
    qscale = float(((d // 2) // GLA_HEADS) ** -0.5)
    return pl.pallas_call(
        functools.partial(_gla_in_kernel, qscale=qscale),
        out_shape=(jax.ShapeDtypeStruct((n, d), F32), jax.ShapeDtypeStruct((n, d), BF16),
                   jax.ShapeDtypeStruct((n, d), F32), jax.ShapeDtypeStruct((n, d), F32)),
        grid=(n // tm,),
        in_specs=[pl.BlockSpec((tm, d), row), pl.BlockSpec((1, 1, d), per_b), pl.BlockSpec((1, 1, d), per_b),
                  pl.BlockSpec((d, 3 * d), fixed), pl.BlockSpec((d, LANES), fixed),
                  pl.BlockSpec((LANES, d), fixed), pl.BlockSpec((1, d), fixed)],
        out_specs=tuple(pl.BlockSpec((tm, d), row) for _ in range(4)),
        compiler_params=_params(1), name="gla_in",
    )(h, sc, sh, w_main, w_lr, w_g, b_g)


def _gla_direction(q_ref, k_ref, v_ref, g_ref, st_scr, o_ref, reverse):
    rt, dk = q_ref.shape
    cs = GLA_CHUNK
    nch = rt // cs
    g = g_ref[...]
    pos = lax.broadcasted_iota(I32, (rt, dk), 0) % cs
    b = g
    sh = 1
    while sh < cs:
        if reverse:
            b = b + jnp.where(pos < cs - sh, pltpu.roll(b, rt - sh, 0), 0.0)
        else:
            b = b + jnp.where(pos >= sh, pltpu.roll(b, sh, 0), 0.0)
        sh *= 2
    edge = 0 if reverse else cs - 1
    b3 = b.reshape(nch, cs, dk)
    b_edge = b3[:, edge:edge + 1, :]
    q = q_ref[...]
    k = k_ref[...]
    qe = (q * jnp.exp(b)).astype(BF16)
    ke = (k * jnp.exp(-b)).astype(BF16)
    kd = (k.reshape(nch, cs, dk) * jnp.exp(b_edge - b3)).astype(BF16)
    decay = jnp.exp(b_edge)
    r_i = lax.broadcasted_iota(I32, (rt, rt), 0)
    c_i = lax.broadcasted_iota(I32, (rt, rt), 1)
    same = (r_i // cs) == (c_i // cs)
    tri = (c_i >= r_i) if reverse else (c_i <= r_i)
    att = jnp.where(same & tri, _dot_nt(qe, ke), 0.0).astype(BF16)
    v = v_ref[...]
    intra = _dot(att, v)
    st = st_scr[...]
    order = range(nch - 1, -1, -1) if reverse else range(nch)
    for j in order:
        rows = slice(j * cs, (j + 1) * cs)
        o_ref[rows, :] = intra[rows, :] + _dot_nt(qe[rows, :], st.astype(BF16))
        st = st * decay[j] + _dot_tn(v[rows, :], kd[j])
    st_scr[...] = st


def _gla_scan_kernel(qkf, vf, gf, qkb, vb, gb, s0f, s0b, of, ob, sfo, sbo, sf_scr, sb_scr):
    c = pl.program_id(1)
    nh = GLA_HEADS
    d = qkf.shape[1]
    dkh, dvh = (d // 2) // nh, d // nh

    @pl.when(c == 0)
    def _():
        sf_scr[...] = s0f[0]
        sb_scr[...] = s0b[0]

    for hd in range(nh):
        qs, ks, vs = pl.ds(hd * dkh, dkh), pl.ds(d // 2 + hd * dkh, dkh), pl.ds(hd * dvh, dvh)
        _gla_direction(qkf.at[:, qs], qkf.at[:, ks], vf.at[:, vs], gf.at[:, qs], sf_scr.at[hd], of.at[:, vs], False)
        _gla_direction(qkb.at[:, qs], qkb.at[:, ks], vb.at[:, vs], gb.at[:, ks], sb_scr.at[hd], ob.at[:, vs], True)
    sfo[0] = sf_scr[...]
    sbo[0] = sb_scr[...]


def _gla_scan(qk, v, r, g, s0f, s0b, nb, seq):
    del r
    n, d = qk.shape
    nh = GLA_HEADS
    dkh, dvh = (d // 2) // nh, d // nh
    rt = min(ROW_TILE, seq)
    npb = seq // rt
    fw = pl.BlockSpec((rt, d), lambda b, c: (b * npb + c, 0))
    bw = pl.BlockSpec((rt, d), lambda b, c: (b * npb + npb - 1 - c, 0))
    st = pl.BlockSpec((1, nh, dvh, dkh), lambda b, c: (b, 0, 0, 0))
    state = jax.ShapeDtypeStruct((nb, nh, dvh, dkh), F32)
    return pl.pallas_call(
        _gla_scan_kernel,
        out_shape=(jax.ShapeDtypeStruct((n, d), F32), jax.ShapeDtypeStruct((n, d), F32), state, state),
        grid=(nb, npb),
        in_specs=[fw, fw, fw, bw, bw, bw, st, st],
        out_specs=(fw, bw, st, st),
        scratch_shapes=[pltpu.VMEM((nh, dvh, dkh), F32), pltpu.VMEM((nh, dvh, dkh), F32)],
        compiler_params=_params(2), name="gla_scan",
    )(qk, v, g, qk, v, g, s0f, s0b)


def _hy_in_kernel(h_ref, hp_ref, hn_ref, sc_ref, sh_ref, w_ref, b_ref, cw_ref, cb_ref, o_ref, a_scr, z_scr,
                  *, tps, halo):
    i = pl.program_id(0)
    tm, d = h_ref.shape
    sc = 1.0 + sc_ref[0]
    sh = sh_ref[0]
    a_scr[0:halo, :] = (hp_ref[...] * sc + sh).astype(BF16)
    a_scr[halo:halo + tm, :] = (h_ref[...] * sc + sh).astype(BF16)
    a_scr[halo + tm:, :] = (hn_ref[...] * sc + sh).astype(BF16)
    row = lax.broadcasted_iota(I32, (tm, 1), 0)
    first = jnp.where(i % tps == 0, 0, -1)
    last = jnp.where(i % tps == tps - 1, tm - 1, -1)
    for c in range(o_ref.shape[0]):
        cols = slice(c * d, (c + 1) * d)
        z_scr[...] = _dot(a_scr[...], w_ref[:, cols]) + b_ref[:, cols]
        zp = jnp.where(row == first, 0.0, z_scr[pl.ds(halo - 1, tm), :])
        zc = z_scr[pl.ds(halo, tm), :]
        zn = jnp.where(row == last, 0.0, z_scr[pl.ds(halo + 1, tm), :])
        o_ref[c] = cw_ref[0:1, cols] * zp + cw_ref[1:2, cols] * zc + cw_ref[2:3, cols] * zn + cb_ref[:, cols]


def _hy_in(h, sc, sh, w, b, cw, cb, seq):
    n, d = h.shape
    nsplit = w.shape[1] // d
    halo = 16
    tm = min(ROW_TILE, seq)
    tps = seq // tm
    hb = tm // halo
    row = lambda i: (i, 0)
    per_b = lambda i: (i // tps, 0, 0)
    fixed = lambda i: (0, 0)
    return pl.pallas_call(
        functools.partial(_hy_in_kernel, tps=tps, halo=halo),
        out_shape=jax.ShapeDtypeStruct((nsplit, n, d), F32),
        grid=(n // tm,),
        in_specs=[pl.BlockSpec((tm, d), row),
                  pl.BlockSpec((halo, d), lambda i: (jnp.maximum(i * hb - 1, 0), 0)),
                  pl.BlockSpec((halo, d), lambda i: (jnp.minimum((i + 1) * hb, n // halo - 1), 0)),
                  pl.BlockSpec((1, 1, d), per_b), pl.BlockSpec((1, 1, d), per_b),
                  pl.BlockSpec((d, nsplit * d), fixed), pl.BlockSpec((1, nsplit * d), fixed),
                  pl.BlockSpec((cw.shape[0], nsplit * d), fixed), pl.BlockSpec((1, nsplit * d), fixed)],
        out_specs=pl.BlockSpec((nsplit, tm, d), lambda i: (0, i, 0)),
        scratch_shapes=[pltpu.VMEM((tm + 2 * halo, d), BF16), pltpu.VMEM((tm + 2 * halo, d), F32)],
        compiler_params=_params(1), name="hyena_in",
    )(h, h, h, sc, sh, w, b, cw, cb)


def _hy_filter_rows(t, seq, w1_ref, b1_ref, w2_ref, b2_ref, w3_ref, b3_ref, n_bands):
    rows = t.shape[0]
    tf = t.astype(F32)
    t_lin = tf * (1.0 / (seq - 1))
    wpos = tf * (2.0 * math.pi / seq)
    lane = lax.broadcasted_iota(I32, (rows, LANES), 1)
    jb = jnp.where(lane > n_bands, lane - n_bands - 1, lane - 1).astype(F32)
    band = 1e-4 + jb * ((n_bands - 1 - 1e-4) / (n_bands - 1))
    ang = band * wpos
    z = jnp.where(lane == 0, t_lin,
                  jnp.where(lane <= n_bands, jnp.cos(ang), jnp.where(lane <= 2 * n_bands, -jnp.sin(ang), 0.0)))
    hf = jnp.sin(_dot(z.astype(BF16), w1_ref[...]) + b1_ref[...])
    hf = jnp.sin(_dot(hf.astype(BF16), w2_ref[...]) + b2_ref[...])
    hf = jnp.sin(_dot(hf.astype(BF16), w3_ref[...]) + b3_ref[...])
    return hf.astype(BF16), t_lin


def _hy_taps_kernel(w1_ref, b1_ref, w2_ref, b2_ref, w3_ref, b3_ref, w4_ref, o_ref, *, seq, n_bands):
    i = pl.program_id(0)
    n_ord, tr, d = o_ref.shape
    mlp = (w1_ref, b1_ref, w2_ref, b2_ref, w3_ref, b3_ref)
    r0 = i * tr
    second = (r0 >= seq).astype(I32)
    n_idx = r0 + lax.broadcasted_iota(I32, (tr, 1), 0)
    t = jnp.where(second == 1, 2 * seq - n_idx, n_idx)
    hf, t_lin = _hy_filter_rows(t, seq, *mlp, n_bands)
    c_idx = lax.broadcasted_iota(I32, (1, d), 1).astype(F32)
    min_decay = math.log(HY_DECAY_TARGET) / HY_DECAY_LONG_PCT
    max_decay = math.log(HY_DECAY_TARGET) / HY_DECAY_SHORT_PCT
    delta = jnp.abs(min_decay + c_idx * ((max_decay - min_decay) / (d - 1)))
    window = jnp.where(n_idx == seq, 0.0, jnp.exp(-t_lin * delta))
    for o in range(n_ord):
        o_ref[o] = _dot(hf, w4_ref[second, o]) * window

    @pl.when(i == 0)
    def _():
        hf0, _ = _hy_filter_rows(jnp.zeros((8, 1), I32), seq, *mlp, n_bands)
        first = lax.broadcasted_iota(I32, (8, 1), 0) == 0
        for o in range(n_ord):
            o_ref[o, 0:8, :] = o_ref[o, 0:8, :] + jnp.where(first, _dot(hf0, w4_ref[1, o]), 0.0)


def _hy_taps(seq, d, w1, b1, w2, b2, w3, b3, w4):
    emb, ff = w1.shape
    n_bands = (emb - 1) // 2
    w1p = jnp.zeros((LANES, ff), F32).at[:emb].set(w1).astype(BF16)
    w4r = w4.reshape(ff, HY_ORDER, HY_DIRS, d).transpose(2, 1, 0, 3).astype(BF16)
    tr = min(ROW_TILE, seq)
    fixed = lambda i: (0, 0)
    return pl.pallas_call(
        functools.partial(_hy_taps_kernel, seq=seq, n_bands=n_bands),
        out_shape=jax.ShapeDtypeStruct((HY_ORDER, 2 * seq, d), F32),
        grid=(2 * seq // tr,),
        in_specs=[pl.BlockSpec((LANES, ff), fixed), pl.BlockSpec((1, ff), fixed),
                  pl.BlockSpec((ff, ff), fixed), pl.BlockSpec((1, ff), fixed),
                  pl.BlockSpec((ff, ff), fixed), pl.BlockSpec((1, ff), fixed),
                  pl.BlockSpec((HY_DIRS, HY_ORDER, ff, d), lambda i: (0, 0, 0, 0))],
        out_specs=pl.BlockSpec((HY_ORDER, tr, d), lambda i: (0, i, 0)),
        compiler_params=_params(1), name="hyena_taps",
    )(w1p, b1.reshape(1, ff), w2.astype(BF16), b2.reshape(1, ff), w3.astype(BF16), b3.reshape(1, ff), w4r)


def _cos_sin(num, den):
    ang = (num % den).astype(F32) * (2.0 * math.pi / den)
    return jnp.cos(ang), jnp.sin(ang)


def _fft_tables(seq, n2):
    n = 2 * seq
    n1 = n // n2
    half = seq // n2
    ar = jnp.arange(n1, dtype=I32)
    ca, sa = _cos_sin(ar[:, None] * ar[None, :], n1)
    fa_half = jnp.concatenate([ca[:, :half], sa[:, :half]], 0).astype(BF16)
    fa_full = jnp.concatenate([ca, sa], 0).astype(BF16)
    ia = (jnp.concatenate([ca[:half], sa[:half]], 0) * (1.0 / n)).astype(BF16)
    k = ar[:, None, None] + n1 * jnp.arange(n2, dtype=I32)[None, :, None]
    cb, sb = _cos_sin(k * jnp.arange(n2, dtype=I32)[None, None, :], n)
    fb = jnp.concatenate([cb, sb], 1).astype(BF16)
    gb = jnp.concatenate([cb.transpose(0, 2, 1), sb.transpose(0, 2, 1)], 1).astype(BF16)
    return fa_half, fa_full, ia, fb, gb


def _pack_c(re, im):
    hi = pltpu.bitcast(re.astype(BF16).astype(F32), U32)
    lo = pltpu.bitcast(im.astype(BF16).astype(F32), U32)
    return hi | (lo >> 16)


def _unpack_c(w):
    re = pltpu.bitcast(w & jnp.uint32(0xFFFF0000), F32)
    im = pltpu.bitcast(w << 16, F32)
    return jnp.concatenate([re, im], axis=1).astype(BF16)


def _cmul_split(p, rows, cols, conj):
    a, b, c, d = p[:rows, :cols], p[:rows, cols:], p[rows:, :cols], p[rows:, cols:]
    return (a - d, b + c) if conj else (a + d, b - c)


def _fft_a_kernel(x_ref, f_ref, o_ref, x_scr, *, cplx):
    n1 = f_ref.shape[0] // 2
    per, rows, g, d = x_ref.shape[1:]
    for b in range(per):
        xb = x_ref[0, b].reshape(rows // SUBLANES, SUBLANES, g, d)
        x_scr[b] = jnp.swapaxes(xb, 1, 2)
    for s in range(g):
        if cplx:
            z = jnp.concatenate([x_scr[b, :, s].reshape(rows, d) for b in range(per)], axis=1).astype(BF16)
            ar, ai = _cmul_split(_dot(f_ref[...], z), n1, d, False)
        else:
            p = _dot(f_ref[...], x_scr[0, :, s].reshape(rows, d).astype(BF16))
            ar, ai = p[:n1], -p[n1:]
        o_ref[0, :, s, :] = _pack_c(ar, ai)


def _fft_a(x5, which, f, cplx):
    _, nb, rows, n2, d = x5.shape
    n1 = f.shape[0] // 2
    per = 2 if cplx else 1
    groups = nb // per
    g = FFT_GROUP
    return pl.pallas_call(
        functools.partial(_fft_a_kernel, cplx=cplx),
        out_shape=jax.ShapeDtypeStruct((groups, n1, n2, d), U32),
        grid=(groups, n2 // g),
        in_specs=[pl.BlockSpec((1, per, rows, g, d), lambda p, j: (which, p, 0, j, 0)),
                  pl.BlockSpec((2 * n1, rows), lambda p, j: (0, 0))],
        out_specs=pl.BlockSpec((1, n1, g, d), lambda p, j: (p, 0, j, 0)),
        scratch_shapes=[pltpu.VMEM((per, rows // SUBLANES, g, SUBLANES, d), F32)],
        compiler_params=_params(2), name="fft_stage_a",
    )(x5, f)


def _fft_b_kernel(*refs, conv):
    if conv:
        a_ref, fb_ref, k_ref, gb_ref, o_ref = refs
    else:
        a_ref, fb_ref, o_ref = refs
    g, n2, d = a_ref.shape[1:]
    for s in range(g):
        xr, xi = _cmul_split(_dot(fb_ref[s], _unpack_c(a_ref[0, s])), n2, d, False)
        if not conv:
            o_ref[0, 0, s] = xr
            o_ref[0, 1, s] = xi
            continue
        kr, ki = k_ref[0, 0, s], k_ref[0, 1, s]
        y = jnp.concatenate([xr * kr - xi * ki, xr * ki + xi * kr], axis=1).astype(BF16)
        yr, yi = _cmul_split(_dot(gb_ref[s], y), n2, d, True)
        o_ref[0, :, s, :] = _pack_c(yr, yi)


def _fft_b(a4, fb, kspec=None, which=0, gb=None):
    groups, n1, n2, d = a4.shape
    conv = kspec is not None
    g = FFT_GROUP
    slab = pl.BlockSpec((1, g, n2, d), lambda k, p: (p, k, 0, 0))
    tab = pl.BlockSpec((g, 2 * n2, n2), lambda k, p: (k, 0, 0))
    if conv:
        in_specs = [slab, tab, pl.BlockSpec((1, 2, g, n2, d), lambda k, p: (which, 0, k, 0, 0)), tab]
        args = [a4, fb, kspec, gb]
        out_shape = jax.ShapeDtypeStruct((groups, n2, n1, d), U32)
        out_spec = pl.BlockSpec((1, n2, g, d), lambda k, p: (p, 0, k, 0))
    else:
        in_specs = [slab, tab]
        args = [a4, fb]
        out_shape = jax.ShapeDtypeStruct((groups, 2, n1, n2, d), F32)
        out_spec = pl.BlockSpec((1, 2, g, n2, d), lambda k, p: (p, 0, k, 0, 0))
    return pl.pallas_call(
        functools.partial(_fft_b_kernel, conv=conv),
        out_shape=out_shape,
        grid=(n1 // g, groups),
        in_specs=in_specs, out_specs=out_spec,
        compiler_params=_params(2), name="fft_stage_b_conv" if conv else "fft_stage_b",
    )(*args)


def _fft_c_kernel(y_ref, ia_ref, u_ref, g_ref, skip_ref, o_ref):
    half = ia_ref.shape[0] // 2
    g, _, d = y_ref.shape[1:]
    for s in range(g):
        cr, ci = _cmul_split(_dot(ia_ref[...], _unpack_c(y_ref[0, s])), half, d, True)
        o_ref[0, :, s, :] = cr
        o_ref[1, :, s, :] = ci
    skip = skip_ref[...][None]
    for b in range(2):
        o_ref[b] = g_ref[0, b] * (o_ref[b] + u_ref[0, b] * skip)


def _fft_c(y4, ia, u5, u_which, g5, g_which, skip):
    groups, n2, n1, d = y4.shape
    half = ia.shape[0] // 2
    g = FFT_GROUP
    return pl.pallas_call(
        _fft_c_kernel,
        out_shape=jax.ShapeDtypeStruct((2 * groups, half, n2, d), F32),
        grid=(groups, n2 // g),
        in_specs=[pl.BlockSpec((1, g, n1, d), lambda p, j: (p, j, 0, 0)),
                  pl.BlockSpec((2 * half, n1), lambda p, j: (0, 0)),
                  pl.BlockSpec((1, 2, half, g, d), lambda p, j: (u_which, p, 0, j, 0)),
                  pl.BlockSpec((1, 2, half, g, d), lambda p, j: (g_which, p, 0, j, 0)),
                  pl.BlockSpec((1, d), lambda p, j: (0, 0))],
        out_specs=pl.BlockSpec((2, half, g, d), lambda p, j: (p, 0, j, 0)),
        compiler_params=_params(2), name="fft_stage_c",
    )(y4, ia, u5, g5, skip.reshape(1, d))


def _hy_long_conv(zs, taps, skip, nb, seq):
    nsplit, n, d = zs.shape
    n2 = FFT_N2
    half = seq // n2
    n1 = 2 * half
    fa_half, fa_full, ia, fb, gb = _fft_tables(seq, n2)
    kspec = _fft_b(_fft_a(taps.reshape(1, HY_ORDER, n1, n2, d), 0, fa_full, False), fb)
    zs5 = zs.reshape(nsplit, nb, half, n2, d)
    u5, u_which = zs5, 0
    for o in range(HY_ORDER):
        a = _fft_a(u5, u_which, fa_half, True)
        y = _fft_b(a, fb, kspec, o, gb)
        out = _fft_c(y, ia, u5, u_which, zs5, o + 1, skip[o])
        u5, u_which = out[None], 0
    return out.reshape(n, d)


def _short_conv_kernel(u_ref, g_ref, t_ref, skip_ref, ff_ref, tf_ref, fi_ref, o_ref):
    seq, ct = u_ref.shape[-2:]
    n = 2 * seq
    z = jnp.concatenate([u_ref[0, 0], u_ref[0, 1]], axis=1).astype(BF16)
    xr, xi = _cmul_split(_dot(ff_ref[...], z), n, ct, False)
    pt = _dot(tf_ref[...], t_ref[0].astype(BF16))
    kr, ki = pt[:n], -pt[n:]
    y = jnp.concatenate([xr * kr - xi * ki, xr * ki + xi * kr], axis=1).astype(BF16)
    cr, ci = _cmul_split(_dot(fi_ref[...], y), seq, ct, True)
    skip = skip_ref[...]
    o_ref[0] = g_ref[0, 0] * (cr + u_ref[0, 0] * skip)
    o_ref[1] = g_ref[0, 1] * (ci + u_ref[0, 1] * skip)


def _hy_short_conv(zs, taps, skip, nb, seq):
    nsplit, n, d = zs.shape
    nn = 2 * seq
    ar = jnp.arange(nn, dtype=I32)
    cm, sm = _cos_sin(ar[:, None] * ar[None, :], nn)
    ff = jnp.concatenate([cm[:, :seq], sm[:, :seq]], 0).astype(BF16)
    tf = jnp.concatenate([cm, sm], 0).astype(BF16)
    fi = (jnp.concatenate([cm[:seq], sm[:seq]], 0) * (1.0 / nn)).astype(BF16)
    ct = d // 2
    zs4 = zs.reshape(nsplit, nb, seq, d)
    u4, u_which = zs4, 0
    for o in range(HY_ORDER):
        out = pl.pallas_call(
            _short_conv_kernel,
            out_shape=jax.ShapeDtypeStruct((nb, seq, d), F32),
            grid=(nb // 2, d // ct),
            in_specs=[pl.BlockSpec((1, 2, seq, ct), lambda p, j, w=u_which: (w, p, 0, j)),
                      pl.BlockSpec((1, 2, seq, ct), lambda p, j, w=o + 1: (w, p, 0, j)),
                      pl.BlockSpec((1, nn, ct), lambda p, j, w=o: (w, 0, j)),
                      pl.BlockSpec((1, ct), lambda p, j: (0, j)),
                      pl.BlockSpec((2 * nn, seq), lambda p, j: (0, 0)),
                      pl.BlockSpec((2 * nn, nn), lambda p, j: (0, 0)),
                      pl.BlockSpec((2 * seq, nn), lambda p, j: (0, 0))],
            out_specs=pl.BlockSpec((2, seq, ct), lambda p, j: (p, 0, j)),
            compiler_params=_params(2), name="hyena_short_conv",
        )(u4, zs4, taps, skip[o].reshape(1, d), ff, tf, fi)
        u4, u_which = out[None], 0
    return out.reshape(n, d)


def kernel(x, c, ctx, c_ctx, w_mod, b_mod, ln_g, ln_b, hy_w_in, hy_b_in, hy_conv_w, hy_conv_b, hy_f_w1, hy_f_b1, hy_f_w2, hy_f_b2, hy_f_w3, hy_f_b3, hy_f_w4, hy_skip, hy_w_out, hy_b_out, gla_w_in, gla_w_gate, gla_b_gate, gla_norm_g, gla_w_out, gm_w_in, gm_b_in, gm_ln_g, gm_ln_b, gm_ws, gm_bs, gm_w_out, gm_b_out, router_w, router_b, moe_w1, moe_w3, moe_w2):
    B, L, D = x.shape
    Lc = ctx.shape[1]
    depth = w_mod.shape[0]
    E = router_w.shape[1]
    alpha = (2.0 * depth) ** 0.25
    gla_layers = list(range(1, depth, N_MIXERS))
    last_ctx = gla_layers[-1] if gla_layers else -1

    cvec = jnp.zeros((8, D), F32).at[:B].set(c).at[B].set(c_ctx)
    mod = _modulation(cvec, w_mod, b_mod)
    h = _add_pos(x.reshape(B * L, D), L)
    hc = ctx.reshape(B * Lc, D)
    rw_t = router_w.T
    rw_hi = rw_t.astype(BF16)
    rhl = jnp.concatenate([rw_hi, (rw_t - rw_hi.astype(F32)).astype(BF16)], axis=0)
    rb_b = jnp.broadcast_to(router_b[:, None], (E, ROW_TILE))
    zero_cnt = jnp.zeros((E, LANES), F32)
    row2 = lambda v: v.reshape(1, -1)

    for i in range(depth):
        kind, j = i % N_MIXERS, i // N_MIXERS
        ctx_full = i < last_ctx
        ctx_any = i <= last_ctx
        lat = [mod[i, :B, k * D:(k + 1) * D].reshape(B, 1, D) for k in range(6)]
        cm = [jnp.broadcast_to(mod[i, B, k * D:(k + 1) * D].reshape(1, 1, D), (B, 1, D)) for k in range(6)]
        streams_in = [(h, lat, L)] + ([(hc, cm, Lc)] if ctx_any else [])
        pre = []
        if kind == 0:
            w_in = hy_w_in[j].astype(BF16)
            for s_h, s_m, s_len in streams_in[:1 + int(ctx_full)]:
                zs = _hy_in(s_h, s_m[1], s_m[0], w_in, row2(hy_b_in[j]), hy_conv_w[j], row2(hy_conv_b[j]), s_len)
                taps = _hy_taps(s_len, D, hy_f_w1[j], hy_f_b1[j], hy_f_w2[j], hy_f_b2[j], hy_f_w3[j], hy_f_b3[j],
                                hy_f_w4[j])
                if s_len == L:
                    y2 = _hy_long_conv(zs, taps, hy_skip[j], B, s_len)
                else:
                    y2 = _hy_short_conv(zs, taps, hy_skip[j], B, s_len)
                pre.append(("hyena", (y2,), lambda tm, d: [pl.BlockSpec((tm, d), lambda t: (t, 0))]))
            w_out, b_out = hy_w_out[j].astype(BF16), row2(hy_b_out[j])
        elif kind == 1:
            w_main, w_lr, w_g, b_g = _gla_weights(gla_w_in[j], gla_w_gate[j], gla_b_gate[j], D)
            proj = [_gla_in(s_h, s_m[1], s_m[0], w_main, w_lr, w_g, b_g, s_len) for s_h, s_m, s_len in streams_in]
            dvh = D // GLA_HEADS
            dkh = (D // 2) // GLA_HEADS
            zero_state = jnp.zeros((B, GLA_HEADS, dvh, dkh), F32)
            if ctx_any:
                ocf, ocb, s_f, s_b = _gla_scan(*proj[1], zero_state, zero_state, B, Lc)
            else:
                s_f = s_b = zero_state
            o_f, o_b, _, _ = _gla_scan(*proj[0], s_f, s_b, B, L)
            gla_specs = lambda tm, d: [pl.BlockSpec((tm, d), lambda t: (t, 0))] * 3 + [pl.BlockSpec((1, d), lambda t: (0, 0))]
            pre.append(("gla", (o_f, o_b, proj[0][2], row2(gla_norm_g[j])), gla_specs))
            if ctx_full:
                pre.append(("gla", (ocf, ocb, proj[1][2], row2(gla_norm_g[j])), gla_specs))
            w_out, b_out = gla_w_out[j].astype(BF16), jnp.zeros((1, D), F32)
        else:
            w_in = gm_w_in[j].astype(BF16)
            ws = gm_ws[j].astype(BF16)
            bs_exp = jnp.repeat(gm_bs[j].T, D // GM_HEADS, axis=1)
            gm_specs = lambda tm, d: [pl.BlockSpec((tm, d), lambda t: (t, 0))] * 2 + [
                pl.BlockSpec((GM_HEADS, GM_CHUNK, GM_CHUNK), lambda t: (0, 0, 0)), pl.BlockSpec((GM_CHUNK, d), lambda t: (0, 0))]
            for s_h, s_m, s_len in streams_in[:1 + int(ctx_full)]:
                u, vn = _gm_in(s_h, s_m[1], s_m[0], w_in, row2(gm_b_in[j]), row2(gm_ln_g[j]), row2(gm_ln_b[j]), s_len)
                pre.append(("gmlp", (u, vn, ws, bs_exp), gm_specs))
            w_out, b_out = gm_w_out[j].astype(BF16), row2(gm_b_out[j])

        moe_streams = []
        cnt, xs = zero_cnt, None
        cap = sum(s_h.shape[0] for (s_h, _, _), _ in zip(streams_in, pre)) + MOE_ROWS
        for idx, ((s_h, s_m, s_len), (pk, pargs, pspecs)) in enumerate(zip(streams_in, pre)):
            h1, meta, cnt, xs = _post(pk, pargs, pspecs, s_h, w_out, b_out, s_m[2], row2(ln_g[i, 0]),
                                      row2(ln_b[i, 0]), s_m[4], s_m[3], rhl, rb_b, cnt, xs, cap,
                                      idx == len(pre) - 1, s_len, alpha)
            moe_streams.append(dict(h1=h1, meta=meta, g2=s_m[5], lg=row2(ln_g[i, 1]), lb=row2(ln_b[i, 1]),
                                    seq=s_len))
        outs = _moe(moe_streams, cnt, xs, cap, moe_w1, moe_w3, moe_w2, i, alpha)
        h = outs[0]
        if ctx_full:
            hc = outs[1]
    return h.reshape(B, L, D)
```

```python
import functools
import math

import jax
import jax.numpy as jnp
from jax import lax
from jax.experimental import pallas as pl
from jax.experimental.pallas import tpu as pltpu

F32 = jnp.float32
BF16 = jnp.bfloat16
I32 = jnp.int32
U32 = jnp.uint32
HIGHEST = lax.Precision.HIGHEST

GRID_W = 64
N_MIXERS = 3
LN_EPS = 1e-5
HY_ORDER = 2
HY_DIRS = 2
HY_DECAY_TARGET = 1e-2
HY_DECAY_SHORT_PCT = 0.3
HY_DECAY_LONG_PCT = 1.5
GLA_HEADS = 4
GLA_RANK = 16
GLA_TAU = 16.0
GLA_CHUNK = 64
GM_CHUNK = 128
GM_HEADS = 4
N_GROUPS = 4
TOP_K = 2

LANES = 128
SUBLANES = 8
V7X_VMEM_LIMIT_BYTES = 56 * 1024 * 1024
MOE_ROWS = 512
ROW_TILE = 512
DMA_TILE = 256
DMA_UNROLL = 8
COMBINE_SLOTS = 2
POST_SLOTS = 3
FFT_N2 = 128
FFT_GROUP = SUBLANES


def _params(n_grid):
    return pltpu.CompilerParams(dimension_semantics=("arbitrary",) * n_grid,
                                vmem_limit_bytes=V7X_VMEM_LIMIT_BYTES)


def _dot(a, b):
    return jnp.dot(a, b, preferred_element_type=F32)


def _dot_nt(a, b):
    return lax.dot_general(a, b, (((1,), (1,)), ((), ())), preferred_element_type=F32)


def _dot_tn(a, b):
    return lax.dot_general(a, b, (((0,), (0,)), ((), ())), preferred_element_type=F32)


def _rows_to_tiles(x):
    rows, d = x.shape
    nsub = d // LANES
    parts = [x[:, s * LANES:(s + 1) * LANES].reshape(rows // SUBLANES, SUBLANES, LANES) for s in range(nsub)]
    return jnp.swapaxes(jnp.stack(parts, axis=1), 1, 2).reshape(rows, nsub, LANES)


def _tiles_to_rows(x):
    rows, nsub, _ = x.shape
    y = jnp.swapaxes(x.reshape(rows // SUBLANES, SUBLANES, nsub, LANES), 1, 2)
    return jnp.concatenate([y[:, s].reshape(rows, LANES) for s in range(nsub)], axis=1)


def _ln(x, g, b):
    mu = jnp.mean(x, -1, keepdims=True)
    xc = x - mu
    var = jnp.mean(xc * xc, -1, keepdims=True)
    return xc * lax.rsqrt(var + LN_EPS) * g + b


def _silu(x):
    return x * jax.nn.sigmoid(x)


def _gelu_tanh(x):
    return 0.5 * x * (1.0 + jnp.tanh(math.sqrt(2.0 / math.pi) * (x + 0.044715 * (x * x * x))))


def _mod_kernel(c_ref, w_ref, b_ref, o_ref):
    s = _silu(c_ref[...])
    o_ref[0] = _dot(s.astype(BF16), w_ref[0].astype(BF16)) + b_ref[0]


def _modulation(cvec, w_mod, b_mod):
    depth, d, n = w_mod.shape
    tn = n // 4
    return pl.pallas_call(
        _mod_kernel,
        out_shape=jax.ShapeDtypeStruct((depth, 8, n), F32),
        grid=(depth, n // tn),
        in_specs=[pl.BlockSpec((8, d), lambda i, j: (0, 0)),
                  pl.BlockSpec((1, d, tn), lambda i, j: (i, 0, j)),
                  pl.BlockSpec((1, 1, tn), lambda i, j: (i, 0, j))],
        out_specs=pl.BlockSpec((1, 8, tn), lambda i, j: (i, 0, j)),
        compiler_params=_params(2), name="modulation",
    )(cvec, w_mod, b_mod.reshape(depth, 1, n))


def _pos_table_kernel(o_ref, *, q):
    rows, cols = o_ref.shape
    p = lax.broadcasted_iota(I32, (rows, cols), 0).astype(F32)
    lane = lax.broadcasted_iota(I32, (rows, cols), 1)
    j = jnp.where(lane >= q, lane - q, lane).astype(F32)
    omega = jnp.exp(j * (-math.log(10000.0) / q))
    ang = p * omega
    o_ref[...] = jnp.where(lane >= q, jnp.cos(ang), jnp.sin(ang))


def _pos_table(n, d):
    q = d // 4
    return pl.pallas_call(functools.partial(_pos_table_kernel, q=q),
                          out_shape=jax.ShapeDtypeStruct((n, 2 * q), F32), name="pos_table")()


def _add_pos_kernel(x_ref, er_ref, ec_ref, o_ref, *, half):
    tm = x_ref.shape[0]
    reps = tm // GRID_W
    er = er_ref[...]
    er_rows = jnp.broadcast_to(er[:, None, :], (reps, GRID_W, half)).reshape(tm, half)
    ec_rows = jnp.broadcast_to(ec_ref[...][None], (reps, GRID_W, half)).reshape(tm, half)
    o_ref[:, :half] = x_ref[:, :half] + er_rows
    o_ref[:, half:] = x_ref[:, half:] + ec_rows


def _add_pos(x2d, seq):
    n, d = x2d.shape
    half = d // 2
    rows = seq // GRID_W
    er = _pos_table(rows, d)
    ec = _pos_table(GRID_W, d)
    tm = ROW_TILE
    reps = tm // GRID_W
    tps = seq // tm
    return pl.pallas_call(
        functools.partial(_add_pos_kernel, half=half),
        out_shape=jax.ShapeDtypeStruct((n, d), F32),
        grid=(n // tm,),
        in_specs=[pl.BlockSpec((tm, d), lambda i: (i, 0)),
                  pl.BlockSpec((reps, half), lambda i: (i % tps, 0)),
                  pl.BlockSpec((GRID_W, half), lambda i: (0, 0))],
        out_specs=pl.BlockSpec((tm, d), lambda i: (i, 0)),
        compiler_params=_params(1), name="add_pos",
    )(x2d, er, ec)


def _gm_in_kernel(h_ref, sc_ref, sh_ref, w_ref, b_ref, g_ref, bb_ref, u_ref, v_ref):
    d = h_ref.shape[1]
    a = (h_ref[...] * (1.0 + sc_ref[0]) + sh_ref[0]).astype(BF16)
    u_ref[...] = _gelu_tanh(_dot(a, w_ref[:, :d]) + b_ref[:, :d])
    v = _gelu_tanh(_dot(a, w_ref[:, d:]) + b_ref[:, d:])
    v_ref[...] = _ln(v, g_ref[...], bb_ref[...]).astype(BF16)


def _gm_in(h, sc, sh, w, b, g, bb, seq):
    n, d = h.shape
    tm = min(ROW_TILE, seq)
    tps = seq // tm
    row = lambda i: (i, 0)
    per_b = lambda i: (i // tps, 0, 0)
    fixed = lambda i: (0, 0)
    return pl.pallas_call(
        _gm_in_kernel,
        out_shape=(jax.ShapeDtypeStruct((n, d), F32), jax.ShapeDtypeStruct((n, d), BF16)),
        grid=(n // tm,),
        in_specs=[pl.BlockSpec((tm, d), row), pl.BlockSpec((1, 1, d), per_b), pl.BlockSpec((1, 1, d), per_b),
                  pl.BlockSpec((d, 2 * d), fixed), pl.BlockSpec((1, 2 * d), fixed),
                  pl.BlockSpec((1, d), fixed), pl.BlockSpec((1, d), fixed)],
        out_specs=(pl.BlockSpec((tm, d), row), pl.BlockSpec((tm, d), row)),
        compiler_params=_params(1), name="gmlp_in",
    )(h, sc, sh, w, b, g, bb)


def _gm_prologue(u_ref, v_ref, ws_ref, bs_ref):
    tm, d = u_ref.shape
    dh = d // GM_HEADS
    for c in range(tm // GM_CHUNK):
        rows = slice(c * GM_CHUNK, (c + 1) * GM_CHUNK)
        parts = [_dot(ws_ref[g], v_ref[rows, g * dh:(g + 1) * dh]) for g in range(GM_HEADS)]
        vm = jnp.concatenate(parts, axis=1) + bs_ref[...]
        yield rows, (u_ref[rows, :] * vm).astype(BF16)


def _gla_prologue(of_ref, ob_ref, r_ref, ng_ref):
    tm, d = of_ref.shape
    dh = d // GLA_HEADS
    o = of_ref[...] + ob_ref[...]
    parts = []
    for hd in range(GLA_HEADS):
        oh = o[:, hd * dh:(hd + 1) * dh]
        mu = jnp.mean(oh, -1, keepdims=True)
        oc = oh - mu
        var = jnp.mean(oc * oc, -1, keepdims=True)
        parts.append(oc * lax.rsqrt(var + LN_EPS) * ng_ref[:, hd * dh:(hd + 1) * dh])
    y = jnp.concatenate(parts, axis=1) * _silu(r_ref[...])
    yield slice(0, tm), y.astype(BF16)


def _hy_prologue(y_ref):
    yield slice(0, y_ref.shape[0]), y_ref[...].astype(BF16)


_PROLOGUES = {"gmlp": (_gm_prologue, 4), "gla": (_gla_prologue, 4), "hyena": (_hy_prologue, 1)}


def _route_t(scores, biased):
    n_experts, tm = scores.shape
    gsz = n_experts // N_GROUPS
    neg = jnp.float32(-jnp.inf)
    v3 = biased.reshape(N_GROUPS, gsz, tm)
    sub = lax.broadcasted_iota(I32, v3.shape, 1).astype(F32)
    m1 = jnp.max(v3, axis=1, keepdims=True)
    i1 = jnp.min(jnp.where(v3 == m1, sub, float(gsz)), axis=1, keepdims=True)
    v3b = jnp.where(sub == i1, neg, v3)
    m2 = jnp.max(v3b, axis=1, keepdims=True)
    i2 = jnp.min(jnp.where(v3b == m2, sub, float(gsz)), axis=1, keepdims=True)
    gscore = (m1 + m2).reshape(N_GROUPS, tm)
    i1 = i1.reshape(N_GROUPS, tm)
    i2 = i2.reshape(N_GROUPS, tm)
    best, e0, e1 = gscore[0:1], i1[0:1], i2[0:1]
    for g in range(1, N_GROUPS):
        better = gscore[g:g + 1] > best
        best = jnp.where(better, gscore[g:g + 1], best)
        e0 = jnp.where(better, i1[g:g + 1] + float(g * gsz), e0)
        e1 = jnp.where(better, i2[g:g + 1] + float(g * gsz), e1)
    row = lax.broadcasted_iota(I32, scores.shape, 0).astype(F32)
    oh0 = (row == e0).astype(F32)
    oh1 = (row == e1).astype(F32)
    w0 = jnp.sum(oh0 * scores, axis=0, keepdims=True)
    w1 = jnp.sum(oh1 * scores, axis=0, keepdims=True)
    den = w0 + w1
    return e0, e1, oh0, oh1, w0 / den, w1 / den


def _post_kernel(*refs, kind, alpha, cap, aliased, finalize):
    prologue, n_pro = _PROLOGUES[kind]
    pro = refs[:n_pro]
    rest = refs[n_pro:]
    (h_ref, w_ref, b_ref, g1_ref, lg_ref, lb_ref, sc_ref, sh_ref, rhl_ref, rb_ref, cin_ref) = rest[:11]
    rest = rest[11 + int(aliased):]
    (h1_ref, meta_ref, cout_ref, xs_ref, cnt_scr, tok_scr, dvm_scr, idx_scr, zero_scr, cvm_scr, csm_scr,
     isem, rsem, zsem) = rest
    i = pl.program_id(0)
    nstep = pl.num_programs(0)
    tm = h_ref.shape[0]
    n_experts = rb_ref.shape[0]
    slot = i % POST_SLOTS

    def idx_copy(s, k):
        return pltpu.make_async_copy(dvm_scr.at[s, k], idx_scr.at[s, k], isem.at[s])

    def row_copy(s, r, k):
        return pltpu.make_async_copy(tok_scr.at[s, r], xs_ref.at[idx_scr[s, k, r]], rsem.at[s])

    def start_rows(s):
        for k in range(TOP_K):
            idx_copy(s, k).wait()

        def start_row(r, u):
            for k in range(TOP_K):
                row_copy(s, r, k).start(priority=(u + k) % 2)
        _row_loop(tm, start_row)

    def wait_rows(s):
        def wait_row(r, u):
            for k in range(TOP_K):
                row_copy(s, r, k).wait()
        _row_loop(tm, wait_row)

    @pl.when(i == 0)
    def _():
        cnt_scr[...] = cin_ref[...]

    @pl.when(i >= POST_SLOTS)
    def _():
        wait_rows(slot)

    for rows, y in prologue(*pro):
        out = _dot(y, w_ref[...]) + b_ref[...]
        h1_ref[rows, :] = _ln(alpha * h_ref[rows, :] + g1_ref[0] * out, lg_ref[...], lb_ref[...])
    tok = h1_ref[...] * (1.0 + sc_ref[0]) + sh_ref[0]
    tok_scr[slot] = _rows_to_tiles(tok)
    t_hi = tok.astype(BF16)
    t_lo = (tok - t_hi.astype(F32)).astype(BF16)
    p_hi = _dot_nt(rhl_ref[...], t_hi)
    logits = p_hi[:n_experts] + p_hi[n_experts:] + _dot_nt(rhl_ref[0:n_experts, :], t_lo)
    scores = jax.nn.sigmoid(logits)
    e0, e1, oh0, oh1, w0, w1 = _route_t(scores, scores + rb_ref[...])
    r_i = lax.broadcasted_iota(I32, (tm, tm), 0)
    c_i = lax.broadcasted_iota(I32, (tm, tm), 1)
    upper = (r_i < c_i).astype(BF16)
    oh = oh0 + oh1
    carry = cnt_scr[...]
    before = _dot(oh.astype(BF16), upper) + jnp.concatenate([carry] * (tm // LANES), axis=1)
    rank0 = jnp.sum(oh0 * before, axis=0, keepdims=True)
    rank1 = jnp.sum(oh1 * before, axis=0, keepdims=True)
    cnt_scr[...] = carry + jnp.sum(oh, axis=1, keepdims=True)
    zero = jnp.zeros_like(w0)
    meta_ref[0] = jnp.concatenate([e0, e1, rank0, rank1, w0, w1, zero, zero], axis=0)
    cout_ref[...] = cnt_scr[...]

    dvm_scr[slot] = jnp.concatenate([e0 * float(cap) + rank0, e1 * float(cap) + rank1] + [zero] * (SUBLANES - TOP_K),
                                    axis=0).astype(I32)
    for k in range(TOP_K):
        idx_copy(slot, k).start()

    @pl.when(i >= 1)
    def _():
        start_rows((i + POST_SLOTS - 1) % POST_SLOTS)

    @pl.when(i == nstep - 1)
    def _():
        start_rows(slot)
        for back in range(POST_SLOTS - 1, -1, -1):
            @pl.when(i >= back)
            def _():
                wait_rows((i + POST_SLOTS - back) % POST_SLOTS)

        if finalize:
            zero_scr[...] = jnp.zeros_like(zero_scr)
            cvm_scr[...] = cnt_scr[...].astype(I32)
            pltpu.make_async_copy(cvm_scr, csm_scr, zsem).start()
            pltpu.make_async_copy(cvm_scr, csm_scr, zsem).wait()

            def fill_copy(e):
                return pltpu.make_async_copy(zero_scr, xs_ref.at[pl.ds(e * cap + csm_scr[e, 0], MOE_ROWS)], zsem)

            def fill_start(e, c):
                fill_copy(e).start()
                return c

            def fill_wait(e, c):
                fill_copy(e).wait()
                return c

            lax.fori_loop(0, n_experts, fill_start, 0)
            lax.fori_loop(0, n_experts, fill_wait, 0)


def _post(kind, pro_args, pro_specs, h, w, b, g1, lg, lb, sc, sh, rhl, rb, cnt_in, xs, cap, finalize, seq, alpha):
    n, d = h.shape
    ts = d // LANES
    tm = min(ROW_TILE, seq)
    tps = seq // tm
    n_experts = rb.shape[0]
    aliased = xs is not None
    row = lambda i: (i, 0)
    per_b = lambda i: (i // tps, 0, 0)
    fixed = lambda i: (0, 0)
    any_spec = pl.BlockSpec(memory_space=pl.ANY)
    pro_in = list(pro_specs(tm, d))
    in_specs = pro_in + [
        pl.BlockSpec((tm, d), row), pl.BlockSpec((d, d), fixed), pl.BlockSpec((1, d), fixed),
        pl.BlockSpec((1, 1, d), per_b), pl.BlockSpec((1, d), fixed), pl.BlockSpec((1, d), fixed),
        pl.BlockSpec((1, 1, d), per_b), pl.BlockSpec((1, 1, d), per_b),
        pl.BlockSpec((2 * n_experts, d), fixed), pl.BlockSpec((n_experts, tm), fixed),
        pl.BlockSpec((n_experts, LANES), fixed)] + ([any_spec] if aliased else [])
    args = list(pro_args) + [h, w, b, g1, lg, lb, sc, sh, rhl, rb[:, :tm], cnt_in] + ([xs] if aliased else [])
    return pl.pallas_call(
        functools.partial(_post_kernel, kind=kind, alpha=alpha, cap=cap, aliased=aliased, finalize=finalize),
        out_shape=(jax.ShapeDtypeStruct((n, d), F32), jax.ShapeDtypeStruct((n // tm, SUBLANES, tm), F32),
                   jax.ShapeDtypeStruct((n_experts, LANES), F32),
                   jax.ShapeDtypeStruct((n_experts * cap, ts, LANES), F32)),
        grid=(n // tm,),
        in_specs=in_specs,
        out_specs=(pl.BlockSpec((tm, d), row), pl.BlockSpec((1, SUBLANES, tm), lambda i: (i, 0, 0)),
                   pl.BlockSpec((n_experts, LANES), fixed), any_spec),
        scratch_shapes=[pltpu.VMEM((n_experts, LANES), F32), pltpu.VMEM((POST_SLOTS, tm, ts, LANES), F32),
                        pltpu.VMEM((POST_SLOTS, SUBLANES, tm), I32), pltpu.SMEM((POST_SLOTS, TOP_K, tm), I32),
                        pltpu.VMEM((MOE_ROWS, ts, LANES), F32), pltpu.VMEM((n_experts, LANES), I32),
                        pltpu.SMEM((n_experts, LANES), I32), pltpu.SemaphoreType.DMA((POST_SLOTS,)),
                        pltpu.SemaphoreType.DMA((POST_SLOTS,)), pltpu.SemaphoreType.DMA],
        input_output_aliases={len(args) - 1: 3} if aliased else {},
        compiler_params=_params(1), name="post_" + kind,
    )(*args)


def _dest_kernel(meta_ref, pstart_ref, o_ref):
    group, _, tmeta = meta_ref.shape
    n_experts = pstart_ref.shape[0]
    tm = o_ref.shape[2]
    per = tmeta // tm
    row = lax.broadcasted_iota(I32, (n_experts, tmeta), 0).astype(F32)
    for g in range(group):
        meta = meta_ref[g]
        rows = []
        for k in range(TOP_K):
            off = jnp.sum(jnp.where(row == meta[k:k + 1], pstart_ref[...], 0.0), axis=0, keepdims=True)
            rows.append(off + meta[TOP_K + k:TOP_K + k + 1])
        rows += [jnp.zeros_like(rows[0])] * (SUBLANES - TOP_K)
        dest = jnp.concatenate(rows, axis=0).astype(I32)
        for j in range(per):
            o_ref[g * per + j] = dest[:, j * tm:(j + 1) * tm]


def _dest(meta, pstart_b):
    ntile, _, tmeta = meta.shape
    tm = DMA_TILE
    per = tmeta // tm
    n_experts = pstart_b.shape[0]
    group = math.gcd(ntile, 8)
    return pl.pallas_call(
        _dest_kernel,
        out_shape=jax.ShapeDtypeStruct((ntile * per, SUBLANES, tm), I32),
        grid=(ntile // group,),
        in_specs=[pl.BlockSpec((group, SUBLANES, tmeta), lambda i: (i, 0, 0)),
                  pl.BlockSpec((n_experts, tmeta), lambda i: (0, 0))],
        out_specs=pl.BlockSpec((group * per, SUBLANES, tm), lambda i: (i, 0, 0)),
        compiler_params=_params(1), name="moe_dest",
    )(meta, pstart_b[:, :tmeta])


def _row_loop(tm, body):
    def step(it, c):
        for u in range(DMA_UNROLL):
            body(it * DMA_UNROLL + u, u)
        return c
    lax.fori_loop(0, tm // DMA_UNROLL, step, 0)


def _dispatch_kernel(lb_ref, dest_ref, tok_ref, *rest, fill):
    if fill:
        xs_ref, idx_scr, tok_scr, zero_scr, isem, lsem, rsem, zsem = rest
    else:
        _, xs_ref, idx_scr, tok_scr, zero_scr, isem, lsem, rsem, zsem = rest
    i = pl.program_id(0)
    nstep = pl.num_programs(0)
    tm = idx_scr.shape[2]
    slot = i % 2

    def load(step, to):
        return pltpu.make_async_copy(tok_ref.at[pl.ds(step * tm, tm)], tok_scr.at[to], lsem.at[to])

    @pl.when(i == 0)
    def _():
        load(0, 0).start()

    if fill:
        @pl.when(i == 0)
        def _():
            zero_scr[...] = jnp.zeros_like(zero_scr)

            def fill_copy(e):
                start_row = pl.multiple_of(jnp.maximum(lb_ref[e], 0), MOE_ROWS)
                return pltpu.make_async_copy(zero_scr, xs_ref.at[pl.ds(start_row, MOE_ROWS)], zsem)

            def start(e, c):
                @pl.when(lb_ref[e] >= 0)
                def _():
                    fill_copy(e).start()
                return c

            def wait(e, c):
                @pl.when(lb_ref[e] >= 0)
                def _():
                    fill_copy(e).wait()
                return c

            lax.fori_loop(0, lb_ref.shape[0], start, 0)
            lax.fori_loop(0, lb_ref.shape[0], wait, 0)

    for k in range(TOP_K):
        pltpu.make_async_copy(dest_ref.at[i, k], idx_scr.at[slot, k], isem).start()
    for k in range(TOP_K):
        pltpu.make_async_copy(dest_ref.at[i, k], idx_scr.at[slot, k], isem).wait()

    def row_copy(s, r, k):
        return pltpu.make_async_copy(tok_scr.at[s, r], xs_ref.at[idx_scr[s, k, r]], rsem.at[s])

    def start_row(r, u):
        for k in range(TOP_K):
            row_copy(slot, r, k).start(priority=(u + k) % 2)

    def wait_rows(s):
        def wait_row(r, u):
            for k in range(TOP_K):
                row_copy(s, r, k).wait()
        _row_loop(tm, wait_row)

    load(i, slot).wait()
    _row_loop(tm, start_row)

    @pl.when(i > 0)
    def _():
        wait_rows(1 - slot)

    @pl.when(i + 1 < nstep)
    def _():
        load(i + 1, 1 - slot).start()

    @pl.when(i == nstep - 1)
    def _():
        wait_rows(slot)


def _dispatch(last_blk, dest, tok, xs, p):
    n, ts, _ = tok.shape
    tm = DMA_TILE
    fill = xs is None
    any_spec = pl.BlockSpec(memory_space=pl.ANY)
    args = [last_blk, dest, tok] + ([] if fill else [xs])
    return pl.pallas_call(
        functools.partial(_dispatch_kernel, fill=fill),
        out_shape=jax.ShapeDtypeStruct((p, ts, LANES), F32),
        grid_spec=pltpu.PrefetchScalarGridSpec(
            num_scalar_prefetch=1, grid=(n // tm,), in_specs=[any_spec] * (len(args) - 1), out_specs=any_spec,
            scratch_shapes=[pltpu.SMEM((2, TOP_K, tm), I32), pltpu.VMEM((2, tm, ts, LANES), F32),
                            pltpu.VMEM((MOE_ROWS, ts, LANES), F32), pltpu.SemaphoreType.DMA,
                            pltpu.SemaphoreType.DMA((2,)), pltpu.SemaphoreType.DMA((2,)), pltpu.SemaphoreType.DMA]),
        input_output_aliases={} if fill else {3: 0},
        compiler_params=_params(1), name="moe_dispatch",
    )(*args)


def _expert_kernel(be_ref, nu_ref, br_ref, x_ref, w1_ref, w3_ref, w2_ref, y_ref, w1_scr, w3_scr, w2_scr):
    del br_ref
    j = pl.program_id(0)
    e = be_ref[j]
    e_prev = be_ref[jnp.maximum(j - 1, 0)]

    @pl.when(j < nu_ref[0])
    def _():
        @pl.when((j == 0) | (e != e_prev))
        def _():
            w1_scr[...] = w1_ref[0, 0].astype(BF16)
            w3_scr[...] = w3_ref[0, 0].astype(BF16)
            w2_scr[...] = w2_ref[0, 0].astype(BF16)

        x = _tiles_to_rows(x_ref[...]).astype(BF16)
        hid = _silu(_dot(x, w1_scr[...])) * _dot(x, w3_scr[...])
        y_ref[...] = _rows_to_tiles(_dot(hid.astype(BF16), w2_scr[...]))

    @pl.when(j >= nu_ref[0])
    def _():
        y_ref[...] = jnp.zeros_like(y_ref)


def _experts(blk_e, n_used, blk_row, xs, nblk, w1, w3, w2, layer):
    _, ts, _ = xs.shape
    d = ts * LANES
    de = w1.shape[-1]
    wmap = lambda j, be, nu, br: (layer, be[j], 0, 0)
    xmap = lambda j, be, nu, br: (br[jnp.minimum(j, nu[0] - 1)], 0, 0)
    return pl.pallas_call(
        _expert_kernel,
        out_shape=jax.ShapeDtypeStruct((nblk * MOE_ROWS, ts, LANES), F32),
        grid_spec=pltpu.PrefetchScalarGridSpec(
            num_scalar_prefetch=3, grid=(nblk,),
            in_specs=[pl.BlockSpec((MOE_ROWS, ts, LANES), xmap),
                      pl.BlockSpec((1, 1, d, de), wmap), pl.BlockSpec((1, 1, d, de), wmap),
                      pl.BlockSpec((1, 1, de, d), wmap)],
            out_specs=pl.BlockSpec((MOE_ROWS, ts, LANES), lambda j, be, nu, br: (j, 0, 0)),
            scratch_shapes=[pltpu.VMEM((d, de), BF16), pltpu.VMEM((d, de), BF16), pltpu.VMEM((de, d), BF16)]),
        compiler_params=_params(1), name="moe_experts",
    )(blk_e, n_used, blk_row, xs, w1, w3, w2)


def _combine_kernel(dest_ref, ys_ref, meta_ref, h_ref, g2_ref, lg_ref, lb_ref, o_ref,
                    idx_scr, buf_scr, isem, rsem, *, alpha):
    i = pl.program_id(0)
    nstep = pl.num_programs(0)
    tm = h_ref.shape[0]

    def gather(step, slot):
        for k in range(TOP_K):
            pltpu.make_async_copy(dest_ref.at[step, k], idx_scr.at[slot, k], isem).start()
        for k in range(TOP_K):
            pltpu.make_async_copy(dest_ref.at[step, k], idx_scr.at[slot, k], isem).wait()

        def start_row(r, u):
            for k in range(TOP_K):
                pltpu.make_async_copy(ys_ref.at[idx_scr[slot, k, r]], buf_scr.at[slot, k, r],
                                      rsem.at[slot]).start(priority=(u + k) % 2)

        _row_loop(tm, start_row)

    depth = COMBINE_SLOTS - 1
    slot = i % COMBINE_SLOTS

    @pl.when(i == 0)
    def _():
        for a in range(depth):
            @pl.when(a < nstep)
            def _():
                gather(a, a)

    @pl.when(i + depth < nstep)
    def _():
        gather(i + depth, (i + depth) % COMBINE_SLOTS)

    def wait_row(r, u):
        for k in range(TOP_K):
            pltpu.make_async_copy(ys_ref.at[idx_scr[slot, k, r]], buf_scr.at[slot, k, r], rsem.at[slot]).wait()

    _row_loop(tm, wait_row)
    meta = meta_ref[0]
    wt = jnp.concatenate([meta, jnp.zeros((LANES - SUBLANES, tm), F32)], axis=0).T
    f = wt[:, 2 * TOP_K:2 * TOP_K + 1] * _tiles_to_rows(buf_scr[slot, 0])
    for k in range(1, TOP_K):
        f = f + wt[:, 2 * TOP_K + k:2 * TOP_K + k + 1] * _tiles_to_rows(buf_scr[slot, k])
    o_ref[...] = _ln(alpha * h_ref[...] + g2_ref[0] * f, lg_ref[...], lb_ref[...])


def _combine(dest, ys, meta, h, g2, lg, lb, seq, alpha):
    n, d = h.shape
    tm = DMA_TILE
    tps = seq // tm
    per = meta.shape[2] // tm
    row = lambda i: (i, 0)
    fixed = lambda i: (0, 0)
    return pl.pallas_call(
        functools.partial(_combine_kernel, alpha=alpha),
        out_shape=jax.ShapeDtypeStruct((n, d), F32),
        grid=(n // tm,),
        in_specs=[pl.BlockSpec(memory_space=pl.ANY), pl.BlockSpec(memory_space=pl.ANY),
                  pl.BlockSpec((1, SUBLANES, tm), lambda i: (i // per, 0, i % per)), pl.BlockSpec((tm, d), row),
                  pl.BlockSpec((1, 1, d), lambda i: (i // tps, 0, 0)),
                  pl.BlockSpec((1, d), fixed), pl.BlockSpec((1, d), fixed)],
        out_specs=pl.BlockSpec((tm, d), row),
        scratch_shapes=[pltpu.SMEM((COMBINE_SLOTS, TOP_K, tm), I32),
                        pltpu.VMEM((COMBINE_SLOTS, TOP_K, tm, d // LANES, LANES), F32),
                        pltpu.SemaphoreType.DMA, pltpu.SemaphoreType.DMA((COMBINE_SLOTS,))],
        compiler_params=_params(1), name="moe_combine",
    )(dest, ys, meta, h, g2, lg, lb)


def _moe(streams, counts, xs, cap, w1, w3, w2, layer, alpha):
    n_experts = w1.shape[1]
    n_assign = TOP_K * sum(s["h1"].shape[0] for s in streams)
    nblk = n_assign // MOE_ROWS + n_experts
    cnt = counts[:, 0].astype(I32)
    padded = (cnt + MOE_ROWS - 1) // MOE_ROWS * MOE_ROWS
    pend = jnp.cumsum(padded)
    pstart = pend - padded
    n_used = (pend[-1] // MOE_ROWS).astype(I32).reshape(1)
    blk = jnp.arange(nblk, dtype=I32) * MOE_ROWS
    blk_e = jnp.sum((pend[None, :] <= blk[:, None]).astype(I32), axis=1)
    last_e = jnp.sum((pend <= jnp.maximum(pend[-1] - 1, 0)).astype(I32))
    blk_e = jnp.minimum(blk_e, last_e).astype(I32)
    blk_row = ((blk_e * cap + blk - pstart[blk_e]) // MOE_ROWS).astype(I32)
    pstart_b = jnp.broadcast_to(pstart.astype(F32)[:, None], (n_experts, ROW_TILE))
    dests = [_dest(s["meta"], pstart_b) for s in streams]
    ys = _experts(blk_e, n_used, blk_row, xs, nblk, w1, w3, w2, layer)
    return [_combine(dst, ys, s["meta"], s["h1"], s["g2"], s["lg"], s["lb"], s["seq"], alpha)
            for s, dst in zip(streams, dests)]


def _log_sigmoid(x):
    return jnp.minimum(x, 0.0) - jnp.log(1.0 + jnp.exp(-jnp.abs(x)))


def _gla_in_kernel(h_ref, sc_ref, sh_ref, w_ref, wl_ref, wg_ref, bg_ref, qk_ref, v_ref, r_ref, g_ref, *, qscale):
    tm, d = h_ref.shape
    a = (h_ref[...] * (1.0 + sc_ref[0]) + sh_ref[0]).astype(BF16)
    qk = _dot(a, w_ref[:, :d])
    lane = lax.broadcasted_iota(I32, (tm, d), 1)
    qk_ref[...] = jnp.where(lane < d // 2, qk * qscale, qk)
    v_ref[...] = _dot(a, w_ref[:, d:2 * d]).astype(BF16)
    r_ref[...] = _dot(a, w_ref[:, 2 * d:])
    lr = _dot(a, wl_ref[...])
    gpre = _dot(lr.astype(BF16), wg_ref[...]) + bg_ref[...]
    g_ref[...] = _log_sigmoid(gpre) * (1.0 / GLA_TAU)


def _gla_weights(w_in, w_gate, b_gate, d):
    dk = d // 2
    n_lr = HY_DIRS * GLA_RANK
    w_main = w_in[:, :2 * dk + 2 * d].astype(BF16)
    w_lr = jnp.zeros((d, LANES), F32).at[:, :n_lr].set(w_in[:, 2 * dk + 2 * d:]).astype(BF16)
    w_g = jnp.zeros((LANES, HY_DIRS * dk), F32)
    for dr in range(HY_DIRS):
        w_g = w_g.at[dr * GLA_RANK:(dr + 1) * GLA_RANK, dr * dk:(dr + 1) * dk].set(w_gate[dr])
    return w_main, w_lr, w_g.astype(BF16), b_gate.reshape(1, HY_DIRS * dk)


def _gla_in(h, sc, sh, w_main, w_lr, w_g, b_g, seq):
    n, d = h.shape
    tm = min(ROW_TILE, seq)
    tps = seq // tm
    row = lambda i: (i, 0)
    per_b = lambda i: (i // tps, 0, 0)
    fixed = lambda i: (0, 0)
    qscale = float(((d // 2) // GLA_HEADS) ** -0.5)
    return pl.pallas_call(
        functools.partial(_gla_in_kernel, qscale=qscale),
        out_shape=(jax.ShapeDtypeStruct((n, d), F32), jax.ShapeDtypeStruct((n, d), BF16),
                   jax.ShapeDtypeStruct((n, d), F32), jax.ShapeDtypeStruct((n, d), F32)),
        grid=(n // tm,),
        in_specs=[pl.BlockSpec((tm, d), row), pl.BlockSpec((1, 1, d), per_b), pl.BlockSpec((1, 1, d), per_b),
                  pl.BlockSpec((d, 3 * d), fixed), pl.BlockSpec((d, LANES), fixed),
                  pl.BlockSpec((LANES, d), fixed), pl.BlockSpec((1, d), fixed)],
        out_specs=tuple(pl.BlockSpec((tm, d), row) for _ in range(4)),
        compiler_params=_params(1), name="gla_in",
    )(h, sc, sh, w_main, w_lr, w_g, b_g)


def _gla_direction(q_ref, k_ref, v_ref, g_ref, st_scr, o_ref, reverse):
    rt, dk = q_ref.shape
    cs = GLA_CHUNK
    nch = rt // cs
    g = g_ref[...]
    pos = lax.broadcasted_iota(I32, (rt, dk), 0) % cs
    b = g
    sh = 1
    while sh < cs:
        if reverse:
            b = b + jnp.where(pos < cs - sh, pltpu.roll(b, rt - sh, 0), 0.0)
        else:
            b = b + jnp.where(pos >= sh, pltpu.roll(b, sh, 0), 0.0)
        sh *= 2
    edge = 0 if reverse else cs - 1
    b3 = b.reshape(nch, cs, dk)
    b_edge = b3[:, edge:edge + 1, :]
    q = q_ref[...]
    k = k_ref[...]
    qe = (q * jnp.exp(b)).astype(BF16)
    ke = (k * jnp.exp(-b)).astype(BF16)
    kd = (k.reshape(nch, cs, dk) * jnp.exp(b_edge - b3)).astype(BF16)
    decay = jnp.exp(b_edge)
    r_i = lax.broadcasted_iota(I32, (rt, rt), 0)
    c_i = lax.broadcasted_iota(I32, (rt, rt), 1)
    same = (r_i // cs) == (c_i // cs)
    tri = (c_i >= r_i) if reverse else (c_i <= r_i)
    att = jnp.where(same & tri, _dot_nt(qe, ke), 0.0).astype(BF16)
    v = v_ref[...]
    intra = _dot(att, v)
    st = st_scr[...]
    order = range(nch - 1, -1, -1) if reverse else range(nch)
    for j in order:
        rows = slice(j * cs, (j + 1) * cs)
        o_ref[rows, :] = intra[rows, :] + _dot_nt(qe[rows, :], st.astype(BF16))
        st = st * decay[j] + _dot_tn(v[rows, :], kd[j])
    st_scr[...] = st


def _gla_scan_kernel(qkf, vf, gf, qkb, vb, gb, s0f, s0b, of, ob, sfo, sbo, sf_scr, sb_scr):
    c = pl.program_id(1)
    nh = GLA_HEADS
    d = qkf.shape[1]
    dkh, dvh = (d // 2) // nh, d // nh

    @pl.when(c == 0)
    def _():
        sf_scr[...] = s0f[0]
        sb_scr[...] = s0b[0]

    for hd in range(nh):
        qs, ks, vs = pl.ds(hd * dkh, dkh), pl.ds(d // 2 + hd * dkh, dkh), pl.ds(hd * dvh, dvh)
        _gla_direction(qkf.at[:, qs], qkf.at[:, ks], vf.at[:, vs], gf.at[:, qs], sf_scr.at[hd], of.at[:, vs], False)
        _gla_direction(qkb.at[:, qs], qkb.at[:, ks], vb.at[:, vs], gb.at[:, ks], sb_scr.at[hd], ob.at[:, vs], True)
    sfo[0] = sf_scr[...]
    sbo[0] = sb_scr[...]


def _gla_scan(qk, v, r, g, s0f, s0b, nb, seq):
    del r
    n, d = qk.shape
    nh = GLA_HEADS
    dkh, dvh = (d // 2) // nh, d // nh
    rt = min(ROW_TILE, seq)
    npb = seq // rt
    fw = pl.BlockSpec((rt, d), lambda b, c: (b * npb + c, 0))
    bw = pl.BlockSpec((rt, d), lambda b, c: (b * npb + npb - 1 - c, 0))
    st = pl.BlockSpec((1, nh, dvh, dkh), lambda b, c: (b, 0, 0, 0))
    state = jax.ShapeDtypeStruct((nb, nh, dvh, dkh), F32)
    return pl.pallas_call(
        _gla_scan_kernel,
        out_shape=(jax.ShapeDtypeStruct((n, d), F32), jax.ShapeDtypeStruct((n, d), F32), state, state),
        grid=(nb, npb),
        in_specs=[fw, fw, fw, bw, bw, bw, st, st],
        out_specs=(fw, bw, st, st),
        scratch_shapes=[pltpu.VMEM((nh, dvh, dkh), F32), pltpu.VMEM((nh, dvh, dkh), F32)],
        compiler_params=_params(2), name="gla_scan",
    )(qk, v, g, qk, v, g, s0f, s0b)


def _hy_in_kernel(h_ref, hp_ref, hn_ref, sc_ref, sh_ref, w_ref, b_ref, cw_ref, cb_ref, o_ref, a_scr, z_scr,
                  *, tps, halo):
    i = pl.program_id(0)
    tm, d = h_ref.shape
    sc = 1.0 + sc_ref[0]
    sh = sh_ref[0]
    a_scr[0:halo, :] = (hp_ref[...] * sc + sh).astype(BF16)
    a_scr[halo:halo + tm, :] = (h_ref[...] * sc + sh).astype(BF16)
    a_scr[halo + tm:, :] = (hn_ref[...] * sc + sh).astype(BF16)
    row = lax.broadcasted_iota(I32, (tm, 1), 0)
    first = jnp.where(i % tps == 0, 0, -1)
    last = jnp.where(i % tps == tps - 1, tm - 1, -1)
    for c in range(o_ref.shape[0]):
        cols = slice(c * d, (c + 1) * d)
        z_scr[...] = _dot(a_scr[...], w_ref[:, cols]) + b_ref[:, cols]
        zp = jnp.where(row == first, 0.0, z_scr[pl.ds(halo - 1, tm), :])
        zc = z_scr[pl.ds(halo, tm), :]
        zn = jnp.where(row == last, 0.0, z_scr[pl.ds(halo + 1, tm), :])
        o_ref[c] = cw_ref[0:1, cols] * zp + cw_ref[1:2, cols] * zc + cw_ref[2:3, cols] * zn + cb_ref[:, cols]


def _hy_in(h, sc, sh, w, b, cw, cb, seq):
    n, d = h.shape
    nsplit = w.shape[1] // d
    halo = 16
    tm = min(ROW_TILE, seq)
    tps = seq // tm
    hb = tm // halo
    row = lambda i: (i, 0)
    per_b = lambda i: (i // tps, 0, 0)
    fixed = lambda i: (0, 0)
    return pl.pallas_call(
        functools.partial(_hy_in_kernel, tps=tps, halo=halo),
        out_shape=jax.ShapeDtypeStruct((nsplit, n, d), F32),
        grid=(n // tm,),
        in_specs=[pl.BlockSpec((tm, d), row),
                  pl.BlockSpec((halo, d), lambda i: (jnp.maximum(i * hb - 1, 0), 0)),
                  pl.BlockSpec((halo, d), lambda i: (jnp.minimum((i + 1) * hb, n // halo - 1), 0)),
                  pl.BlockSpec((1, 1, d), per_b), pl.BlockSpec((1, 1, d), per_b),
                  pl.BlockSpec((d, nsplit * d), fixed), pl.BlockSpec((1, nsplit * d), fixed),
                  pl.BlockSpec((cw.shape[0], nsplit * d), fixed), pl.BlockSpec((1, nsplit * d), fixed)],
        out_specs=pl.BlockSpec((nsplit, tm, d), lambda i: (0, i, 0)),
        scratch_shapes=[pltpu.VMEM((tm + 2 * halo, d), BF16), pltpu.VMEM((tm + 2 * halo, d), F32)],
        compiler_params=_params(1), name="hyena_in",
    )(h, h, h, sc, sh, w, b, cw, cb)


def _hy_filter_rows(t, seq, w1_ref, b1_ref, w2_ref, b2_ref, w3_ref, b3_ref, n_bands):
    rows = t.shape[0]
    tf = t.astype(F32)
    t_lin = tf * (1.0 / (seq - 1))
    wpos = tf * (2.0 * math.pi / seq)
    lane = lax.broadcasted_iota(I32, (rows, LANES), 1)
    jb = jnp.where(lane > n_bands, lane - n_bands - 1, lane - 1).astype(F32)
    band = 1e-4 + jb * ((n_bands - 1 - 1e-4) / (n_bands - 1))
    ang = band * wpos
    z = jnp.where(lane == 0, t_lin,
                  jnp.where(lane <= n_bands, jnp.cos(ang), jnp.where(lane <= 2 * n_bands, -jnp.sin(ang), 0.0)))
    hf = jnp.sin(_dot(z.astype(BF16), w1_ref[...]) + b1_ref[...])
    hf = jnp.sin(_dot(hf.astype(BF16), w2_ref[...]) + b2_ref[...])
    hf = jnp.sin(_dot(hf.astype(BF16), w3_ref[...]) + b3_ref[...])
    return hf.astype(BF16), t_lin


def _hy_taps_kernel(w1_ref, b1_ref, w2_ref, b2_ref, w3_ref, b3_ref, w4_ref, o_ref, *, seq, n_bands):
    i = pl.program_id(0)
    n_ord, tr, d = o_ref.shape
    mlp = (w1_ref, b1_ref, w2_ref, b2_ref, w3_ref, b3_ref)
    r0 = i * tr
    second = (r0 >= seq).astype(I32)
    n_idx = r0 + lax.broadcasted_iota(I32, (tr, 1), 0)
    t = jnp.where(second == 1, 2 * seq - n_idx, n_idx)
    hf, t_lin = _hy_filter_rows(t, seq, *mlp, n_bands)
    c_idx = lax.broadcasted_iota(I32, (1, d), 1).astype(F32)
    min_decay = math.log(HY_DECAY_TARGET) / HY_DECAY_LONG_PCT
    max_decay = math.log(HY_DECAY_TARGET) / HY_DECAY_SHORT_PCT
    delta = jnp.abs(min_decay + c_idx * ((max_decay - min_decay) / (d - 1)))
    window = jnp.where(n_idx == seq, 0.0, jnp.exp(-t_lin * delta))
    for o in range(n_ord):
        o_ref[o] = _dot(hf, w4_ref[second, o]) * window

    @pl.when(i == 0)
    def _():
        hf0, _ = _hy_filter_rows(jnp.zeros((8, 1), I32), seq, *mlp, n_bands)
        first = lax.broadcasted_iota(I32, (8, 1), 0) == 0
        for o in range(n_ord):
            o_ref[o, 0:8, :] = o_ref[o, 0:8, :] + jnp.where(first, _dot(hf0, w4_ref[1, o]), 0.0)


def _hy_taps(seq, d, w1, b1, w2, b2, w3, b3, w4):
    emb, ff = w1.shape
    n_bands = (emb - 1) // 2
    w1p = jnp.zeros((LANES, ff), F32).at[:emb].set(w1).astype(BF16)
    w4r = w4.reshape(ff, HY_ORDER, HY_DIRS, d).transpose(2, 1, 0, 3).astype(BF16)
    tr = min(ROW_TILE, seq)
    fixed = lambda i: (0, 0)
    return pl.pallas_call(
        functools.partial(_hy_taps_kernel, seq=seq, n_bands=n_bands),
        out_shape=jax.ShapeDtypeStruct((HY_ORDER, 2 * seq, d), F32),
        grid=(2 * seq // tr,),
        in_specs=[pl.BlockSpec((LANES, ff), fixed), pl.BlockSpec((1, ff), fixed),
                  pl.BlockSpec((ff, ff), fixed), pl.BlockSpec((1, ff), fixed),
                  pl.BlockSpec((ff, ff), fixed), pl.BlockSpec((1, ff), fixed),
                  pl.BlockSpec((HY_DIRS, HY_ORDER, ff, d), lambda i: (0, 0, 0, 0))],
        out_specs=pl.BlockSpec((HY_ORDER, tr, d), lambda i: (0, i, 0)),
        compiler_params=_params(1), name="hyena_taps",
    )(w1p, b1.reshape(1, ff), w2.astype(BF16), b2.reshape(1, ff), w3.astype(BF16), b3.reshape(1, ff), w4r)


def _cos_sin(num, den):
    ang = (num % den).astype(F32) * (2.0 * math.pi / den)
    return jnp.cos(ang), jnp.sin(ang)


def _fft_tables(seq, n2):
    n = 2 * seq
    n1 = n // n2
    half = seq // n2
    ar = jnp.arange(n1, dtype=I32)
    ca, sa = _cos_sin(ar[:, None] * ar[None, :], n1)
    blk = lambda c, s, ax: jnp.concatenate([jnp.concatenate([c, s], ax + 1), jnp.concatenate([-s, c], ax + 1)], ax)
    fa_half = blk(ca[:, :half], sa[:, :half], 0).astype(BF16)
    fa_full = jnp.concatenate([ca, -sa], 0).astype(BF16)
    ia = (blk(ca[:half], -sa[:half], 0) * (1.0 / n)).astype(BF16)
    k = ar[:, None, None] + n1 * jnp.arange(n2, dtype=I32)[None, :, None]
    cb, sb = _cos_sin(k * jnp.arange(n2, dtype=I32)[None, None, :], n)
    fb = blk(cb, sb, 1).astype(BF16)
    gb = blk(cb.transpose(0, 2, 1), -sb.transpose(0, 2, 1), 1).astype(BF16)
    return fa_half, fa_full, ia, fb, gb


def _pack_c(re, im):
    hi = pltpu.bitcast(re.astype(BF16).astype(F32), U32)
    lo = pltpu.bitcast(im.astype(BF16).astype(F32), U32)
    return hi | (lo >> 16)


def _unpack_c(w):
    re = pltpu.bitcast(w & jnp.uint32(0xFFFF0000), F32)
    im = pltpu.bitcast(w << 16, F32)
    return jnp.concatenate([re, im], axis=0).astype(BF16)


def _cmul_split(p, rows, cols, conj):
    a, b, c, d = p[:rows, :cols], p[:rows, cols:], p[rows:, :cols], p[rows:, cols:]
    return (a - d, b + c) if conj else (a + d, b - c)


def _fft_a_kernel(x_ref, f_ref, o_ref, x_scr, *, cplx):
    n1 = f_ref.shape[0] // 2
    per, rows, g, d = x_ref.shape[1:]
    for b in range(per):
        xb = x_ref[0, b].reshape(rows // SUBLANES, SUBLANES, g, d)
        x_scr[b] = jnp.swapaxes(xb, 1, 2)
    for s in range(g):
        z = jnp.concatenate([x_scr[b, :, s].reshape(rows, d) for b in range(per)], axis=0).astype(BF16)
        p = _dot(f_ref[...], z)
        o_ref[0, :, s, :] = _pack_c(p[:n1], p[n1:])


def _fft_a(x5, which, f, cplx):
    _, nb, rows, n2, d = x5.shape
    n1 = f.shape[0] // 2
    per = 2 if cplx else 1
    groups = nb // per
    g = FFT_GROUP
    return pl.pallas_call(
        functools.partial(_fft_a_kernel, cplx=cplx),
        out_shape=jax.ShapeDtypeStruct((groups, n1, n2, d), U32),
        grid=(groups, n2 // g),
        in_specs=[pl.BlockSpec((1, per, rows, g, d), lambda p, j: (which, p, 0, j, 0)),
                  pl.BlockSpec(f.shape, lambda p, j: (0, 0))],
        out_specs=pl.BlockSpec((1, n1, g, d), lambda p, j: (p, 0, j, 0)),
        scratch_shapes=[pltpu.VMEM((per, rows // SUBLANES, g, SUBLANES, d), F32)],
        compiler_params=_params(2), name="fft_stage_a",
    )(x5, f)


def _fft_b_kernel(*refs, conv):
    if conv:
        a_ref, fb_ref, k_ref, gb_ref, o_ref = refs
    else:
        a_ref, fb_ref, o_ref = refs
    g, n2, d = a_ref.shape[1:]
    for s in range(g):
        x2 = _dot(fb_ref[s], _unpack_c(a_ref[0, s]))
        xr, xi = x2[:n2], x2[n2:]
        if not conv:
            o_ref[0, 0, s] = xr
            o_ref[0, 1, s] = xi
            continue
        kr, ki = k_ref[0, 0, s], k_ref[0, 1, s]
        y = jnp.concatenate([xr * kr - xi * ki, xr * ki + xi * kr], axis=0).astype(BF16)
        y2 = _dot(gb_ref[s], y)
        o_ref[0, :, s, :] = _pack_c(y2[:n2], y2[n2:])


def _fft_b(a4, fb, kspec=None, which=0, gb=None):
    groups, n1, n2, d = a4.shape
    conv = kspec is not None
    g = FFT_GROUP
    slab = pl.BlockSpec((1, g, n2, d), lambda k, p: (p, k, 0, 0))
    tab = pl.BlockSpec((g, 2 * n2, 2 * n2), lambda k, p: (k, 0, 0))
    if conv:
        in_specs = [slab, tab, pl.BlockSpec((1, 2, g, n2, d), lambda k, p: (which, 0, k, 0, 0)), tab]
        args = [a4, fb, kspec, gb]
        out_shape = jax.ShapeDtypeStruct((groups, n2, n1, d), U32)
        out_spec = pl.BlockSpec((1, n2, g, d), lambda k, p: (p, 0, k, 0))
    else:
        in_specs = [slab, tab]
        args = [a4, fb]
        out_shape = jax.ShapeDtypeStruct((groups, 2, n1, n2, d), F32)
        out_spec = pl.BlockSpec((1, 2, g, n2, d), lambda k, p: (p, 0, k, 0, 0))
    return pl.pallas_call(
        functools.partial(_fft_b_kernel, conv=conv),
        out_shape=out_shape,
        grid=(n1 // g, groups),
        in_specs=in_specs, out_specs=out_spec,
        compiler_params=_params(2), name="fft_stage_b_conv" if conv else "fft_stage_b",
    )(*args)


def _fft_c_kernel(y_ref, ia_ref, u_ref, g_ref, skip_ref, o_ref):
    half = ia_ref.shape[0] // 2
    g, _, d = y_ref.shape[1:]
    for s in range(g):
        c2 = _dot(ia_ref[...], _unpack_c(y_ref[0, s]))
        o_ref[0, :, s, :] = c2[:half]
        o_ref[1, :, s, :] = c2[half:]
    skip = skip_ref[...][None]
    for b in range(2):
        o_ref[b] = g_ref[0, b] * (o_ref[b] + u_ref[0, b] * skip)


def _fft_c(y4, ia, u5, u_which, g5, g_which, skip):
    groups, n2, n1, d = y4.shape
    half = ia.shape[0] // 2
    g = FFT_GROUP
    return pl.pallas_call(
        _fft_c_kernel,
        out_shape=jax.ShapeDtypeStruct((2 * groups, half, n2, d), F32),
        grid=(groups, n2 // g),
        in_specs=[pl.BlockSpec((1, g, n1, d), lambda p, j: (p, j, 0, 0)),
                  pl.BlockSpec(ia.shape, lambda p, j: (0, 0)),
                  pl.BlockSpec((1, 2, half, g, d), lambda p, j: (u_which, p, 0, j, 0)),
                  pl.BlockSpec((1, 2, half, g, d), lambda p, j: (g_which, p, 0, j, 0)),
                  pl.BlockSpec((1, d), lambda p, j: (0, 0))],
        out_specs=pl.BlockSpec((2, half, g, d), lambda p, j: (p, 0, j, 0)),
        compiler_params=_params(2), name="fft_stage_c",
    )(y4, ia, u5, g5, skip.reshape(1, d))


def _hy_long_conv(zs, taps, skip, nb, seq):
    nsplit, n, d = zs.shape
    n2 = FFT_N2
    half = seq // n2
    n1 = 2 * half
    fa_half, fa_full, ia, fb, gb = _fft_tables(seq, n2)
    kspec = _fft_b(_fft_a(taps.reshape(1, HY_ORDER, n1, n2, d), 0, fa_full, False), fb)
    zs5 = zs.reshape(nsplit, nb, half, n2, d)
    u5, u_which = zs5, 0
    for o in range(HY_ORDER):
        a = _fft_a(u5, u_which, fa_half, True)
        y = _fft_b(a, fb, kspec, o, gb)
        out = _fft_c(y, ia, u5, u_which, zs5, o + 1, skip[o])
        u5, u_which = out[None], 0
    return out.reshape(n, d)


def _short_conv_kernel(u_ref, g_ref, t_ref, skip_ref, ff_ref, tf_ref, fi_ref, o_ref):
    seq, ct = u_ref.shape[-2:]
    n = 2 * seq
    z = jnp.concatenate([u_ref[0, 0], u_ref[0, 1]], axis=1).astype(BF16)
    xr, xi = _cmul_split(_dot(ff_ref[...], z), n, ct, False)
    pt = _dot(tf_ref[...], t_ref[0].astype(BF16))
    kr, ki = pt[:n], -pt[n:]
    y = jnp.concatenate([xr * kr - xi * ki, xr * ki + xi * kr], axis=1).astype(BF16)
    cr, ci = _cmul_split(_dot(fi_ref[...], y), seq, ct, True)
    skip = skip_ref[...]
    o_ref[0] = g_ref[0, 0] * (cr + u_ref[0, 0] * skip)
    o_ref[1] = g_ref[0, 1] * (ci + u_ref[0, 1] * skip)


def _hy_short_conv(zs, taps, skip, nb, seq):
    nsplit, n, d = zs.shape
    nn = 2 * seq
    ar = jnp.arange(nn, dtype=I32)
    cm, sm = _cos_sin(ar[:, None] * ar[None, :], nn)
    ff = jnp.concatenate([cm[:, :seq], sm[:, :seq]], 0).astype(BF16)
    tf = jnp.concatenate([cm, sm], 0).astype(BF16)
    fi = (jnp.concatenate([cm[:seq], sm[:seq]], 0) * (1.0 / nn)).astype(BF16)
    ct = d // 2
    zs4 = zs.reshape(nsplit, nb, seq, d)
    u4, u_which = zs4, 0
    for o in range(HY_ORDER):
        out = pl.pallas_call(
            _short_conv_kernel,
            out_shape=jax.ShapeDtypeStruct((nb, seq, d), F32),
            grid=(nb // 2, d // ct),
            in_specs=[pl.BlockSpec((1, 2, seq, ct), lambda p, j, w=u_which: (w, p, 0, j)),
                      pl.BlockSpec((1, 2, seq, ct), lambda p, j, w=o + 1: (w, p, 0, j)),
                      pl.BlockSpec((1, nn, ct), lambda p, j, w=o: (w, 0, j)),
                      pl.BlockSpec((1, ct), lambda p, j: (0, j)),
                      pl.BlockSpec((2 * nn, seq), lambda p, j: (0, 0)),
                      pl.BlockSpec((2 * nn, nn), lambda p, j: (0, 0)),
                      pl.BlockSpec((2 * seq, nn), lambda p, j: (0, 0))],
            out_specs=pl.BlockSpec((2, seq, ct), lambda p, j: (p, 0, j)),
            compiler_params=_params(2), name="hyena_short_conv",
        )(u4, zs4, taps, skip[o].reshape(1, d), ff, tf, fi)
        u4, u_which = out[None], 0
    return out.reshape(n, d)


def kernel(x, c, ctx, c_ctx, w_mod, b_mod, ln_g, ln_b, hy_w_in, hy_b_in, hy_conv_w, hy_conv_b, hy_f_w1, hy_f_b1, hy_f_w2, hy_f_b2, hy_f_w3, hy_f_b3, hy_f_w4, hy_skip, hy_w_out, hy_b_out, gla_w_in, gla_w_gate, gla_b_gate, gla_norm_g, gla_w_out, gm_w_in, gm_b_in, gm_ln_g, gm_ln_b, gm_ws, gm_bs, gm_w_out, gm_b_out, router_w, router_b, moe_w1, moe_w3, moe_w2):
    B, L, D = x.shape
    Lc = ctx.shape[1]
    depth = w_mod.shape[0]
    E = router_w.shape[1]
    alpha = (2.0 * depth) ** 0.25
    gla_layers = list(range(1, depth, N_MIXERS))
    last_ctx = gla_layers[-1] if gla_layers else -1

    cvec = jnp.zeros((8, D), F32).at[:B].set(c).at[B].set(c_ctx)
    mod = _modulation(cvec, w_mod, b_mod)
    h = _add_pos(x.reshape(B * L, D), L)
    hc = ctx.reshape(B * Lc, D)
    rw_t = router_w.T
    rw_hi = rw_t.astype(BF16)
    rhl = jnp.concatenate([rw_hi, (rw_t - rw_hi.astype(F32)).astype(BF16)], axis=0)
    rb_b = jnp.broadcast_to(router_b[:, None], (E, ROW_TILE))
    zero_cnt = jnp.zeros((E, LANES), F32)
    row2 = lambda v: v.reshape(1, -1)

    for i in range(depth):
        kind, j = i % N_MIXERS, i // N_MIXERS
        ctx_full = i < last_ctx
        ctx_any = i <= last_ctx
        lat = [mod[i, :B, k * D:(k + 1) * D].reshape(B, 1, D) for k in range(6)]
        cm = [jnp.broadcast_to(mod[i, B, k * D:(k + 1) * D].reshape(1, 1, D), (B, 1, D)) for k in range(6)]
        streams_in = [(h, lat, L)] + ([(hc, cm, Lc)] if ctx_any else [])
        pre = []
        if kind == 0:
            w_in = hy_w_in[j].astype(BF16)
            for s_h, s_m, s_len in streams_in[:1 + int(ctx_full)]:
                zs = _hy_in(s_h, s_m[1], s_m[0], w_in, row2(hy_b_in[j]), hy_conv_w[j], row2(hy_conv_b[j]), s_len)
                taps = _hy_taps(s_len, D, hy_f_w1[j], hy_f_b1[j], hy_f_w2[j], hy_f_b2[j], hy_f_w3[j], hy_f_b3[j],
                                hy_f_w4[j])
                if s_len == L:
                    y2 = _hy_long_conv(zs, taps, hy_skip[j], B, s_len)
                else:
                    y2 = _hy_short_conv(zs, taps, hy_skip[j], B, s_len)
                pre.append(("hyena", (y2,), lambda tm, d: [pl.BlockSpec((tm, d), lambda t: (t, 0))]))
            w_out, b_out = hy_w_out[j].astype(BF16), row2(hy_b_out[j])
        elif kind == 1:
            w_main, w_lr, w_g, b_g = _gla_weights(gla_w_in[j], gla_w_gate[j], gla_b_gate[j], D)
            proj = [_gla_in(s_h, s_m[1], s_m[0], w_main, w_lr, w_g, b_g, s_len) for s_h, s_m, s_len in streams_in]
            dvh = D // GLA_HEADS
            dkh = (D // 2) // GLA_HEADS
            zero_state = jnp.zeros((B, GLA_HEADS, dvh, dkh), F32)
            if ctx_any:
                ocf, ocb, s_f, s_b = _gla_scan(*proj[1], zero_state, zero_state, B, Lc)
            else:
                s_f = s_b = zero_state
            o_f, o_b, _, _ = _gla_scan(*proj[0], s_f, s_b, B, L)
            gla_specs = lambda tm, d: [pl.BlockSpec((tm, d), lambda t: (t, 0))] * 3 + [pl.BlockSpec((1, d), lambda t: (0, 0))]
            pre.append(("gla", (o_f, o_b, proj[0][2], row2(gla_norm_g[j])), gla_specs))
            if ctx_full:
                pre.append(("gla", (ocf, ocb, proj[1][2], row2(gla_norm_g[j])), gla_specs))
            w_out, b_out = gla_w_out[j].astype(BF16), jnp.zeros((1, D), F32)
        else:
            w_in = gm_w_in[j].astype(BF16)
            ws = gm_ws[j].astype(BF16)
            bs_exp = jnp.repeat(gm_bs[j].T, D // GM_HEADS, axis=1)
            gm_specs = lambda tm, d: [pl.BlockSpec((tm, d), lambda t: (t, 0))] * 2 + [
                pl.BlockSpec((GM_HEADS, GM_CHUNK, GM_CHUNK), lambda t: (0, 0, 0)), pl.BlockSpec((GM_CHUNK, d), lambda t: (0, 0))]
            for s_h, s_m, s_len in streams_in[:1 + int(ctx_full)]:
                u, vn = _gm_in(s_h, s_m[1], s_m[0], w_in, row2(gm_b_in[j]), row2(gm_ln_g[j]), row2(gm_ln_b[j]), s_len)
                pre.append(("gmlp", (u, vn, ws, bs_exp), gm_specs))
            w_out, b_out = gm_w_out[j].astype(BF16), row2(gm_b_out[j])

        moe_streams = []
        cnt, xs = zero_cnt, None
        cap = sum(s_h.shape[0] for (s_h, _, _), _ in zip(streams_in, pre)) + MOE_ROWS
        for idx, ((s_h, s_m, s_len), (pk, pargs, pspecs)) in enumerate(zip(streams_in, pre)):
            h1, meta, cnt, xs = _post(pk, pargs, pspecs, s_h, w_out, b_out, s_m[2], row2(ln_g[i, 0]),
                                      row2(ln_b[i, 0]), s_m[4], s_m[3], rhl, rb_b, cnt, xs, cap,
                                      idx == len(pre) - 1, s_len, alpha)
            moe_streams.append(dict(h1=h1, meta=meta, g2=s_m[5], lg=row2(ln_g[i, 1]), lb=row2(ln_b[i, 1]),
                                    seq=s_len))
        outs = _moe(moe_streams, cnt, xs, cap, moe_w1, moe_w3, moe_w2, i, alpha)
        h = outs[0]
        if ctx_full:
            hc = outs[1]
    return h.reshape(B, L, D)
```

```python
import functools
import math

import jax
import jax.numpy as jnp
from jax import lax
from jax.experimental import pallas as pl
from jax.experimental.pallas import tpu as pltpu

F32 = jnp.float32
BF16 = jnp.bfloat16
I32 = jnp.int32
U32 = jnp.uint32
HIGHEST = lax.Precision.HIGHEST

GRID_W = 64
N_MIXERS = 3
LN_EPS = 1e-5
HY_ORDER = 2
HY_DIRS = 2
HY_DECAY_TARGET = 1e-2
HY_DECAY_SHORT_PCT = 0.3
HY_DECAY_LONG_PCT = 1.5
GLA_HEADS = 4
GLA_RANK = 16
GLA_TAU = 16.0
GLA_CHUNK = 64
GM_CHUNK = 128
GM_HEADS = 4
N_GROUPS = 4
TOP_K = 2

LANES = 128
SUBLANES = 8
V7X_VMEM_LIMIT_BYTES = 56 * 1024 * 1024
MOE_ROWS = 512
ROW_TILE = 512
DMA_TILE = 256
DMA_UNROLL = 8
COMBINE_SLOTS = 2
POST_SLOTS = 3
FFT_N2 = 128
FFT_GROUP = SUBLANES


def _params(n_grid):
    return pltpu.CompilerParams(dimension_semantics=("arbitrary",) * n_grid,
                                vmem_limit_bytes=V7X_VMEM_LIMIT_BYTES)


def _dot(a, b):
    return jnp.dot(a, b, preferred_element_type=F32)


def _dot_nt(a, b):
    return lax.dot_general(a, b, (((1,), (1,)), ((), ())), preferred_element_type=F32)


def _dot_tn(a, b):
    return lax.dot_general(a, b, (((0,), (0,)), ((), ())), preferred_element_type=F32)


def _rows_to_tiles(x):
    rows, d = x.shape
    nsub = d // LANES
    parts = [x[:, s * LANES:(s + 1) * LANES].reshape(rows // SUBLANES, SUBLANES, LANES) for s in range(nsub)]
    return jnp.swapaxes(jnp.stack(parts, axis=1), 1, 2).reshape(rows, nsub, LANES)


def _tiles_to_rows(x):
    rows, nsub, _ = x.shape
    y = jnp.swapaxes(x.reshape(rows // SUBLANES, SUBLANES, nsub, LANES), 1, 2)
    return jnp.concatenate([y[:, s].reshape(rows, LANES) for s in range(nsub)], axis=1)


def _ln(x, g, b):
    mu = jnp.mean(x, -1, keepdims=True)
    xc = x - mu
    var = jnp.mean(xc * xc, -1, keepdims=True)
    return xc * lax.rsqrt(var + LN_EPS) * g + b


def _silu(x):
    return x * jax.nn.sigmoid(x)


def _gelu_tanh(x):
    return 0.5 * x * (1.0 + jnp.tanh(math.sqrt(2.0 / math.pi) * (x + 0.044715 * (x * x * x))))


def _mod_kernel(c_ref, w_ref, b_ref, o_ref):
    s = _silu(c_ref[...])
    o_ref[0] = _dot(s.astype(BF16), w_ref[0].astype(BF16)) + b_ref[0]


def _modulation(cvec, w_mod, b_mod):
    depth, d, n = w_mod.shape
    tn = n // 4
    return pl.pallas_call(
        _mod_kernel,
        out_shape=jax.ShapeDtypeStruct((depth, 8, n), F32),
        grid=(depth, n // tn),
        in_specs=[pl.BlockSpec((8, d), lambda i, j: (0, 0)),
                  pl.BlockSpec((1, d, tn), lambda i, j: (i, 0, j)),
                  pl.BlockSpec((1, 1, tn), lambda i, j: (i, 0, j))],
        out_specs=pl.BlockSpec((1, 8, tn), lambda i, j: (i, 0, j)),
        compiler_params=_params(2), name="modulation",
    )(cvec, w_mod, b_mod.reshape(depth, 1, n))


def _pos_table_kernel(o_ref, *, q):
    rows, cols = o_ref.shape
    p = lax.broadcasted_iota(I32, (rows, cols), 0).astype(F32)
    lane = lax.broadcasted_iota(I32, (rows, cols), 1)
    j = jnp.where(lane >= q, lane - q, lane).astype(F32)
    omega = jnp.exp(j * (-math.log(10000.0) / q))
    ang = p * omega
    o_ref[...] = jnp.where(lane >= q, jnp.cos(ang), jnp.sin(ang))


def _pos_table(n, d):
    q = d // 4
    return pl.pallas_call(functools.partial(_pos_table_kernel, q=q),
                          out_shape=jax.ShapeDtypeStruct((n, 2 * q), F32), name="pos_table")()


def _add_pos_kernel(x_ref, er_ref, ec_ref, o_ref, *, half):
    tm = x_ref.shape[0]
    reps = tm // GRID_W
    er = er_ref[...]
    er_rows = jnp.broadcast_to(er[:, None, :], (reps, GRID_W, half)).reshape(tm, half)
    ec_rows = jnp.broadcast_to(ec_ref[...][None], (reps, GRID_W, half)).reshape(tm, half)
    o_ref[:, :half] = x_ref[:, :half] + er_rows
    o_ref[:, half:] = x_ref[:, half:] + ec_rows


def _add_pos(x2d, seq):
    n, d = x2d.shape
    half = d // 2
    rows = seq // GRID_W
    er = _pos_table(rows, d)
    ec = _pos_table(GRID_W, d)
    tm = ROW_TILE
    reps = tm // GRID_W
    tps = seq // tm
    return pl.pallas_call(
        functools.partial(_add_pos_kernel, half=half),
        out_shape=jax.ShapeDtypeStruct((n, d), F32),
        grid=(n // tm,),
        in_specs=[pl.BlockSpec((tm, d), lambda i: (i, 0)),
                  pl.BlockSpec((reps, half), lambda i: (i % tps, 0)),
                  pl.BlockSpec((GRID_W, half), lambda i: (0, 0))],
        out_specs=pl.BlockSpec((tm, d), lambda i: (i, 0)),
        compiler_params=_params(1), name="add_pos",
    )(x2d, er, ec)


def _gm_in_kernel(h_ref, sc_ref, sh_ref, w_ref, b_ref, g_ref, bb_ref, u_ref, v_ref):
    d = h_ref.shape[1]
    a = (h_ref[...] * (1.0 + sc_ref[0]) + sh_ref[0]).astype(BF16)
    u_ref[...] = _gelu_tanh(_dot(a, w_ref[:, :d]) + b_ref[:, :d])
    v = _gelu_tanh(_dot(a, w_ref[:, d:]) + b_ref[:, d:])
    v_ref[...] = _ln(v, g_ref[...], bb_ref[...]).astype(BF16)


def _gm_in(h, sc, sh, w, b, g, bb, seq):
    n, d = h.shape
    tm = min(ROW_TILE, seq)
    tps = seq // tm
    row = lambda i: (i, 0)
    per_b = lambda i: (i // tps, 0, 0)
    fixed = lambda i: (0, 0)
    return pl.pallas_call(
        _gm_in_kernel,
        out_shape=(jax.ShapeDtypeStruct((n, d), F32), jax.ShapeDtypeStruct((n, d), BF16)),
        grid=(n // tm,),
        in_specs=[pl.BlockSpec((tm, d), row), pl.BlockSpec((1, 1, d), per_b), pl.BlockSpec((1, 1, d), per_b),
                  pl.BlockSpec((d, 2 * d), fixed), pl.BlockSpec((1, 2 * d), fixed),
                  pl.BlockSpec((1, d), fixed), pl.BlockSpec((1, d), fixed)],
        out_specs=(pl.BlockSpec((tm, d), row), pl.BlockSpec((tm, d), row)),
        compiler_params=_params(1), name="gmlp_in",
    )(h, sc, sh, w, b, g, bb)


def _gm_prologue(u_ref, v_ref, ws_ref, bs_ref):
    tm, d = u_ref.shape
    dh = d // GM_HEADS
    for c in range(tm // GM_CHUNK):
        rows = slice(c * GM_CHUNK, (c + 1) * GM_CHUNK)
        parts = [_dot(ws_ref[g], v_ref[rows, g * dh:(g + 1) * dh]) for g in range(GM_HEADS)]
        vm = jnp.concatenate(parts, axis=1) + bs_ref[...]
        yield rows, (u_ref[rows, :] * vm).astype(BF16)


def _gla_prologue(of_ref, ob_ref, r_ref, ng_ref):
    tm, d = of_ref.shape
    dh = d // GLA_HEADS
    o = of_ref[...] + ob_ref[...]
    parts = []
    for hd in range(GLA_HEADS):
        oh = o[:, hd * dh:(hd + 1) * dh]
        mu = jnp.mean(oh, -1, keepdims=True)
        oc = oh - mu
        var = jnp.mean(oc * oc, -1, keepdims=True)
        parts.append(oc * lax.rsqrt(var + LN_EPS) * ng_ref[:, hd * dh:(hd + 1) * dh])
    y = jnp.concatenate(parts, axis=1) * _silu(r_ref[...])
    yield slice(0, tm), y.astype(BF16)


def _hy_prologue(y_ref):
    yield slice(0, y_ref.shape[0]), y_ref[...].astype(BF16)


_PROLOGUES = {"gmlp": (_gm_prologue, 4), "gla": (_gla_prologue, 4), "hyena": (_hy_prologue, 1)}


def _route_t(scores, biased):
    n_experts, tm = scores.shape
    gsz = n_experts // N_GROUPS
    neg = jnp.float32(-jnp.inf)
    v3 = biased.reshape(N_GROUPS, gsz, tm)
    sub = lax.broadcasted_iota(I32, v3.shape, 1).astype(F32)
    m1 = jnp.max(v3, axis=1, keepdims=True)
    i1 = jnp.min(jnp.where(v3 == m1, sub, float(gsz)), axis=1, keepdims=True)
    v3b = jnp.where(sub == i1, neg, v3)
    m2 = jnp.max(v3b, axis=1, keepdims=True)
    i2 = jnp.min(jnp.where(v3b == m2, sub, float(gsz)), axis=1, keepdims=True)
    gscore = (m1 + m2).reshape(N_GROUPS, tm)
    i1 = i1.reshape(N_GROUPS, tm)
    i2 = i2.reshape(N_GROUPS, tm)
    best, e0, e1 = gscore[0:1], i1[0:1], i2[0:1]
    for g in range(1, N_GROUPS):
        better = gscore[g:g + 1] > best
        best = jnp.where(better, gscore[g:g + 1], best)
        e0 = jnp.where(better, i1[g:g + 1] + float(g * gsz), e0)
        e1 = jnp.where(better, i2[g:g + 1] + float(g * gsz), e1)
    row = lax.broadcasted_iota(I32, scores.shape, 0).astype(F32)
    oh0 = (row == e0).astype(F32)
    oh1 = (row == e1).astype(F32)
    w0 = jnp.sum(oh0 * scores, axis=0, keepdims=True)
    w1 = jnp.sum(oh1 * scores, axis=0, keepdims=True)
    den = w0 + w1
    return e0, e1, oh0, oh1, w0 / den, w1 / den


def _post_kernel(*refs, kind, alpha, cap, aliased, finalize):
    prologue, n_pro = _PROLOGUES[kind]
    pro = refs[:n_pro]
    rest = refs[n_pro:]
    (h_ref, w_ref, b_ref, g1_ref, lg_ref, lb_ref, sc_ref, sh_ref, rhl_ref, rb_ref, cin_ref) = rest[:11]
    rest = rest[11 + int(aliased):]
    (h1_ref, meta_ref, cout_ref, xs_ref, cnt_scr, tok_scr, dvm_scr, idx_scr, zero_scr, cvm_scr, csm_scr,
     isem, rsem, zsem) = rest
    i = pl.program_id(0)
    nstep = pl.num_programs(0)
    tm = h_ref.shape[0]
    n_experts = rb_ref.shape[0]
    slot = i % POST_SLOTS

    def idx_copy(s, k):
        return pltpu.make_async_copy(dvm_scr.at[s, k], idx_scr.at[s, k], isem.at[s])

    def row_copy(s, r, k):
        return pltpu.make_async_copy(tok_scr.at[s, r], xs_ref.at[idx_scr[s, k, r]], rsem.at[s])

    def start_rows(s):
        for k in range(TOP_K):
            idx_copy(s, k).wait()

        def start_row(r, u):
            for k in range(TOP_K):
                row_copy(s, r, k).start(priority=(u + k) % 2)
        _row_loop(tm, start_row)

    def wait_rows(s):
        def wait_row(r, u):
            for k in range(TOP_K):
                row_copy(s, r, k).wait()
        _row_loop(tm, wait_row)

    @pl.when(i == 0)
    def _():
        cnt_scr[...] = cin_ref[...]

    @pl.when(i >= POST_SLOTS)
    def _():
        wait_rows(slot)

    for rows, y in prologue(*pro):
        out = _dot(y, w_ref[...]) + b_ref[...]
        h1_ref[rows, :] = _ln(alpha * h_ref[rows, :] + g1_ref[0] * out, lg_ref[...], lb_ref[...])
    tok = h1_ref[...] * (1.0 + sc_ref[0]) + sh_ref[0]
    tok_scr[slot] = _rows_to_tiles(tok)
    t_hi = tok.astype(BF16)
    t_lo = (tok - t_hi.astype(F32)).astype(BF16)
    p_hi = _dot_nt(rhl_ref[...], t_hi)
    logits = p_hi[:n_experts] + p_hi[n_experts:] + _dot_nt(rhl_ref[0:n_experts, :], t_lo)
    scores = jax.nn.sigmoid(logits)
    e0, e1, oh0, oh1, w0, w1 = _route_t(scores, scores + rb_ref[...])
    r_i = lax.broadcasted_iota(I32, (tm, tm), 0)
    c_i = lax.broadcasted_iota(I32, (tm, tm), 1)
    upper = (r_i < c_i).astype(BF16)
    oh = oh0 + oh1
    carry = cnt_scr[...]
    before = _dot(oh.astype(BF16), upper) + jnp.concatenate([carry] * (tm // LANES), axis=1)
    rank0 = jnp.sum(oh0 * before, axis=0, keepdims=True)
    rank1 = jnp.sum(oh1 * before, axis=0, keepdims=True)
    cnt_scr[...] = carry + jnp.sum(oh, axis=1, keepdims=True)
    zero = jnp.zeros_like(w0)
    meta_ref[0] = jnp.concatenate([e0, e1, rank0, rank1, w0, w1, zero, zero], axis=0)
    cout_ref[...] = cnt_scr[...]

    dvm_scr[slot] = jnp.concatenate([e0 * float(cap) + rank0, e1 * float(cap) + rank1] + [zero] * (SUBLANES - TOP_K),
                                    axis=0).astype(I32)
    for k in range(TOP_K):
        idx_copy(slot, k).start()

    @pl.when(i >= 1)
    def _():
        start_rows((i + POST_SLOTS - 1) % POST_SLOTS)

    @pl.when(i == nstep - 1)
    def _():
        start_rows(slot)
        for back in range(POST_SLOTS - 1, -1, -1):
            @pl.when(i >= back)
            def _():
                wait_rows((i + POST_SLOTS - back) % POST_SLOTS)

        if finalize:
            zero_scr[...] = jnp.zeros_like(zero_scr)
            cvm_scr[...] = cnt_scr[...].astype(I32)
            pltpu.make_async_copy(cvm_scr, csm_scr, zsem).start()
            pltpu.make_async_copy(cvm_scr, csm_scr, zsem).wait()

            def fill_copy(e):
                return pltpu.make_async_copy(zero_scr, xs_ref.at[pl.ds(e * cap + csm_scr[e, 0], MOE_ROWS)], zsem)

            def fill_start(e, c):
                fill_copy(e).start()
                return c

            def fill_wait(e, c):
                fill_copy(e).wait()
                return c

            lax.fori_loop(0, n_experts, fill_start, 0)
            lax.fori_loop(0, n_experts, fill_wait, 0)


def _post(kind, pro_args, pro_specs, h, w, b, g1, lg, lb, sc, sh, rhl, rb, cnt_in, xs, cap, finalize, seq, alpha):
    n, d = h.shape
    ts = d // LANES
    tm = min(ROW_TILE, seq)
    tps = seq // tm
    n_experts = rb.shape[0]
    aliased = xs is not None
    row = lambda i: (i, 0)
    per_b = lambda i: (i // tps, 0, 0)
    fixed = lambda i: (0, 0)
    any_spec = pl.BlockSpec(memory_space=pl.ANY)
    pro_in = list(pro_specs(tm, d))
    in_specs = pro_in + [
        pl.BlockSpec((tm, d), row), pl.BlockSpec((d, d), fixed), pl.BlockSpec((1, d), fixed),
        pl.BlockSpec((1, 1, d), per_b), pl.BlockSpec((1, d), fixed), pl.BlockSpec((1, d), fixed),
        pl.BlockSpec((1, 1, d), per_b), pl.BlockSpec((1, 1, d), per_b),
        pl.BlockSpec((2 * n_experts, d), fixed), pl.BlockSpec((n_experts, tm), fixed),
        pl.BlockSpec((n_experts, LANES), fixed)] + ([any_spec] if aliased else [])
    args = list(pro_args) + [h, w, b, g1, lg, lb, sc, sh, rhl, rb[:, :tm], cnt_in] + ([xs] if aliased else [])
    return pl.pallas_call(
        functools.partial(_post_kernel, kind=kind, alpha=alpha, cap=cap, aliased=aliased, finalize=finalize),
        out_shape=(jax.ShapeDtypeStruct((n, d), F32), jax.ShapeDtypeStruct((n // tm, SUBLANES, tm), F32),
                   jax.ShapeDtypeStruct((n_experts, LANES), F32),
                   jax.ShapeDtypeStruct((n_experts * cap, ts, LANES), F32)),
        grid=(n // tm,),
        in_specs=in_specs,
        out_specs=(pl.BlockSpec((tm, d), row), pl.BlockSpec((1, SUBLANES, tm), lambda i: (i, 0, 0)),
                   pl.BlockSpec((n_experts, LANES), fixed), any_spec),
        scratch_shapes=[pltpu.VMEM((n_experts, LANES), F32), pltpu.VMEM((POST_SLOTS, tm, ts, LANES), F32),
                        pltpu.VMEM((POST_SLOTS, SUBLANES, tm), I32), pltpu.SMEM((POST_SLOTS, TOP_K, tm), I32),
                        pltpu.VMEM((MOE_ROWS, ts, LANES), F32), pltpu.VMEM((n_experts, LANES), I32),
                        pltpu.SMEM((n_experts, LANES), I32), pltpu.SemaphoreType.DMA((POST_SLOTS,)),
                        pltpu.SemaphoreType.DMA((POST_SLOTS,)), pltpu.SemaphoreType.DMA],
        input_output_aliases={len(args) - 1: 3} if aliased else {},
        compiler_params=_params(1), name="post_" + kind,
    )(*args)


def _dest_kernel(meta_ref, pstart_ref, o_ref):
    group, _, tmeta = meta_ref.shape
    n_experts = pstart_ref.shape[0]
    tm = o_ref.shape[2]
    per = tmeta // tm
    row = lax.broadcasted_iota(I32, (n_experts, tmeta), 0).astype(F32)
    for g in range(group):
        meta = meta_ref[g]
        rows = []
        for k in range(TOP_K):
            off = jnp.sum(jnp.where(row == meta[k:k + 1], pstart_ref[...], 0.0), axis=0, keepdims=True)
            rows.append(off + meta[TOP_K + k:TOP_K + k + 1])
        rows += [jnp.zeros_like(rows[0])] * (SUBLANES - TOP_K)
        dest = jnp.concatenate(rows, axis=0).astype(I32)
        for j in range(per):
            o_ref[g * per + j] = dest[:, j * tm:(j + 1) * tm]


def _dest(meta, pstart_b):
    ntile, _, tmeta = meta.shape
    tm = DMA_TILE
    per = tmeta // tm
    n_experts = pstart_b.shape[0]
    group = math.gcd(ntile, 8)
    return pl.pallas_call(
        _dest_kernel,
        out_shape=jax.ShapeDtypeStruct((ntile * per, SUBLANES, tm), I32),
        grid=(ntile // group,),
        in_specs=[pl.BlockSpec((group, SUBLANES, tmeta), lambda i: (i, 0, 0)),
                  pl.BlockSpec((n_experts, tmeta), lambda i: (0, 0))],
        out_specs=pl.BlockSpec((group * per, SUBLANES, tm), lambda i: (i, 0, 0)),
        compiler_params=_params(1), name="moe_dest",
    )(meta, pstart_b[:, :tmeta])


def _row_loop(tm, body):
    def step(it, c):
        for u in range(DMA_UNROLL):
            body(it * DMA_UNROLL + u, u)
        return c
    lax.fori_loop(0, tm // DMA_UNROLL, step, 0)


def _expert_kernel(be_ref, nu_ref, br_ref, x_ref, w1_ref, w3_ref, w2_ref, y_ref, w1_scr, w3_scr, w2_scr):
    del br_ref
    j = pl.program_id(0)
    e = be_ref[j]
    e_prev = be_ref[jnp.maximum(j - 1, 0)]

    @pl.when(j < nu_ref[0])
    def _():
        @pl.when((j == 0) | (e != e_prev))
        def _():
            w1_scr[...] = w1_ref[0, 0].astype(BF16)
            w3_scr[...] = w3_ref[0, 0].astype(BF16)
            w2_scr[...] = w2_ref[0, 0].astype(BF16)

        x = _tiles_to_rows(x_ref[...]).astype(BF16)
        hid = _silu(_dot(x, w1_scr[...])) * _dot(x, w3_scr[...])
        y_ref[...] = _rows_to_tiles(_dot(hid.astype(BF16), w2_scr[...]))

    @pl.when(j >= nu_ref[0])
    def _():
        y_ref[...] = jnp.zeros_like(y_ref)


def _experts(blk_e, n_used, blk_row, xs, nblk, w1, w3, w2, layer):
    _, ts, _ = xs.shape
    d = ts * LANES
    de = w1.shape[-1]
    wmap = lambda j, be, nu, br: (layer, be[j], 0, 0)
    xmap = lambda j, be, nu, br: (br[jnp.maximum(jnp.minimum(j, nu[0] - 1), 0)], 0, 0)
    return pl.pallas_call(
        _expert_kernel,
        out_shape=jax.ShapeDtypeStruct((nblk * MOE_ROWS, ts, LANES), F32),
        grid_spec=pltpu.PrefetchScalarGridSpec(
            num_scalar_prefetch=3, grid=(nblk,),
            in_specs=[pl.BlockSpec((MOE_ROWS, ts, LANES), xmap),
                      pl.BlockSpec((1, 1, d, de), wmap), pl.BlockSpec((1, 1, d, de), wmap),
                      pl.BlockSpec((1, 1, de, d), wmap)],
            out_specs=pl.BlockSpec((MOE_ROWS, ts, LANES), lambda j, be, nu, br: (j, 0, 0)),
            scratch_shapes=[pltpu.VMEM((d, de), BF16), pltpu.VMEM((d, de), BF16), pltpu.VMEM((de, d), BF16)]),
        compiler_params=_params(1), name="moe_experts",
    )(blk_e, n_used, blk_row, xs, w1, w3, w2)


def _combine_kernel(dest_ref, ys_ref, meta_ref, h_ref, g2_ref, lg_ref, lb_ref, o_ref,
                    idx_scr, buf_scr, isem, rsem, *, alpha):
    i = pl.program_id(0)
    nstep = pl.num_programs(0)
    tm = h_ref.shape[0]

    def gather(step, slot):
        for k in range(TOP_K):
            pltpu.make_async_copy(dest_ref.at[step, k], idx_scr.at[slot, k], isem).start()
        for k in range(TOP_K):
            pltpu.make_async_copy(dest_ref.at[step, k], idx_scr.at[slot, k], isem).wait()

        def start_row(r, u):
            for k in range(TOP_K):
                pltpu.make_async_copy(ys_ref.at[idx_scr[slot, k, r]], buf_scr.at[slot, k, r],
                                      rsem.at[slot]).start(priority=(u + k) % 2)

        _row_loop(tm, start_row)

    depth = COMBINE_SLOTS - 1
    slot = i % COMBINE_SLOTS

    @pl.when(i == 0)
    def _():
        for a in range(depth):
            @pl.when(a < nstep)
            def _():
                gather(a, a)

    @pl.when(i + depth < nstep)
    def _():
        gather(i + depth, (i + depth) % COMBINE_SLOTS)

    def wait_row(r, u):
        for k in range(TOP_K):
            pltpu.make_async_copy(ys_ref.at[idx_scr[slot, k, r]], buf_scr.at[slot, k, r], rsem.at[slot]).wait()

    _row_loop(tm, wait_row)
    meta = meta_ref[0]
    wt = jnp.concatenate([meta, jnp.zeros((LANES - SUBLANES, tm), F32)], axis=0).T
    f = wt[:, 2 * TOP_K:2 * TOP_K + 1] * _tiles_to_rows(buf_scr[slot, 0])
    for k in range(1, TOP_K):
        f = f + wt[:, 2 * TOP_K + k:2 * TOP_K + k + 1] * _tiles_to_rows(buf_scr[slot, k])
    o_ref[...] = _ln(alpha * h_ref[...] + g2_ref[0] * f, lg_ref[...], lb_ref[...])


def _combine(dest, ys, meta, h, g2, lg, lb, seq, alpha):
    n, d = h.shape
    tm = DMA_TILE
    tps = seq // tm
    per = meta.shape[2] // tm
    row = lambda i: (i, 0)
    fixed = lambda i: (0, 0)
    return pl.pallas_call(
        functools.partial(_combine_kernel, alpha=alpha),
        out_shape=jax.ShapeDtypeStruct((n, d), F32),
        grid=(n // tm,),
        in_specs=[pl.BlockSpec(memory_space=pl.ANY), pl.BlockSpec(memory_space=pl.ANY),
                  pl.BlockSpec((1, SUBLANES, tm), lambda i: (i // per, 0, i % per)), pl.BlockSpec((tm, d), row),
                  pl.BlockSpec((1, 1, d), lambda i: (i // tps, 0, 0)),
                  pl.BlockSpec((1, d), fixed), pl.BlockSpec((1, d), fixed)],
        out_specs=pl.BlockSpec((tm, d), row),
        scratch_shapes=[pltpu.SMEM((COMBINE_SLOTS, TOP_K, tm), I32),
                        pltpu.VMEM((COMBINE_SLOTS, TOP_K, tm, d // LANES, LANES), F32),
                        pltpu.SemaphoreType.DMA, pltpu.SemaphoreType.DMA((COMBINE_SLOTS,))],
        compiler_params=_params(1), name="moe_combine",
    )(dest, ys, meta, h, g2, lg, lb)


def _moe(streams, counts, xs, cap, w1, w3, w2, layer, alpha):
    n_experts = w1.shape[1]
    n_assign = TOP_K * sum(s["h1"].shape[0] for s in streams)
    nblk = n_assign // MOE_ROWS + n_experts
    cnt = counts[:, 0].astype(I32)
    padded = (cnt + MOE_ROWS - 1) // MOE_ROWS * MOE_ROWS
    pend = jnp.cumsum(padded)
    pstart = pend - padded
    n_used = (pend[-1] // MOE_ROWS).astype(I32).reshape(1)
    blk = jnp.arange(nblk, dtype=I32) * MOE_ROWS
    blk_e = jnp.sum((pend[None, :] <= blk[:, None]).astype(I32), axis=1)
    last_e = jnp.sum((pend <= jnp.maximum(pend[-1] - 1, 0)).astype(I32))
    blk_e = jnp.minimum(blk_e, last_e).astype(I32)
    blk_row = ((blk_e * cap + blk - pstart[blk_e]) // MOE_ROWS).astype(I32)
    pstart_b = jnp.broadcast_to(pstart.astype(F32)[:, None], (n_experts, ROW_TILE))
    dests = [_dest(s["meta"], pstart_b) for s in streams]
    ys = _experts(blk_e, n_used, blk_row, xs, nblk, w1, w3, w2, layer)
    return [_combine(dst, ys, s["meta"], s["h1"], s["g2"], s["lg"], s["lb"], s["seq"], alpha)
            for s, dst in zip(streams, dests)]


def _log_sigmoid(x):
    return jnp.minimum(x, 0.0) - jnp.log(1.0 + jnp.exp(-jnp.abs(x)))


def _gla_in_kernel(h_ref, sc_ref, sh_ref, w_ref, wl_ref, wg_ref, bg_ref, qk_ref, v_ref, r_ref, g_ref, *, qscale):
    tm, d = h_ref.shape
    a = (h_ref[...] * (1.0 + sc_ref[0]) + sh_ref[0]).astype(BF16)
    qk = _dot(a, w_ref[:, :d])
    lane = lax.broadcasted_iota(I32, (tm, d), 1)
    qk_ref[...] = jnp.where(lane < d // 2, qk * qscale, qk)
    v_ref[...] = _dot(a, w_ref[:, d:2 * d]).astype(BF16)
    r_ref[...] = _dot(a, w_ref[:, 2 * d:])
    lr = _dot(a, wl_ref[...])
    gpre = _dot(lr.astype(BF16), wg_ref[...]) + bg_ref[...]
    g_ref[...] = _log_sigmoid(gpre) * (1.0 / GLA_TAU)


def _gla_weights(w_in, w_gate, b_gate, d):
    dk = d // 2
    n_lr = HY_DIRS * GLA_RANK
    w_main = w_in[:, :2 * dk + 2 * d].astype(BF16)
    w_lr = jnp.zeros((d, LANES), F32).at[:, :n_lr].set(w_in[:, 2 * dk + 2 * d:]).astype(BF16)
    w_g = jnp.zeros((LANES, HY_DIRS * dk), F32)
    for dr in range(HY_DIRS):
        w_g = w_g.at[dr * GLA_RANK:(dr + 1) * GLA_RANK, dr * dk:(dr + 1) * dk].set(w_gate[dr])
    return w_main, w_lr, w_g.astype(BF16), b_gate.reshape(1, HY_DIRS * dk)


def _gla_in(h, sc, sh, w_main, w_lr, w_g, b_g, seq):
    n, d = h.shape
    tm = min(ROW_TILE, seq)
    tps = seq // tm
    row = lambda i: (i, 0)
    per_b = lambda i: (i // tps, 0, 0)
    fixed = lambda i: (0, 0)
    qscale = float(((d // 2) // GLA_HEADS) ** -0.5)
    return pl.pallas_call(
        functools.partial(_gla_in_kernel, qscale=qscale),
        out_shape=(jax.ShapeDtypeStruct((n, d), F32), jax.ShapeDtypeStruct((n, d), BF16),
                   jax.ShapeDtypeStruct((n, d), F32), jax.ShapeDtypeStruct((n, d), F32)),
        grid=(n // tm,),
        in_specs=[pl.BlockSpec((tm, d), row), pl.BlockSpec((1, 1, d), per_b), pl.BlockSpec((1, 1, d), per_b),
                  pl.BlockSpec((d, 3 * d), fixed), pl.BlockSpec((d, LANES), fixed),
                  pl.BlockSpec((LANES, d), fixed), pl.BlockSpec((1, d), fixed)],
        out_specs=tuple(pl.BlockSpec((tm, d), row) for _ in range(4)),
        compiler_params=_params(1), name="gla_in",
    )(h, sc, sh, w_main, w_lr, w_g, b_g)


def _gla_direction(q_ref, k_ref, v_ref, g_ref, st_scr, o_ref, reverse):
    rt, dk = q_ref.shape
    cs = GLA_CHUNK
    nch = rt // cs
    g = g_ref[...]
    pos = lax.broadcasted_iota(I32, (rt, dk), 0) % cs
    b = g
    sh = 1
    while sh < cs:
        if reverse:
            b = b + jnp.where(pos < cs - sh, pltpu.roll(b, rt - sh, 0), 0.0)
        else:
            b = b + jnp.where(pos >= sh, pltpu.roll(b, sh, 0), 0.0)
        sh *= 2
    edge = 0 if reverse else cs - 1
    b3 = b.reshape(nch, cs, dk)
    b_edge = b3[:, edge:edge + 1, :]
    q = q_ref[...]
    k = k_ref[...]
    qe = (q * jnp.exp(b)).astype(BF16)
    ke = (k * jnp.exp(-b)).astype(BF16)
    kd = (k.reshape(nch, cs, dk) * jnp.exp(b_edge - b3)).astype(BF16)
    decay = jnp.exp(b_edge)
    r_i = lax.broadcasted_iota(I32, (rt, rt), 0)
    c_i = lax.broadcasted_iota(I32, (rt, rt), 1)
    same = (r_i // cs) == (c_i // cs)
    tri = (c_i >= r_i) if reverse else (c_i <= r_i)
    att = jnp.where(same & tri, _dot_nt(qe, ke), 0.0).astype(BF16)
    v = v_ref[...]
    intra = _dot(att, v)
    st = st_scr[...]
    order = range(nch - 1, -1, -1) if reverse else range(nch)
    for j in order:
        rows = slice(j * cs, (j + 1) * cs)
        o_ref[rows, :] = intra[rows, :] + _dot_nt(qe[rows, :], st.astype(BF16))
        st = st * decay[j] + _dot_tn(v[rows, :], kd[j])
    st_scr[...] = st


def _gla_scan_kernel(qkf, vf, gf, qkb, vb, gb, s0f, s0b, of, ob, sfo, sbo, sf_scr, sb_scr):
    c = pl.program_id(1)
    nh = GLA_HEADS
    d = qkf.shape[1]
    dkh, dvh = (d // 2) // nh, d // nh

    @pl.when(c == 0)
    def _():
        sf_scr[...] = s0f[0]
        sb_scr[...] = s0b[0]

    for hd in range(nh):
        qs, ks, vs = pl.ds(hd * dkh, dkh), pl.ds(d // 2 + hd * dkh, dkh), pl.ds(hd * dvh, dvh)
        _gla_direction(qkf.at[:, qs], qkf.at[:, ks], vf.at[:, vs], gf.at[:, qs], sf_scr.at[hd], of.at[:, vs], False)
        _gla_direction(qkb.at[:, qs], qkb.at[:, ks], vb.at[:, vs], gb.at[:, ks], sb_scr.at[hd], ob.at[:, vs], True)
    sfo[0] = sf_scr[...]
    sbo[0] = sb_scr[...]


def _gla_scan(qk, v, r, g, s0f, s0b, nb, seq):
    del r
    n, d = qk.shape
    nh = GLA_HEADS
    dkh, dvh = (d // 2) // nh, d // nh
    rt = min(ROW_TILE, seq)
    npb = seq // rt
    fw = pl.BlockSpec((rt, d), lambda b, c: (b * npb + c, 0))
    bw = pl.BlockSpec((rt, d), lambda b, c: (b * npb + npb - 1 - c, 0))
    st = pl.BlockSpec((1, nh, dvh, dkh), lambda b, c: (b, 0, 0, 0))
    state = jax.ShapeDtypeStruct((nb, nh, dvh, dkh), F32)
    return pl.pallas_call(
        _gla_scan_kernel,
        out_shape=(jax.ShapeDtypeStruct((n, d), F32), jax.ShapeDtypeStruct((n, d), F32), state, state),
        grid=(nb, npb),
        in_specs=[fw, fw, fw, bw, bw, bw, st, st],
        out_specs=(fw, bw, st, st),
        scratch_shapes=[pltpu.VMEM((nh, dvh, dkh), F32), pltpu.VMEM((nh, dvh, dkh), F32)],
        compiler_params=_params(2), name="gla_scan",
    )(qk, v, g, qk, v, g, s0f, s0b)


def _hy_in_kernel(h_ref, hp_ref, hn_ref, sc_ref, sh_ref, w_ref, b_ref, cw_ref, cb_ref, o_ref, a_scr, z_scr,
                  *, tps, halo):
    i = pl.program_id(0)
    tm, d = h_ref.shape
    sc = 1.0 + sc_ref[0]
    sh = sh_ref[0]
    a_scr[0:halo, :] = (hp_ref[...] * sc + sh).astype(BF16)
    a_scr[halo:halo + tm, :] = (h_ref[...] * sc + sh).astype(BF16)
    a_scr[halo + tm:, :] = (hn_ref[...] * sc + sh).astype(BF16)
    row = lax.broadcasted_iota(I32, (tm, 1), 0)
    first = jnp.where(i % tps == 0, 0, -1)
    last = jnp.where(i % tps == tps - 1, tm - 1, -1)
    for c in range(o_ref.shape[0]):
        cols = slice(c * d, (c + 1) * d)
        z_scr[...] = _dot(a_scr[...], w_ref[:, cols]) + b_ref[:, cols]
        zp = jnp.where(row == first, 0.0, z_scr[pl.ds(halo - 1, tm), :])
        zc = z_scr[pl.ds(halo, tm), :]
        zn = jnp.where(row == last, 0.0, z_scr[pl.ds(halo + 1, tm), :])
        o_ref[c] = cw_ref[0:1, cols] * zp + cw_ref[1:2, cols] * zc + cw_ref[2:3, cols] * zn + cb_ref[:, cols]


def _hy_in(h, sc, sh, w, b, cw, cb, seq):
    n, d = h.shape
    nsplit = w.shape[1] // d
    halo = 16
    tm = min(ROW_TILE, seq)
    tps = seq // tm
    hb = tm // halo
    row = lambda i: (i, 0)
    per_b = lambda i: (i // tps, 0, 0)
    fixed = lambda i: (0, 0)
    return pl.pallas_call(
        functools.partial(_hy_in_kernel, tps=tps, halo=halo),
        out_shape=jax.ShapeDtypeStruct((nsplit, n, d), F32),
        grid=(n // tm,),
        in_specs=[pl.BlockSpec((tm, d), row),
                  pl.BlockSpec((halo, d), lambda i: (jnp.maximum(i * hb - 1, 0), 0)),
                  pl.BlockSpec((halo, d), lambda i: (jnp.minimum((i + 1) * hb, n // halo - 1), 0)),
                  pl.BlockSpec((1, 1, d), per_b), pl.BlockSpec((1, 1, d), per_b),
                  pl.BlockSpec((d, nsplit * d), fixed), pl.BlockSpec((1, nsplit * d), fixed),
                  pl.BlockSpec((cw.shape[0], nsplit * d), fixed), pl.BlockSpec((1, nsplit * d), fixed)],
        out_specs=pl.BlockSpec((nsplit, tm, d), lambda i: (0, i, 0)),
        scratch_shapes=[pltpu.VMEM((tm + 2 * halo, d), BF16), pltpu.VMEM((tm + 2 * halo, d), F32)],
        compiler_params=_params(1), name="hyena_in",
    )(h, h, h, sc, sh, w, b, cw, cb)


def _hy_filter_rows(t, seq, w1_ref, b1_ref, w2_ref, b2_ref, w3_ref, b3_ref, n_bands):
    rows = t.shape[0]
    tf = t.astype(F32)
    t_lin = tf * (1.0 / (seq - 1))
    wpos = tf * (2.0 * math.pi / seq)
    lane = lax.broadcasted_iota(I32, (rows, LANES), 1)
    jb = jnp.where(lane > n_bands, lane - n_bands - 1, lane - 1).astype(F32)
    band = 1e-4 + jb * ((n_bands - 1 - 1e-4) / (n_bands - 1))
    ang = band * wpos
    z = jnp.where(lane == 0, t_lin,
                  jnp.where(lane <= n_bands, jnp.cos(ang), jnp.where(lane <= 2 * n_bands, -jnp.sin(ang), 0.0)))
    hf = jnp.sin(_dot(z.astype(BF16), w1_ref[...]) + b1_ref[...])
    hf = jnp.sin(_dot(hf.astype(BF16), w2_ref[...]) + b2_ref[...])
    hf = jnp.sin(_dot(hf.astype(BF16), w3_ref[...]) + b3_ref[...])
    return hf.astype(BF16), t_lin


def _hy_taps_kernel(w1_ref, b1_ref, w2_ref, b2_ref, w3_ref, b3_ref, w4_ref, o_ref, *, seq, n_bands):
    i = pl.program_id(0)
    n_ord, tr, d = o_ref.shape
    mlp = (w1_ref, b1_ref, w2_ref, b2_ref, w3_ref, b3_ref)
    r0 = i * tr
    second = (r0 >= seq).astype(I32)
    n_idx = r0 + lax.broadcasted_iota(I32, (tr, 1), 0)
    t = jnp.where(second == 1, 2 * seq - n_idx, n_idx)
    hf, t_lin = _hy_filter_rows(t, seq, *mlp, n_bands)
    c_idx = lax.broadcasted_iota(I32, (1, d), 1).astype(F32)
    min_decay = math.log(HY_DECAY_TARGET) / HY_DECAY_LONG_PCT
    max_decay = math.log(HY_DECAY_TARGET) / HY_DECAY_SHORT_PCT
    delta = jnp.abs(min_decay + c_idx * ((max_decay - min_decay) / (d - 1)))
    window = jnp.where(n_idx == seq, 0.0, jnp.exp(-t_lin * delta))
    for o in range(n_ord):
        o_ref[o] = _dot(hf, w4_ref[second, o]) * window

    @pl.when(i == 0)
    def _():
        hf0, _ = _hy_filter_rows(jnp.zeros((8, 1), I32), seq, *mlp, n_bands)
        first = lax.broadcasted_iota(I32, (8, 1), 0) == 0
        for o in range(n_ord):
            o_ref[o, 0:8, :] = o_ref[o, 0:8, :] + jnp.where(first, _dot(hf0, w4_ref[1, o]), 0.0)


def _hy_taps(seq, d, w1, b1, w2, b2, w3, b3, w4):
    emb, ff = w1.shape
    n_bands = (emb - 1) // 2
    w1p = jnp.zeros((LANES, ff), F32).at[:emb].set(w1).astype(BF16)
    w4r = w4.reshape(ff, HY_ORDER, HY_DIRS, d).transpose(2, 1, 0, 3).astype(BF16)
    tr = min(ROW_TILE, seq)
    fixed = lambda i: (0, 0)
    return pl.pallas_call(
        functools.partial(_hy_taps_kernel, seq=seq, n_bands=n_bands),
        out_shape=jax.ShapeDtypeStruct((HY_ORDER, 2 * seq, d), F32),
        grid=(2 * seq // tr,),
        in_specs=[pl.BlockSpec((LANES, ff), fixed), pl.BlockSpec((1, ff), fixed),
                  pl.BlockSpec((ff, ff), fixed), pl.BlockSpec((1, ff), fixed),
                  pl.BlockSpec((ff, ff), fixed), pl.BlockSpec((1, ff), fixed),
                  pl.BlockSpec((HY_DIRS, HY_ORDER, ff, d), lambda i: (0, 0, 0, 0))],
        out_specs=pl.BlockSpec((HY_ORDER, tr, d), lambda i: (0, i, 0)),
        compiler_params=_params(1), name="hyena_taps",
    )(w1p, b1.reshape(1, ff), w2.astype(BF16), b2.reshape(1, ff), w3.astype(BF16), b3.reshape(1, ff), w4r)


def _cos_sin(num, den):
    ang = (num % den).astype(F32) * (2.0 * math.pi / den)
    return jnp.cos(ang), jnp.sin(ang)


def _fft_tables(seq, n2):
    n = 2 * seq
    n1 = n // n2
    half = seq // n2
    ar = jnp.arange(n1, dtype=I32)
    ca, sa = _cos_sin(ar[:, None] * ar[None, :], n1)
    blk = lambda c, s, ax: jnp.concatenate([jnp.concatenate([c, s], ax + 1), jnp.concatenate([-s, c], ax + 1)], ax)
    fa_half = blk(ca[:, :half], sa[:, :half], 0).astype(BF16)
    fa_full = jnp.concatenate([ca, -sa], 0).astype(BF16)
    ia = (blk(ca[:half], -sa[:half], 0) * (1.0 / n)).astype(BF16)
    k = ar[:, None, None] + n1 * jnp.arange(n2, dtype=I32)[None, :, None]
    cb, sb = _cos_sin(k * jnp.arange(n2, dtype=I32)[None, None, :], n)
    fb = blk(cb, sb, 1).astype(BF16)
    gb = blk(cb.transpose(0, 2, 1), -sb.transpose(0, 2, 1), 1).astype(BF16)
    return fa_half, fa_full, ia, fb, gb


def _pack_c(re, im):
    hi = pltpu.bitcast(re.astype(BF16).astype(F32), U32)
    lo = pltpu.bitcast(im.astype(BF16).astype(F32), U32)
    return hi | (lo >> 16)


def _unpack_c(w):
    re = pltpu.bitcast(w & jnp.uint32(0xFFFF0000), F32)
    im = pltpu.bitcast(w << 16, F32)
    return jnp.concatenate([re, im], axis=0).astype(BF16)


def _cmul_split(p, rows, cols, conj):
    a, b, c, d = p[:rows, :cols], p[:rows, cols:], p[rows:, :cols], p[rows:, cols:]
    return (a - d, b + c) if conj else (a + d, b - c)


def _fft_a_kernel(x_ref, f_ref, o_ref, x_scr, *, cplx):
    n1 = f_ref.shape[0] // 2
    per, rows, g, d = x_ref.shape[1:]
    for b in range(per):
        xb = x_ref[0, b].reshape(rows // SUBLANES, SUBLANES, g, d)
        x_scr[b] = jnp.swapaxes(xb, 1, 2)
    for s in range(g):
        z = jnp.concatenate([x_scr[b, :, s].reshape(rows, d) for b in range(per)], axis=0).astype(BF16)
        p = _dot(f_ref[...], z)
        o_ref[0, :, s, :] = _pack_c(p[:n1], p[n1:])


def _fft_a(x5, which, f, cplx):
    _, nb, rows, n2, d = x5.shape
    n1 = f.shape[0] // 2
    per = 2 if cplx else 1
    groups = nb // per
    g = FFT_GROUP
    return pl.pallas_call(
        functools.partial(_fft_a_kernel, cplx=cplx),
        out_shape=jax.ShapeDtypeStruct((groups, n1, n2, d), U32),
        grid=(groups, n2 // g),
        in_specs=[pl.BlockSpec((1, per, rows, g, d), lambda p, j: (which, p, 0, j, 0)),
                  pl.BlockSpec(f.shape, lambda p, j: (0, 0))],
        out_specs=pl.BlockSpec((1, n1, g, d), lambda p, j: (p, 0, j, 0)),
        scratch_shapes=[pltpu.VMEM((per, rows // SUBLANES, g, SUBLANES, d), F32)],
        compiler_params=_params(2), name="fft_stage_a",
    )(x5, f)


def _fft_b_kernel(*refs, conv):
    if conv:
        a_ref, fb_ref, k_ref, gb_ref, o_ref = refs
    else:
        a_ref, fb_ref, o_ref = refs
    g, n2, d = a_ref.shape[1:]
    for s in range(g):
        x2 = _dot(fb_ref[s], _unpack_c(a_ref[0, s]))
        xr, xi = x2[:n2], x2[n2:]
        if not conv:
            o_ref[0, 0, s] = xr
            o_ref[0, 1, s] = xi
            continue
        kr, ki = k_ref[0, 0, s], k_ref[0, 1, s]
        y = jnp.concatenate([xr * kr - xi * ki, xr * ki + xi * kr], axis=0).astype(BF16)
        y2 = _dot(gb_ref[s], y)
        o_ref[0, :, s, :] = _pack_c(y2[:n2], y2[n2:])


def _fft_b(a4, fb, kspec=None, which=0, gb=None):
    groups, n1, n2, d = a4.shape
    conv = kspec is not None
    g = FFT_GROUP
    slab = pl.BlockSpec((1, g, n2, d), lambda k, p: (p, k, 0, 0))
    tab = pl.BlockSpec((g, 2 * n2, 2 * n2), lambda k, p: (k, 0, 0))
    if conv:
        in_specs = [slab, tab, pl.BlockSpec((1, 2, g, n2, d), lambda k, p: (which, 0, k, 0, 0)), tab]
        args = [a4, fb, kspec, gb]
        out_shape = jax.ShapeDtypeStruct((groups, n2, n1, d), U32)
        out_spec = pl.BlockSpec((1, n2, g, d), lambda k, p: (p, 0, k, 0))
    else:
        in_specs = [slab, tab]
        args = [a4, fb]
        out_shape = jax.ShapeDtypeStruct((groups, 2, n1, n2, d), F32)
        out_spec = pl.BlockSpec((1, 2, g, n2, d), lambda k, p: (p, 0, k, 0, 0))
    return pl.pallas_call(
        functools.partial(_fft_b_kernel, conv=conv),
        out_shape=out_shape,
        grid=(n1 // g, groups),
        in_specs=in_specs, out_specs=out_spec,
        compiler_params=_params(2), name="fft_stage_b_conv" if conv else "fft_stage_b",
    )(*args)


def _fft_c_kernel(y_ref, ia_ref, u_ref, g_ref, skip_ref, o_ref):
    half = ia_ref.shape[0] // 2
    g, _, d = y_ref.shape[1:]
    for s in range(g):
        c2 = _dot(ia_ref[...], _unpack_c(y_ref[0, s]))
        o_ref[0, :, s, :] = c2[:half]
        o_ref[1, :, s, :] = c2[half:]
    skip = skip_ref[...][None]
    for b in range(2):
        o_ref[b] = g_ref[0, b] * (o_ref[b] + u_ref[0, b] * skip)


def _fft_c(y4, ia, u5, u_which, g5, g_which, skip):
    groups, n2, n1, d = y4.shape
    half = ia.shape[0] // 2
    g = FFT_GROUP
    return pl.pallas_call(
        _fft_c_kernel,
        out_shape=jax.ShapeDtypeStruct((2 * groups, half, n2, d), F32),
        grid=(groups, n2 // g),
        in_specs=[pl.BlockSpec((1, g, n1, d), lambda p, j: (p, j, 0, 0)),
                  pl.BlockSpec(ia.shape, lambda p, j: (0, 0)),
                  pl.BlockSpec((1, 2, half, g, d), lambda p, j: (u_which, p, 0, j, 0)),
                  pl.BlockSpec((1, 2, half, g, d), lambda p, j: (g_which, p, 0, j, 0)),
                  pl.BlockSpec((1, d), lambda p, j: (0, 0))],
        out_specs=pl.BlockSpec((2, half, g, d), lambda p, j: (p, 0, j, 0)),
        compiler_params=_params(2), name="fft_stage_c",
    )(y4, ia, u5, g5, skip.reshape(1, d))


def _hy_long_conv(zs, taps, skip, nb, seq):
    nsplit, n, d = zs.shape
    n2 = FFT_N2
    half = seq // n2
    n1 = 2 * half
    fa_half, fa_full, ia, fb, gb = _fft_tables(seq, n2)
    kspec = _fft_b(_fft_a(taps.reshape(1, HY_ORDER, n1, n2, d), 0, fa_full, False), fb)
    zs5 = zs.reshape(nsplit, nb, half, n2, d)
    u5, u_which = zs5, 0
    for o in range(HY_ORDER):
        a = _fft_a(u5, u_which, fa_half, True)
        y = _fft_b(a, fb, kspec, o, gb)
        out = _fft_c(y, ia, u5, u_which, zs5, o + 1, skip[o])
        u5, u_which = out[None], 0
    return out.reshape(n, d)


def _short_conv_kernel(u_ref, g_ref, t_ref, skip_ref, ff_ref, tf_ref, fi_ref, o_ref):
    seq, ct = u_ref.shape[-2:]
    n = 2 * seq
    z = jnp.concatenate([u_ref[0, 0], u_ref[0, 1]], axis=1).astype(BF16)
    xr, xi = _cmul_split(_dot(ff_ref[...], z), n, ct, False)
    pt = _dot(tf_ref[...], t_ref[0].astype(BF16))
    kr, ki = pt[:n], -pt[n:]
    y = jnp.concatenate([xr * kr - xi * ki, xr * ki + xi * kr], axis=1).astype(BF16)
    cr, ci = _cmul_split(_dot(fi_ref[...], y), seq, ct, True)
    skip = skip_ref[...]
    o_ref[0] = g_ref[0, 0] * (cr + u_ref[0, 0] * skip)
    o_ref[1] = g_ref[0, 1] * (ci + u_ref[0, 1] * skip)


def _hy_short_conv(zs, taps, skip, nb, seq):
    nsplit, n, d = zs.shape
    nn = 2 * seq
    ar = jnp.arange(nn, dtype=I32)
    cm, sm = _cos_sin(ar[:, None] * ar[None, :], nn)
    ff = jnp.concatenate([cm[:, :seq], sm[:, :seq]], 0).astype(BF16)
    tf = jnp.concatenate([cm, sm], 0).astype(BF16)
    fi = (jnp.concatenate([cm[:seq], sm[:seq]], 0) * (1.0 / nn)).astype(BF16)
    ct = d // 2
    zs4 = zs.reshape(nsplit, nb, seq, d)
    u4, u_which = zs4, 0
    for o in range(HY_ORDER):
        out = pl.pallas_call(
            _short_conv_kernel,
            out_shape=jax.ShapeDtypeStruct((nb, seq, d), F32),
            grid=(nb // 2, d // ct),
            in_specs=[pl.BlockSpec((1, 2, seq, ct), lambda p, j, w=u_which: (w, p, 0, j)),
                      pl.BlockSpec((1, 2, seq, ct), lambda p, j, w=o + 1: (w, p, 0, j)),
                      pl.BlockSpec((1, nn, ct), lambda p, j, w=o: (w, 0, j)),
                      pl.BlockSpec((1, ct), lambda p, j: (0, j)),
                      pl.BlockSpec((2 * nn, seq), lambda p, j: (0, 0)),
                      pl.BlockSpec((2 * nn, nn), lambda p, j: (0, 0)),
                      pl.BlockSpec((2 * seq, nn), lambda p, j: (0, 0))],
            out_specs=pl.BlockSpec((2, seq, ct), lambda p, j: (p, 0, j)),
            compiler_params=_params(2), name="hyena_short_conv",
        )(u4, zs4, taps, skip[o].reshape(1, d), ff, tf, fi)
        u4, u_which = out[None], 0
    return out.reshape(n, d)


def kernel(x, c, ctx, c_ctx, w_mod, b_mod, ln_g, ln_b, hy_w_in, hy_b_in, hy_conv_w, hy_conv_b, hy_f_w1, hy_f_b1, hy_f_w2, hy_f_b2, hy_f_w3, hy_f_b3, hy_f_w4, hy_skip, hy_w_out, hy_b_out, gla_w_in, gla_w_gate, gla_b_gate, gla_norm_g, gla_w_out, gm_w_in, gm_b_in, gm_ln_g, gm_ln_b, gm_ws, gm_bs, gm_w_out, gm_b_out, router_w, router_b, moe_w1, moe_w3, moe_w2):
    B, L, D = x.shape
    Lc = ctx.shape[1]
    depth = w_mod.shape[0]
    E = router_w.shape[1]
    alpha = (2.0 * depth) ** 0.25
    gla_layers = list(range(1, depth, N_MIXERS))
    last_ctx = gla_layers[-1] if gla_layers else -1

    cvec = jnp.zeros((8, D), F32).at[:B].set(c).at[B].set(c_ctx)
    mod = _modulation(cvec, w_mod, b_mod)
    h = _add_pos(x.reshape(B * L, D), L)
    hc = ctx.reshape(B * Lc, D)
    rw_t = router_w.T
    rw_hi = rw_t.astype(BF16)
    rhl = jnp.concatenate([rw_hi, (rw_t - rw_hi.astype(F32)).astype(BF16)], axis=0)
    rb_b = jnp.broadcast_to(router_b[:, None], (E, ROW_TILE))
    zero_cnt = jnp.zeros((E, LANES), F32)
    row2 = lambda v: v.reshape(1, -1)

    for i in range(depth):
        kind, j = i % N_MIXERS, i // N_MIXERS
        ctx_full = i < last_ctx
        ctx_any = i <= last_ctx
        lat = [mod[i, :B, k * D:(k + 1) * D].reshape(B, 1, D) for k in range(6)]
        cm = [jnp.broadcast_to(mod[i, B, k * D:(k + 1) * D].reshape(1, 1, D), (B, 1, D)) for k in range(6)]
        streams_in = [(h, lat, L)] + ([(hc, cm, Lc)] if ctx_any else [])
        pre = []
        if kind == 0:
            w_in = hy_w_in[j].astype(BF16)
            for s_h, s_m, s_len in streams_in[:1 + int(ctx_full)]:
                zs = _hy_in(s_h, s_m[1], s_m[0], w_in, row2(hy_b_in[j]), hy_conv_w[j], row2(hy_conv_b[j]), s_len)
                taps = _hy_taps(s_len, D, hy_f_w1[j], hy_f_b1[j], hy_f_w2[j], hy_f_b2[j], hy_f_w3[j], hy_f_b3[j],
                                hy_f_w4[j])
                if s_len == L:
                    y2 = _hy_long_conv(zs, taps, hy_skip[j], B, s_len)
                else:
                    y2 = _hy_short_conv(zs, taps, hy_skip[j], B, s_len)
                pre.append(("hyena", (y2,), lambda tm, d: [pl.BlockSpec((tm, d), lambda t: (t, 0))]))
            w_out, b_out = hy_w_out[j].astype(BF16), row2(hy_b_out[j])
        elif kind == 1:
            w_main, w_lr, w_g, b_g = _gla_weights(gla_w_in[j], gla_w_gate[j], gla_b_gate[j], D)
            proj = [_gla_in(s_h, s_m[1], s_m[0], w_main, w_lr, w_g, b_g, s_len) for s_h, s_m, s_len in streams_in]
            dvh = D // GLA_HEADS
            dkh = (D // 2) // GLA_HEADS
            zero_state = jnp.zeros((B, GLA_HEADS, dvh, dkh), F32)
            if ctx_any:
                ocf, ocb, s_f, s_b = _gla_scan(*proj[1], zero_state, zero_state, B, Lc)
            else:
                s_f = s_b = zero_state
            o_f, o_b, _, _ = _gla_scan(*proj[0], s_f, s_b, B, L)
            gla_specs = lambda tm, d: [pl.BlockSpec((tm, d), lambda t: (t, 0))] * 3 + [pl.BlockSpec((1, d), lambda t: (0, 0))]
            pre.append(("gla", (o_f, o_b, proj[0][2], row2(gla_norm_g[j])), gla_specs))
            if ctx_full:
                pre.append(("gla", (ocf, ocb, proj[1][2], row2(gla_norm_g[j])), gla_specs))
            w_out, b_out = gla_w_out[j].astype(BF16), jnp.zeros((1, D), F32)
        else:
            w_in = gm_w_in[j].astype(BF16)
            ws = gm_ws[j].astype(BF16)
            bs_exp = jnp.repeat(gm_bs[j].T, D // GM_HEADS, axis=1)
            gm_specs = lambda tm, d: [pl.BlockSpec((tm, d), lambda t: (t, 0))] * 2 + [
                pl.BlockSpec((GM_HEADS, GM_CHUNK, GM_CHUNK), lambda t: (0, 0, 0)), pl.BlockSpec((GM_CHUNK, d), lambda t: (0, 0))]
            for s_h, s_m, s_len in streams_in[:1 + int(ctx_full)]:
                u, vn = _gm_in(s_h, s_m[1], s_m[0], w_in, row2(gm_b_in[j]), row2(gm_ln_g[j]), row2(gm_ln_b[j]), s_len)
                pre.append(("gmlp", (u, vn, ws, bs_exp), gm_specs))
            w_out, b_out = gm_w_out[j].astype(BF16), row2(gm_b_out[j])

        moe_streams = []
        cnt, xs = zero_cnt, None
        cap = sum(s_h.shape[0] for (s_h, _, _), _ in zip(streams_in, pre)) + MOE_ROWS
        for idx, ((s_h, s_m, s_len), (pk, pargs, pspecs)) in enumerate(zip(streams_in, pre)):
            h1, meta, cnt, xs = _post(pk, pargs, pspecs, s_h, w_out, b_out, s_m[2], row2(ln_g[i, 0]),
                                      row2(ln_b[i, 0]), s_m[4], s_m[3], rhl, rb_b, cnt, xs, cap,
                                      idx == len(pre) - 1, s_len, alpha)
            moe_streams.append(dict(h1=h1, meta=meta, g2=s_m[5], lg=row2(ln_g[i, 1]), lb=row2(ln_b[i, 1]),
                                    seq=s_len))
        outs = _moe(moe_streams, cnt, xs, cap, moe_w1, moe_w3, moe_w2, i, alpha)
        h = outs[0]
        if ctx_full:
            hc = outs[1]
    return h.reshape(B, L, D)
```
